```python
import math
import jax, jax.numpy as jnp
from jax import lax
import numpy as np

D_MODEL = 4096
BATCH = 4
SEQ = 2048
DEPTH = 2

N_META = 16
NORM_EPS = 1e-6
POOL_WIDTH = D_MODEL // 2
POOL_WINDOWS = (2, 4, 8, 16)
POOL_GROUP = POOL_WIDTH // len(POOL_WINDOWS)
RWKV_WIDTH = D_MODEL // 2
RWKV_HEAD = 64
RWKV_HEADS = RWKV_WIDTH // RWKV_HEAD
RWKV_W_RANK = 96
RWKV_A_RANK = 96
RWKV_G_RANK = 256
RWKV_GN_EPS = 64e-5
RWKV_SPLITS = (RWKV_WIDTH, 2 * RWKV_WIDTH, 3 * RWKV_WIDTH,
               3 * RWKV_WIDTH + RWKV_W_RANK, 3 * RWKV_WIDTH + RWKV_W_RANK + RWKV_A_RANK)
SHIFT_COLS = 3 * RWKV_WIDTH + RWKV_W_RANK + RWKV_A_RANK + RWKV_G_RANK
IN_EVEN_COLS = POOL_WIDTH + SHIFT_COLS
SSM_WIDTH = D_MODEL // 2
SSM_GROUP = 16
SSM_GROUPS = SSM_WIDTH // SSM_GROUP
SSM_STATE = 64
SSM_DT_MIN = 1e-3
SSM_DT_MAX = 1e-1
PEER_KEYS = 128
PEER_EXPERTS = PEER_KEYS * PEER_KEYS
PEER_HEADS = 8
PEER_TOPK = 16
PEER_QHALF = 128
PEER_QDIM = 2 * PEER_QHALF
TOKEN_BLOCK = 64

N_EVEN = (DEPTH + 1) // 2
N_ODD = DEPTH // 2

kernel_name = "hybrid_pool_rwkv7_s5_peer_meta"


def rms_norm(x, gain):
    xf = x.astype(jnp.float32)
    inv = lax.rsqrt(jnp.mean(xf * xf, axis=-1, keepdims=True) + NORM_EPS)
    return (xf * inv).astype(x.dtype) * gain


def shift_prev(z):
    return jnp.pad(z, ((0, 0), (1, 0), (0, 0)))[:, :-1]


def pool_mixer(z, pool_w, pool_scale):
    bsz, L, _ = z.shape
    zg = z.astype(jnp.float32).reshape(bsz, L, len(POOL_WINDOWS), POOL_GROUP)
    cs = jnp.cumsum(zg, axis=1)
    pos1 = jnp.arange(1, L + 1, dtype=jnp.float32)
    outs = []
    for i, w in enumerate(POOL_WINDOWS):
        c = cs[:, :, i]
        lag = jnp.pad(c, ((0, 0), (w, 0), (0, 0)))[:, :L]
        cnt = jnp.minimum(pos1, float(w))[None, :, None]
        outs.append((c - lag) / cnt - zg[:, :, i])
    m = jnp.stack(outs, axis=2)
    y = jnp.einsum('blgc,gcd->blgd', m, pool_w.astype(jnp.float32))
    return (y.reshape(bsz, L, POOL_WIDTH) * pool_scale).astype(z.dtype)


def rwkv7_mixer(z, shift_mu, w0, w2, a0, a2, g2, k_k, k_a, r_k, gn_gain, gn_bias):
    bsz, L, _ = z.shape
    f32 = jnp.float32
    z = z + (shift_prev(z) - z) * shift_mu
    r, k, v, w_lr, a_lr, g_lr = jnp.split(z, RWKV_SPLITS, axis=-1)
    w = -jax.nn.softplus(-(w0 + jnp.tanh(w_lr) @ w2).astype(f32)) - 0.5
    decay = jnp.exp(-jnp.exp(w))
    a = jax.nn.sigmoid(a0 + a_lr @ a2)
    g = jax.nn.sigmoid(g_lr) @ g2
    heads = lambda t: t.reshape(bsz, L, RWKV_HEADS, RWKV_HEAD).astype(f32)
    kk = heads(k * k_k)
    kk = kk / jnp.maximum(jnp.sqrt(jnp.sum(kk * kk, axis=-1, keepdims=True)), 1e-12)
    k = k * (1 + (a - 1) * k_a)
    r_h, k_h, v_h, a_h, w_h = heads(r), heads(k), heads(v), heads(a), heads(decay)

    def step(S, inp):
        r_t, w_t, k_t, v_t, kk_t, a_t = inp
        sa = jnp.einsum('bhij,bhj->bhi', S, -kk_t)
        S = (S * w_t[:, :, None, :] + sa[..., None] * (kk_t * a_t)[:, :, None, :]
             + v_t[..., None] * k_t[:, :, None, :])
        return S, jnp.einsum('bhij,bhj->bhi', S, r_t)

    S0 = jnp.zeros((bsz, RWKV_HEADS, RWKV_HEAD, RWKV_HEAD), f32)
    xs = tuple(jnp.moveaxis(t, 1, 0) for t in (r_h, w_h, k_h, v_h, kk, a_h))
    _, y = lax.scan(step, S0, xs)
    y = jnp.moveaxis(y, 0, 1)
    mean = jnp.mean(y, axis=-1, keepdims=True)
    var = jnp.mean(jnp.square(y - mean), axis=-1, keepdims=True)
    y = ((y - mean) * lax.rsqrt(var + RWKV_GN_EPS)).reshape(bsz, L, RWKV_WIDTH) * gn_gain + gn_bias
    bonus = jnp.sum(r_h * k_h * r_k, axis=-1, keepdims=True) * v_h
    y = (y + bonus.reshape(bsz, L, RWKV_WIDTH)) * g.astype(f32)
    return y.astype(z.dtype)


def _complex_affine_combine(e1, e2):
    a1r, a1i, b1r, b1i = e1
    a2r, a2i, b2r, b2i = e2
    return (a2r * a1r - a2i * a1i,
            a2r * a1i + a2i * a1r,
            a2r * b1r - a2i * b1i + b2r,
            a2r * b1i + a2i * b1r + b2i)


def s5_mixer(u, lam_re, lam_im, log_dt, b_re, b_im, c_re, c_im, d_skip):
    bsz, L, _ = u.shape
    f32 = jnp.float32
    uf = u.astype(f32).reshape(bsz, L, SSM_GROUPS, SSM_GROUP)
    lr, li = lam_re.astype(f32), lam_im.astype(f32)
    dt = jnp.exp(log_dt.astype(f32))[:, None]
    mag = jnp.exp(lr * dt)
    ar, ai = mag * jnp.cos(li * dt), mag * jnp.sin(li * dt)
    den = lr * lr + li * li
    fr = ((ar - 1.0) * lr + ai * li) / den
    fi = (ai * lr - (ar - 1.0) * li) / den
    br, bi = b_re.astype(f32), b_im.astype(f32)
    bbar_re = fr[..., None] * br - fi[..., None] * bi
    bbar_im = fr[..., None] * bi + fi[..., None] * br
    bu_re = jnp.einsum('blgc,gpc->blgp', uf, bbar_re)
    bu_im = jnp.einsum('blgc,gpc->blgp', uf, bbar_im)
    a_re = jnp.broadcast_to(ar, (1, L) + ar.shape)
    a_im = jnp.broadcast_to(ai, (1, L) + ai.shape)
    _, _, x_re, x_im = lax.associative_scan(_complex_affine_combine, (a_re, a_im, bu_re, bu_im), axis=1)
    y = (jnp.einsum('gcp,blgp->blgc', c_re.astype(f32), x_re)
         - jnp.einsum('gcp,blgp->blgc', c_im.astype(f32), x_im)
         + uf * d_skip.astype(f32).reshape(SSM_GROUPS, SSM_GROUP))
    return y.reshape(bsz, L, SSM_WIDTH).astype(u.dtype)


def peer_ffn(h, w_q, sub_keys, u_emb, v_emb):
    bsz, L, D = h.shape
    f32 = jnp.float32
    q = (h @ w_q).reshape(bsz, L, PEER_HEADS, 2, PEER_QHALF).astype(f32)
    s = jnp.einsum('blhpd,hpnd->blhpn', q, sub_keys.astype(f32))
    s1, i1 = lax.top_k(s[..., 0, :], PEER_TOPK)
    s2, i2 = lax.top_k(s[..., 1, :], PEER_TOPK)
    n_cand = PEER_TOPK * PEER_TOPK
    cand_s = (s1[..., :, None] + s2[..., None, :]).reshape(bsz, L, PEER_HEADS, n_cand)
    cand_id = (i1[..., :, None] * PEER_KEYS + i2[..., None, :]).reshape(bsz, L, PEER_HEADS, n_cand)
    top_s, pos = lax.top_k(cand_s, PEER_TOPK)
    eid = jnp.take_along_axis(cand_id, pos, axis=-1)
    gate = jax.nn.softmax(top_s, axis=-1).astype(h.dtype)

    n_tok = bsz * L
    pad = (-n_tok) % TOKEN_BLOCK
    n_blk = (n_tok + pad) // TOKEN_BLOCK

    def blockify(t):
        t = t.reshape((n_tok,) + t.shape[2:])
        t = jnp.pad(t, [(0, pad)] + [(0, 0)] * (t.ndim - 1))
        return t.reshape((n_blk, TOKEN_BLOCK) + t.shape[1:])

    def expert_block(args):
        hb, eb, gb = args
        act = jax.nn.gelu(jnp.einsum('cd,chkd->chk', hb, u_emb[eb]), approximate=False)
        return jnp.einsum('chk,chkd->cd', gb * act, v_emb[eb])

    out = lax.map(expert_block, (blockify(h), blockify(eid), blockify(gate)))
    return out.reshape(n_blk * TOKEN_BLOCK, D)[:n_tok].reshape(bsz, L, D)


def setup_inputs(seed: int = 0) -> dict:
    key = jax.random.key(seed)
    ks = iter(jax.random.split(key, 48))
    f32 = jnp.float32
    nrm = lambda shape, scale: scale * jax.random.normal(next(ks), shape, f32)
    gain = lambda shape: 1.0 + nrm(shape, 0.02)
    uni = lambda shape, lo, hi: jax.random.uniform(next(ks), shape, f32, minval=lo, maxval=hi)
    G, P = SSM_GROUPS, SSM_STATE
    return {
        "x": nrm((BATCH, SEQ, D_MODEL), 1.0),
        "meta_tokens": nrm((N_META, D_MODEL), 1.0),
        "mix_norm_gain": gain((DEPTH, D_MODEL)),
        "ffn_norm_gain": gain((DEPTH, D_MODEL)),
        "final_norm_gain": gain((D_MODEL,)),
        "w_in_even": nrm((N_EVEN, D_MODEL, IN_EVEN_COLS), D_MODEL ** -0.5),
        "pool_w": nrm((N_EVEN, len(POOL_WINDOWS), POOL_GROUP, POOL_GROUP), POOL_GROUP ** -0.5),
        "pool_scale": gain((N_EVEN, POOL_WIDTH)),
        "shift_mu": uni((N_EVEN, SHIFT_COLS), 0.0, 1.0),
        "rwkv_w0": uni((N_EVEN, RWKV_WIDTH), -4.0, 0.0),
        "rwkv_w2": nrm((N_EVEN, RWKV_W_RANK, RWKV_WIDTH), 0.5 * RWKV_W_RANK ** -0.5),
        "rwkv_a0": nrm((N_EVEN, RWKV_WIDTH), 0.1),
        "rwkv_a2": nrm((N_EVEN, RWKV_A_RANK, RWKV_WIDTH), 0.5 * RWKV_A_RANK ** -0.5),
        "rwkv_g2": nrm((N_EVEN, RWKV_G_RANK, RWKV_WIDTH), RWKV_G_RANK ** -0.5),
        "rwkv_k_k": 0.85 + nrm((N_EVEN, RWKV_WIDTH), 0.05),
        "rwkv_k_a": 1.0 + nrm((N_EVEN, RWKV_WIDTH), 0.05),
        "rwkv_r_k": nrm((N_EVEN, RWKV_HEADS, RWKV_HEAD), 0.1),
        "rwkv_gn_gain": gain((N_EVEN, RWKV_WIDTH)),
        "rwkv_gn_bias": nrm((N_EVEN, RWKV_WIDTH), 0.02),
        "w_out_even": nrm((N_EVEN, POOL_WIDTH + RWKV_WIDTH, D_MODEL), (POOL_WIDTH + RWKV_WIDTH) ** -0.5),
        "w_in_odd": nrm((N_ODD, D_MODEL, SSM_WIDTH), D_MODEL ** -0.5),
        "ssm_lam_re": -0.5 + nrm((N_ODD, G, P), 0.01),
        "ssm_lam_im": jnp.pi * jnp.arange(P, dtype=f32) + nrm((N_ODD, G, P), 0.01),
        "ssm_log_dt": uni((N_ODD, G), math.log(SSM_DT_MIN), math.log(SSM_DT_MAX)),
        "ssm_b_re": nrm((N_ODD, G, P, SSM_GROUP), (2 * SSM_GROUP) ** -0.5),
        "ssm_b_im": nrm((N_ODD, G, P, SSM_GROUP), (2 * SSM_GROUP) ** -0.5),
        "ssm_c_re": nrm((N_ODD, G, SSM_GROUP, P), P ** -0.5),
        "ssm_c_im": nrm((N_ODD, G, SSM_GROUP, P), P ** -0.5),
        "ssm_d": nrm((N_ODD, SSM_WIDTH), 1.0),
        "w_glu": nrm((N_ODD, SSM_WIDTH, 2 * D_MODEL), SSM_WIDTH ** -0.5),
        "peer_w_q": nrm((DEPTH, D_MODEL, PEER_HEADS * PEER_QDIM), D_MODEL ** -0.5),
        "peer_sub_keys": nrm((DEPTH, PEER_HEADS, 2, PEER_KEYS, PEER_QHALF), PEER_QHALF ** -0.5),
        "peer_u": nrm((DEPTH, PEER_EXPERTS, D_MODEL), D_MODEL ** -0.5),
        "peer_v": nrm((DEPTH, PEER_EXPERTS, D_MODEL), PEER_HEADS ** -0.5),
    }


def reference(x, meta_tokens, mix_norm_gain, ffn_norm_gain, final_norm_gain,
              w_in_even, pool_w, pool_scale, shift_mu, rwkv_w0, rwkv_w2, rwkv_a0, rwkv_a2,
              rwkv_g2, rwkv_k_k, rwkv_k_a, rwkv_r_k, rwkv_gn_gain, rwkv_gn_bias, w_out_even,
              w_in_odd, ssm_lam_re, ssm_lam_im, ssm_log_dt, ssm_b_re, ssm_b_im, ssm_c_re,
              ssm_c_im, ssm_d, w_glu, peer_w_q, peer_sub_keys, peer_u, peer_v):
    bsz = x.shape[0]
    meta = jnp.broadcast_to(meta_tokens[None].astype(x.dtype), (bsz, N_META, D_MODEL))
    h = jnp.concatenate([meta, x], axis=1)
    for layer in range(DEPTH):
        hn = rms_norm(h, mix_norm_gain[layer])
        if layer % 2 == 0:
            i = layer // 2
            z = hn @ w_in_even[i]
            y_pool = pool_mixer(z[..., :POOL_WIDTH], pool_w[i], pool_scale[i])
            y_rwkv = rwkv7_mixer(z[..., POOL_WIDTH:], shift_mu[i], rwkv_w0[i], rwkv_w2[i],
                                 rwkv_a0[i], rwkv_a2[i], rwkv_g2[i], rwkv_k_k[i], rwkv_k_a[i],
                                 rwkv_r_k[i], rwkv_gn_gain[i], rwkv_gn_bias[i])
            h = h + jnp.concatenate([y_pool, y_rwkv], axis=-1) @ w_out_even[i]
        else:
            j = layer // 2
            u = hn @ w_in_odd[j]
            y = jax.nn.gelu(s5_mixer(u, ssm_lam_re[j], ssm_lam_im[j], ssm_log_dt[j], ssm_b_re[j],
                                     ssm_b_im[j], ssm_c_re[j], ssm_c_im[j], ssm_d[j]),
                            approximate=False)
            ga, gb = jnp.split(y @ w_glu[j], 2, axis=-1)
            h = h + ga * jax.nn.sigmoid(gb)
        h = h + peer_ffn(rms_norm(h, ffn_norm_gain[layer]), peer_w_q[layer], peer_sub_keys[layer],
                         peer_u[layer], peer_v[layer])
    return rms_norm(h, final_norm_gain)[:, N_META:]
```

```python
import functools

import jax
import jax.numpy as jnp
from jax import lax
from jax.experimental import pallas as pl
from jax.experimental.pallas import tpu as pltpu

F32 = jnp.float32
BF16 = jnp.bfloat16

NORM_EPS = 1e-6
N_META = 16
POOL_WINDOWS = (2, 4, 8, 16)
RWKV_HEAD = 64
RWKV_W_RANK = 96
RWKV_A_RANK = 96
RWKV_G_RANK = 256
RWKV_GN_EPS = 64e-5
RWKV_CHUNK = 64
SSM_GROUP = 16
SSM_STATE = 64
SSM_CHUNK = 16
PEER_KEYS = 128
PEER_HEADS = 8
PEER_TOPK = 16
LANES = 128
SEQ_ALIGN = 64

NEG_INF = float("-inf")


def _params(sem, vmem_mb):
    return pltpu.CompilerParams(dimension_semantics=sem, vmem_limit_bytes=vmem_mb << 20)


def _split2(x):
    hi = x.astype(BF16)
    lo = (x - hi.astype(F32)).astype(BF16)
    return hi, lo


def _split3(x):
    hi = x.astype(BF16)
    r = x - hi.astype(F32)
    mid = r.astype(BF16)
    lo = (r - mid.astype(F32)).astype(BF16)
    return hi, mid, lo


def _dot(a, b):
    return jnp.dot(a, b, preferred_element_type=F32)


def _dot_nt(a, b):
    return lax.dot_general(a, b, (((1,), (1,)), ((), ())), preferred_element_type=F32)


def _dot_acc(a, b):
    ah, al = _split2(a)
    bh, bl = _split2(b)
    return _dot(ah, bh) + _dot(al, bh) + _dot(ah, bl)


def _dot_nt_acc(a, b):
    ah, al = _split2(a)
    bh, bl = _split2(b)
    return _dot_nt(ah, bh) + _dot_nt(al, bh) + _dot_nt(ah, bl)


def _gelu(x):
    return 0.5 * x * (1.0 + lax.erf(x * 0.7071067811865476))


def _sigmoid(x):
    return 1.0 / (1.0 + jnp.exp(-x))


def _pick_tile(n, prefs):
    for t in prefs:
        if n % t == 0:
            return t
    return n


def _rms_mm_body(x_ref, g_ref, w_ref, o_ref, *rest, with_xn):
    if with_xn:
        xn_out_ref, xn_ref = rest
    else:
        (xn_ref,) = rest

    @pl.when(pl.program_id(1) == 0)
    def _():
        x = x_ref[...]
        inv = lax.rsqrt(jnp.mean(x * x, axis=-1, keepdims=True) + NORM_EPS)
        xn = ((x * inv) * g_ref[...]).astype(BF16)
        xn_ref[...] = xn
        if with_xn:
            xn_out_ref[...] = xn

    o_ref[...] = _dot(xn_ref[...], w_ref[...]).astype(o_ref.dtype)


def rms_matmul(x, gain, w, *, out_dtype=F32, with_xn=False):
    t, d = x.shape
    n = w.shape[1]
    tm = _pick_tile(t, (384, 256, 128, 64, 32, 16, 8))
    tn = _pick_tile(n, (512, 256, 128))
    out_shape = [jax.ShapeDtypeStruct((t, n), out_dtype)]
    out_specs = [pl.BlockSpec((tm, tn), lambda i, j: (i, j))]
    if with_xn:
        out_shape.append(jax.ShapeDtypeStruct((t, d), BF16))
        out_specs.append(pl.BlockSpec((tm, d), lambda i, j: (i, 0)))
    res = pl.pallas_call(
        functools.partial(_rms_mm_body, with_xn=with_xn),
        grid=(t // tm, n // tn),
        in_specs=[
            pl.BlockSpec((tm, d), lambda i, j: (i, 0)),
            pl.BlockSpec((1, d), lambda i, j: (0, 0)),
            pl.BlockSpec((d, tn), lambda i, j: (0, j)),
        ],
        out_specs=out_specs,
        out_shape=out_shape,
        scratch_shapes=[pltpu.VMEM((tm, d), BF16)],
        compiler_params=_params(("parallel", "arbitrary"), 56),
        name="rms_matmul",
    )(x, gain.reshape(1, d), w)
    return res if with_xn else res[0]


def _proj2_body(a1_ref, a2_ref, w1_ref, w2_ref, h_ref, o_ref):
    o_ref[...] = h_ref[...] + _dot(a1_ref[...], w1_ref[...]) + _dot(a2_ref[...], w2_ref[...])


def proj2_residual(a1, a2, w, h):
    t, k1 = a1.shape
    k2 = a2.shape[1]
    n = w.shape[1]
    assert k1 == k2
    tm = _pick_tile(t, (768, 512, 384, 256, 128, 64, 32, 16))
    tn = _pick_tile(n, (512, 256, 128))
    return pl.pallas_call(
        _proj2_body,
        grid=(t // tm, n // tn),
        in_specs=[
            pl.BlockSpec((tm, k1), lambda i, j: (i, 0)),
            pl.BlockSpec((tm, k2), lambda i, j: (i, 0)),
            pl.BlockSpec((k1, tn), lambda i, j: (0, j)),
            pl.BlockSpec((k2, tn), lambda i, j: (1, j)),
            pl.BlockSpec((tm, tn), lambda i, j: (i, j)),
        ],
        out_specs=pl.BlockSpec((tm, tn), lambda i, j: (i, j)),
        out_shape=jax.ShapeDtypeStruct((t, n), F32),
        compiler_params=_params(("parallel", "arbitrary"), 56),
        name="proj2_residual",
    )(a1, a2, w, w, h)


def _glu_body(y_ref, wa_ref, wb_ref, h_ref, o_ref):
    y = y_ref[...]
    ga = _dot(y, wa_ref[...])
    gb = _dot(y, wb_ref[...])
    o_ref[...] = h_ref[...] + ga * _sigmoid(gb)


def glu_residual(y, w, h):
    t, k = y.shape
    n = w.shape[1] // 2
    tm = _pick_tile(t, (768, 512, 384, 256, 128, 64, 32, 16))
    tn = _pick_tile(n, (512, 256, 128))
    nb = n // tn
    return pl.pallas_call(
        _glu_body,
        grid=(t // tm, nb),
        in_specs=[
            pl.BlockSpec((tm, k), lambda i, j: (i, 0)),
            pl.BlockSpec((k, tn), lambda i, j: (0, j)),
            pl.BlockSpec((k, tn), lambda i, j: (0, j + nb)),
            pl.BlockSpec((tm, tn), lambda i, j: (i, j)),
        ],
        out_specs=pl.BlockSpec((tm, tn), lambda i, j: (i, j)),
        out_shape=jax.ShapeDtypeStruct((t, n), F32),
        compiler_params=_params(("parallel", "arbitrary"), 56),
        name="glu_residual",
    )(y, w, w, h)


def _pool_body(z_ref, w_ref, s_ref, o_ref):
    grp = pl.program_id(1)
    z = z_ref[...]
    length = z.shape[0]
    row = lax.broadcasted_iota(jnp.int32, (length, 1), 0)
    posf = (row + 1).astype(F32)

    def shifted(x, k):
        return jnp.where(row >= k, pltpu.roll(x, k, axis=0), 0.0)

    for gi, win in enumerate(POOL_WINDOWS):

        @pl.when(grp == gi)
        def _(win=win):
            acc = z
            span = 1
            while span < win:
                acc = acc + shifted(acc, span)
                span *= 2
            cnt = jnp.minimum(posf, float(win))
            m = acc / cnt - z
            y = _dot(m.astype(BF16), w_ref[...])
            o_ref[...] = (y * s_ref[...]).astype(o_ref.dtype)


def pool_mixer(z3, pool_w, pool_scale):
    bsz, length, _ = z3.shape
    ng, c, _ = pool_w.shape
    return pl.pallas_call(
        _pool_body,
        grid=(bsz, ng),
        in_specs=[
            pl.BlockSpec((None, length, c), lambda b, g: (b, 0, g)),
            pl.BlockSpec((None, c, c), lambda b, g: (g, 0, 0)),
            pl.BlockSpec((1, c), lambda b, g: (0, g)),
        ],
        out_specs=pl.BlockSpec((None, length, c), lambda b, g: (b, 0, g)),
        out_shape=jax.ShapeDtypeStruct((bsz, length, ng * c), BF16),
        compiler_params=_params(("parallel", "arbitrary"), 48),
        name="pool_mixer",
    )(z3, pool_w.astype(BF16), pool_scale.reshape(1, ng * c))


def _head_sum_matrix(width, head):
    shift = head.bit_length() - 1
    r = lax.broadcasted_iota(jnp.int32, (width, width), 0) >> shift
    c = lax.broadcasted_iota(jnp.int32, (width, width), 1) >> shift
    return jnp.where(r == c, 1.0, 0.0).astype(BF16)


def _head_sums(x, ones_bd):
    hi, lo = _split2(x)
    return _dot(hi, ones_bd) + _dot(lo, ones_bd)


def _rwkv_prep_body(zr_ref, zk_ref, zv_ref, zl_ref, pr_ref, pk_ref, pv_ref, pl_ref,
                    mur_ref, muk_ref, muv_ref, mul_ref, w0_ref, w2_ref, a0_ref, a2_ref, g2_ref,
                    kk_ref, ka_ref,
                    r_out, k_out, v_out, kap_out, b_out, lw_out, g_out):
    first = pl.program_id(1) == 0
    tl = zr_ref.shape[0]
    row = lax.broadcasted_iota(jnp.int32, (tl, 1), 0)

    def mix(z_ref, p_ref, mu_ref):
        z = z_ref[...]
        prev_last = jnp.where(first, 0.0, p_ref[7:8, :])
        prev = jnp.where(row == 0, prev_last, pltpu.roll(z, 1, axis=0))
        return z + (prev - z) * mu_ref[...]

    r = mix(zr_ref, pr_ref, mur_ref)
    k = mix(zk_ref, pk_ref, muk_ref)
    v = mix(zv_ref, pv_ref, muv_ref)
    lora = mix(zl_ref, pl_ref, mul_ref)
    w_lr = lora[:, 0:LANES]
    a_lr = lora[:, LANES:2 * LANES]
    g_lr = lora[:, 2 * LANES:]

    x = w0_ref[...] + _dot_acc(jnp.tanh(w_lr), w2_ref[...])
    y = -x
    softplus = jnp.maximum(y, 0.0) + jnp.log(1.0 + jnp.exp(-jnp.abs(y)))
    lw = -jnp.exp(-softplus - 0.5)
    a = _sigmoid(a0_ref[...] + _dot_acc(a_lr, a2_ref[...]))
    g = _dot_acc(_sigmoid(g_lr), g2_ref[...])

    ones_bd = _head_sum_matrix(LANES, RWKV_HEAD)
    kk = k * kk_ref[...]
    kk2 = kk * kk
    width = k.shape[1]
    ss = jnp.concatenate(
        [_head_sums(kk2[:, c:c + LANES], ones_bd) for c in range(0, width, LANES)], axis=1)
    kap = kk / jnp.maximum(jnp.sqrt(ss), 1e-12)

    r_out[...] = r
    k_out[...] = k * (1.0 + (a - 1.0) * ka_ref[...])
    v_out[...] = v
    kap_out[...] = kap
    b_out[...] = kap * a
    lw_out[...] = lw
    g_out[...] = g


def rwkv_prep(z3, off_blocks, mu, w0, w2p, a0, a2p, g2, k_k, k_a):
    bsz, length, _ = z3.shape
    width = w0.shape[-1]
    cw = 512
    ncb = width // cw
    tl = _pick_tile(length, (352, 192, 176, 128, 64))
    ob_r, ob_k, ob_v, ob_l = off_blocks
    t8 = tl // 8

    def cur(ob, fixed=False):
        if fixed:
            return pl.BlockSpec((None, tl, cw), lambda b, i, c: (b, i, ob))
        return pl.BlockSpec((None, tl, cw), lambda b, i, c: (b, i, ob + c))

    def prev(ob, fixed=False):
        if fixed:
            return pl.BlockSpec((None, 8, cw), lambda b, i, c: (b, jnp.maximum(i * t8 - 1, 0), ob))
        return pl.BlockSpec((None, 8, cw), lambda b, i, c: (b, jnp.maximum(i * t8 - 1, 0), ob + c))

    def vec(ob, fixed=False):
        if fixed:
            return pl.BlockSpec((1, cw), lambda b, i, c: (0, ob))
        return pl.BlockSpec((1, cw), lambda b, i, c: (0, ob + c))

    colvec = pl.BlockSpec((1, cw), lambda b, i, c: (0, c))
    out_spec = pl.BlockSpec((None, tl, cw), lambda b, i, c: (b, i, c))
    out_sds = jax.ShapeDtypeStruct((bsz, length, width), F32)
    return pl.pallas_call(
        _rwkv_prep_body,
        grid=(bsz, length // tl, ncb),
        in_specs=[
            cur(ob_r), cur(ob_k), cur(ob_v), cur(ob_l, True),
            prev(ob_r), prev(ob_k), prev(ob_v), prev(ob_l, True),
            vec(ob_r), vec(ob_k), vec(ob_v), vec(ob_l, True),
            colvec,
            pl.BlockSpec((LANES, cw), lambda b, i, c: (0, c)),
            colvec,
            pl.BlockSpec((LANES, cw), lambda b, i, c: (0, c)),
            pl.BlockSpec((RWKV_G_RANK, cw), lambda b, i, c: (0, c)),
            colvec, colvec,
        ],
        out_specs=[out_spec] * 7,
        out_shape=[out_sds] * 7,
        compiler_params=_params(("parallel", "parallel", "arbitrary"), 56),
        name="rwkv_prep",
    )(z3, z3, z3, z3, z3, z3, z3, z3, mu, mu, mu, mu,
      w0.reshape(1, width), w2p, a0.reshape(1, width), a2p, g2,
      k_k.reshape(1, width), k_a.reshape(1, width))


def _rwkv_scan_body(r_ref, k_ref, v_ref, kap_ref, b_ref, lw_ref, g_ref, rk_ref, gain_ref, bias_ref,
                    o_ref, h_ref, *, pairs):
    cn = RWKV_CHUNK
    hd = RWKV_HEAD

    @pl.when(pl.program_id(2) == 0)
    def _():
        h_ref[...] = jnp.zeros_like(h_ref)

    lane = lax.broadcasted_iota(jnp.int32, (1, LANES), 1)
    m0 = lane < hd
    rowi = lax.broadcasted_iota(jnp.int32, (cn, LANES), 0)
    coli = lax.broadcasted_iota(jnp.int32, (cn, LANES), 1) & (hd - 1)
    strict = rowi > coli
    incl = rowi >= coli
    eye2 = jnp.where(rowi == coli, 1.0, 0.0)
    tri = jnp.where(lax.broadcasted_iota(jnp.int32, (cn, cn), 0)
                    >= lax.broadcasted_iota(jnp.int32, (cn, cn), 1), 1.0, 0.0).astype(BF16)
    br = lax.broadcasted_iota(jnp.int32, (LANES, LANES), 0) >> (hd.bit_length() - 1)
    bc = lax.broadcasted_iota(jnp.int32, (LANES, LANES), 1) >> (hd.bit_length() - 1)
    bd_mask = br == bc
    ones_bd = jnp.where(bd_mask, 1.0, 0.0).astype(BF16)

    def stk(x):
        return jnp.concatenate([jnp.where(m0, x, 0.0), jnp.where(m0, 0.0, x)], axis=0).astype(BF16)

    for p in range(pairs):
        sl = slice(p * LANES, (p + 1) * LANES)
        r = r_ref[:, sl]
        k = k_ref[:, sl]
        v = v_ref[:, sl]
        kap = kap_ref[:, sl]
        b = b_ref[:, sl]
        lw = lw_ref[:, sl]
        hbd = h_ref[p]

        l1, l2, l3 = _split3(lw)
        lp = _dot(tri, l1) + _dot(tri, l2) + _dot(tri, l3)
        lpc = lp[cn - 1:cn, :]
        pinv = jnp.exp(-lp)
        pend = jnp.exp(lpc - lp)
        kt = kap * jnp.exp(lp - lw)
        rt = r * jnp.exp(lp)
        kh = k * pinv
        bh = b * pinv
        khc = k * pend
        bhc = b * pend

        ktrt = jnp.concatenate([kt, rt], axis=0).astype(BF16)
        ab_all = _dot_nt(ktrt, stk(bh))
        ak_all = _dot_nt(ktrt, stk(kh))
        n_mat = jnp.where(strict, ab_all[:cn], 0.0)
        a_k = jnp.where(strict, ak_all[:cn], 0.0)
        a_rb = jnp.where(incl, ab_all[cn:], 0.0)
        a_rk = jnp.where(incl, ak_all[cn:], 0.0)

        t_mat = eye2 - n_mat
        pw = n_mat
        span = 1
        while 2 * span < cn:
            pw = _dot(pw.astype(BF16), stk(pw))
            t_mat = t_mat + _dot(t_mat.astype(BF16), stk(pw))
            span *= 2
        t_bf = t_mat.astype(BF16)

        kbar = _dot(t_bf, stk(kt))
        akv = _dot(a_k.astype(BF16), stk(v))
        ubar = _dot(t_bf, stk(akv))
        hb = hbd.astype(BF16)
        u = _dot(kbar.astype(BF16), hb) + ubar
        y = (_dot(rt.astype(BF16), hb) + _dot(a_rk.astype(BF16), stk(v))
             - _dot(a_rb.astype(BF16), stk(u)))

        kb_t = jnp.concatenate([khc, -bhc], axis=0).T
        vu = jnp.concatenate([v, u], axis=0).astype(BF16)
        g_mat = _dot(kb_t.astype(BF16), vu)
        pc_col = jnp.exp(jnp.broadcast_to(lpc, (LANES, LANES)).T)
        h_ref[p] = jnp.where(bd_mask, pc_col * hbd + g_mat, 0.0)

        mean = _head_sums(y, ones_bd) * (1.0 / hd)
        yc = y - mean
        var = _head_sums(yc * yc, ones_bd) * (1.0 / hd)
        yn = yc * lax.rsqrt(var + RWKV_GN_EPS) * gain_ref[:, sl] + bias_ref[:, sl]
        bonus = _head_sums(r * k * rk_ref[:, sl], ones_bd) * v
        o_ref[:, sl] = ((yn + bonus) * g_ref[:, sl]).astype(o_ref.dtype)


def rwkv_scan(r, k, v, kap, b, lw, g, r_k, gn_gain, gn_bias):
    bsz, length, width = r.shape
    cw = 512
    pairs = cw // LANES
    seq = pl.BlockSpec((None, RWKV_CHUNK, cw), lambda bi, c, t: (bi, t, c))
    vec = pl.BlockSpec((1, cw), lambda bi, c, t: (0, c))
    return pl.pallas_call(
        functools.partial(_rwkv_scan_body, pairs=pairs),
        grid=(bsz, width // cw, length // RWKV_CHUNK),
        in_specs=[seq] * 7 + [vec] * 3,
        out_specs=seq,
        out_shape=jax.ShapeDtypeStruct((bsz, length, width), BF16),
        scratch_shapes=[pltpu.VMEM((pairs, LANES, LANES), F32)],
        compiler_params=_params(("parallel", "parallel", "arbitrary"), 48),
        name="rwkv_scan",
    )(r, k, v, kap, b, lw, g, r_k.reshape(1, width), gn_gain.reshape(1, width),
      gn_bias.reshape(1, width))


def _s5_param_body(lr_ref, li_ref, ldt_ref, btr_ref, bti_ref, cr_ref, ci_ref,
                   kst_ref, etr_ref, eti_ref, car_ref, cai_ref, a16r_ref, a16i_ref, *, groups):
    nc = SSM_CHUNK
    npow = 24
    tau = lax.broadcasted_iota(jnp.int32, (npow, SSM_STATE), 0).astype(F32)
    for gidx in range(groups):
        lr = lr_ref[gidx]
        li = li_ref[gidx]
        dt = jnp.exp(ldt_ref[gidx])
        mag = jnp.exp(tau * (lr * dt))
        ang = tau * (li * dt)
        pr = mag * jnp.cos(ang)
        pi = mag * jnp.sin(ang)
        ar = pr[1:2]
        ai = pi[1:2]
        den = lr * lr + li * li
        fr = ((ar - 1.0) * lr + ai * li) / den
        fi = (ai * lr - (ar - 1.0) * li) / den
        btr = btr_ref[gidx]
        bti = bti_ref[gidx]
        bbr = fr * btr - fi * bti
        bbi = fr * bti + fi * btr
        cr = cr_ref[gidx]
        ci = ci_ref[gidx]
        etr = jnp.concatenate([pr[s:s + 1] * bbr - pi[s:s + 1] * bbi for s in range(nc)], axis=0)
        eti = jnp.concatenate([pr[s:s + 1] * bbi + pi[s:s + 1] * bbr for s in range(nc)], axis=0)
        car = jnp.concatenate([pr[s:s + 1] * cr - pi[s:s + 1] * ci for s in range(1, nc + 1)], axis=0)
        cai = jnp.concatenate([pr[s:s + 1] * ci + pi[s:s + 1] * cr for s in range(1, nc + 1)], axis=0)
        kst_ref[gidx] = _dot_nt_acc(etr, cr) - _dot_nt_acc(eti, ci)
        etr_ref[gidx] = etr
        eti_ref[gidx] = eti
        car_ref[gidx] = car
        cai_ref[gidx] = cai
        a16r_ref[gidx] = jnp.broadcast_to(pr[nc:nc + 1], (8, SSM_STATE))
        a16i_ref[gidx] = jnp.broadcast_to(pi[nc:nc + 1], (8, SSM_STATE))


def s5_params(lam_re, lam_im, log_dt, b_re, b_im, c_re, c_im):
    ng, ns = lam_re.shape
    gc = SSM_GROUP
    gb = 8
    rows = SSM_CHUNK * gc

    def spec(*dims):
        return pl.BlockSpec((gb,) + dims, lambda i: (i,) + (0,) * len(dims))

    outs = [((rows, gc), F32)] + [((rows, ns), F32)] * 4 + [((8, ns), F32)] * 2
    return pl.pallas_call(
        functools.partial(_s5_param_body, groups=gb),
        grid=(ng // gb,),
        in_specs=[spec(1, ns), spec(1, ns), spec(1, 1), spec(gc, ns), spec(gc, ns), spec(gc, ns),
                  spec(gc, ns)],
        out_specs=[spec(*s) for s, _ in outs],
        out_shape=[jax.ShapeDtypeStruct((ng,) + s, dt) for s, dt in outs],
        compiler_params=_params(("parallel",), 32),
        name="s5_params",
    )(lam_re.reshape(ng, 1, ns), lam_im.reshape(ng, 1, ns), log_dt.reshape(ng, 1, 1),
      jnp.swapaxes(b_re, 1, 2), jnp.swapaxes(b_im, 1, 2), c_re, c_im)


def _s5_body(x_ref, toep_ref, wsr_ref, wsi_ref, wor_ref, woi_ref, a16r_ref, a16i_ref, d_ref,
             o_ref, sre_ref, sim_ref, xre_ref, xim_ref):
    rows = x_ref.shape[1]
    nchunk = rows // 8
    x0 = x_ref[0]
    x1 = x_ref[1]
    sre_ref[...] = _dot(x0, wsr_ref[0]) + _dot(x1, wsr_ref[1])
    sim_ref[...] = _dot(x0, wsi_ref[0]) + _dot(x1, wsi_ref[1])
    ar = a16r_ref[...]
    ai = a16i_ref[...]

    def step(c, carry):
        xr, xi = carry
        off = pl.multiple_of(c * 8, 8)
        xre_ref[pl.ds(off, 8), :] = xr
        xim_ref[pl.ds(off, 8), :] = xi
        sr = sre_ref[pl.ds(off, 8), :]
        si = sim_ref[pl.ds(off, 8), :]
        return ar * xr - ai * xi + sr, ar * xi + ai * xr + si

    zero = jnp.zeros((8, LANES), F32)
    lax.fori_loop(0, nchunk, step, (zero, zero))
    xr = xre_ref[...].astype(BF16)
    xi = xim_ref[...].astype(BF16)
    for q, xq in enumerate((x0, x1)):
        y = (_dot(xq, toep_ref[q]) + _dot(xr, wor_ref[q]) + _dot(xi, woi_ref[q])
             + xq.astype(F32) * d_ref[q])
        o_ref[q] = _gelu(y).astype(o_ref.dtype)


def s5_apply(xg, toep, wsr, wsi, wor, woi, a16r, a16i, dflat):
    ng, rows, cols = xg.shape

    def spec(*dims):
        return pl.BlockSpec((2,) + dims, lambda i: (i,) + (0,) * len(dims))

    return pl.pallas_call(
        _s5_body,
        grid=(ng // 2,),
        in_specs=[spec(rows, cols), spec(cols, cols), spec(cols, LANES), spec(cols, LANES),
                  spec(LANES, cols), spec(LANES, cols),
                  pl.BlockSpec((None, 8, LANES), lambda i: (i, 0, 0)),
                  pl.BlockSpec((None, 8, LANES), lambda i: (i, 0, 0)),
                  spec(1, cols)],
        out_specs=spec(rows, cols),
        out_shape=jax.ShapeDtypeStruct((ng, rows, cols), BF16),
        scratch_shapes=[pltpu.VMEM((rows, LANES), F32)] * 4,
        compiler_params=_params(("parallel",), 32),
        name="s5_apply",
    )(xg, toep, wsr, wsi, wor, woi, a16r, a16i, dflat)


def s5_mixer(u3, lam_re, lam_im, log_dt, b_re, b_im, c_re, c_im, d_skip):
    bsz, length, width = u3.shape
    ng, ns = lam_re.shape
    gc = SSM_GROUP
    nc = SSM_CHUNK
    kst, etr, eti, car, cai, a16r, a16i = s5_params(lam_re, lam_im, log_dt, b_re, b_im, c_re, c_im)

    k4 = kst.reshape(ng, nc, gc, gc)
    s_idx = jnp.arange(nc)[:, None]
    t_idx = jnp.arange(nc)[None, :]
    lag = t_idx - s_idx
    toep = jnp.where((lag >= 0)[None, :, :, None, None], k4[:, jnp.clip(lag, 0, nc - 1)], 0.0)
    toep = toep.transpose(0, 1, 3, 2, 4).reshape(ng, nc * gc, nc * gc).astype(BF16)
    odd = (jnp.arange(ng) % 2 == 1)[:, None, None]

    def lanes_by_parity(w):
        z = jnp.zeros_like(w)
        return jnp.where(odd, jnp.concatenate([z, w], -1), jnp.concatenate([w, z], -1))

    def flip_lag(e):
        return e.reshape(ng, nc, gc, ns)[:, ::-1].reshape(ng, nc * gc, ns)

    wsr = lanes_by_parity(flip_lag(etr)).astype(BF16)
    wsi = lanes_by_parity(flip_lag(eti)).astype(BF16)
    wor = jnp.swapaxes(lanes_by_parity(car), 1, 2).astype(BF16)
    woi = jnp.swapaxes(lanes_by_parity(-cai), 1, 2).astype(BF16)

    def pair_lanes(a):
        return a.reshape(ng // 2, 2, 8, ns).transpose(0, 2, 1, 3).reshape(ng // 2, 8, 2 * ns)

    dflat = jnp.tile(d_skip.reshape(ng, 1, gc), (1, 1, nc))

    nchunk = length // nc
    x = u3.reshape(bsz, nchunk, nc, ng, gc).transpose(3, 1, 0, 2, 4)
    x = jnp.pad(x, ((0, 0), (0, 0), (0, 8 - bsz), (0, 0), (0, 0))).reshape(ng, nchunk * 8, nc * gc)
    y = s5_apply(x, toep, wsr, wsi, wor, woi, pair_lanes(a16r), pair_lanes(a16i), dflat)
    y = y.reshape(ng, nchunk, 8, nc, gc)[:, :, :bsz].transpose(2, 1, 3, 0, 4)
    return y.reshape(bsz * length, width)


def _topk_rows(s, k):
    tops = []
    cur = s
    for _ in range(k):
        m = jnp.max(cur, axis=0, keepdims=True)
        tops.append(m)
        cur = jnp.where(cur >= m, NEG_INF, cur)
    return tops


def _peer_route_body(q_ref, keys_ref, s1_ref, e1_ref, s2_ref, e2_ref, tau_ref):
    nk = PEER_KEYS
    taus = []
    for h in range(PEER_HEADS):
        q1 = q_ref[:, (2 * h) * nk:(2 * h + 1) * nk]
        q2 = q_ref[:, (2 * h + 1) * nk:(2 * h + 2) * nk]
        s1 = _dot_nt_acc(keys_ref[h, 0], q1)
        s2 = _dot_nt_acc(keys_ref[h, 1], q2)
        t1 = _topk_rows(s1, PEER_TOPK)
        t2 = _topk_rows(s2, PEER_TOPK)
        top2 = jnp.concatenate(t2, axis=0)
        cand = jnp.concatenate([a + top2 for a in t1], axis=0)
        best = t1[0] + t2[0]
        zsum = jnp.zeros_like(best)
        cur = cand
        m = best
        for _ in range(PEER_TOPK):
            m = jnp.max(cur, axis=0, keepdims=True)
            zsum = zsum + jnp.exp(m - best)
            cur = jnp.where(cur >= m, NEG_INF, cur)
        taus.append(m)
        s1_ref[h] = s1
        s2_ref[h] = s2
        e1_ref[h] = jnp.exp(s1 - t1[0]) / zsum
        e2_ref[h] = jnp.exp(s2 - t2[0])
    tau_ref[...] = jnp.concatenate(taus, axis=0)


def peer_route(q, keys):
    t = q.shape[0]
    tm = _pick_tile(t, (256, 128))
    big = pl.BlockSpec((PEER_HEADS, PEER_KEYS, tm), lambda i: (0, 0, i))
    sds = jax.ShapeDtypeStruct((PEER_HEADS, PEER_KEYS, t), F32)
    return pl.pallas_call(
        _peer_route_body,
        grid=(t // tm,),
        in_specs=[pl.BlockSpec((tm, q.shape[1]), lambda i: (i, 0)),
                  pl.BlockSpec(keys.shape, lambda i: (0, 0, 0, 0))],
        out_specs=[big, big, big, big, pl.BlockSpec((PEER_HEADS, tm), lambda i: (0, i))],
        out_shape=[sds, sds, sds, sds, jax.ShapeDtypeStruct((PEER_HEADS, t), F32)],
        compiler_params=_params(("parallel",), 48),
        name="peer_route",
    )(q, keys)


def _peer_dense_body(hn_ref, u_ref, vt_ref, s1_ref, e1_ref, s2_ref, e2_ref, tau_ref, o_ref, w_ref,
                     *, te):
    j = pl.program_id(1)
    nk = PEER_KEYS

    @pl.when(j == 0)
    def _():
        o_ref[...] = jnp.zeros_like(o_ref)

    act = _gelu(_dot(u_ref[...], hn_ref[...]))
    for r in range(te // nk):
        i1 = j * (te // nk) + r
        gate = None
        for h in range(PEER_HEADS):
            s1row = s1_ref[h, pl.ds(i1, 1), :]
            e1row = e1_ref[h, pl.ds(i1, 1), :]
            keep = (s2_ref[h] + s1row) >= tau_ref[h:h + 1, :]
            term = jnp.where(keep, e2_ref[h] * e1row, 0.0)
            gate = term if gate is None else gate + term
        w_ref[r * nk:(r + 1) * nk, :] = (gate * act[r * nk:(r + 1) * nk, :]).astype(BF16)
    o_ref[...] += _dot(vt_ref[...], w_ref[...])


def peer_dense(hn_t, u_bf, vt_bf, s1, e1, s2, e2, tau):
    d, t = hn_t.shape
    ne = u_bf.shape[0]
    tm = _pick_tile(t, (256, 128))
    te = _pick_tile(ne, (512, 256, 128))
    big = pl.BlockSpec((PEER_HEADS, PEER_KEYS, tm), lambda i, j: (0, 0, i))
    return pl.pallas_call(
        functools.partial(_peer_dense_body, te=te),
        grid=(t // tm, ne // te),
        in_specs=[
            pl.BlockSpec((d, tm), lambda i, j: (0, i)),
            pl.BlockSpec((te, d), lambda i, j: (j, 0)),
            pl.BlockSpec((d, te), lambda i, j: (0, j)),
            big, big, big, big,
            pl.BlockSpec((PEER_HEADS, tm), lambda i, j: (0, i)),
        ],
        out_specs=pl.BlockSpec((d, tm), lambda i, j: (0, i)),
        out_shape=jax.ShapeDtypeStruct((d, t), F32),
        scratch_shapes=[pltpu.VMEM((te, tm), BF16)],
        compiler_params=_params(("parallel", "arbitrary"), 56),
        name="peer_dense",
    )(hn_t, u_bf, vt_bf, s1, e1, s2, e2, tau)


def _add_t_body(h_ref, pt_ref, o_ref):
    o_ref[...] = h_ref[...] + pt_ref[...].T


def add_transposed(h, p_t):
    t, d = h.shape
    tm = _pick_tile(t, (256, 128))
    td = _pick_tile(d, (1024, 512, 256, 128))
    return pl.pallas_call(
        _add_t_body,
        grid=(t // tm, d // td),
        in_specs=[pl.BlockSpec((tm, td), lambda i, j: (i, j)),
                  pl.BlockSpec((td, tm), lambda i, j: (j, i))],
        out_specs=pl.BlockSpec((tm, td), lambda i, j: (i, j)),
        out_shape=jax.ShapeDtypeStruct((t, d), F32),
        compiler_params=_params(("parallel", "parallel"), 32),
        name="add_transposed",
    )(h, p_t)


def peer_ffn_residual(h, gain, w_q_bf, keys, u_bf, vt_bf):
    q, hn = rms_matmul(h, gain, w_q_bf, with_xn=True)
    s1, e1, s2, e2, tau = peer_route(q, keys)
    out_t = peer_dense(hn.T, u_bf, vt_bf, s1, e1, s2, e2, tau)
    return add_transposed(h, out_t)


def _final_norm_body(x_ref, g_ref, o_ref):
    x = x_ref[...]
    inv = lax.rsqrt(jnp.mean(x * x, axis=-1, keepdims=True) + NORM_EPS)
    o_ref[...] = (x * inv) * g_ref[...]


def final_norm(h, gain):
    t, d = h.shape
    tm = _pick_tile(t, (256, 128, 64, 32, 16, 8))
    return pl.pallas_call(
        _final_norm_body,
        grid=(t // tm,),
        in_specs=[pl.BlockSpec((tm, d), lambda i: (i, 0)), pl.BlockSpec((1, d), lambda i: (0, 0))],
        out_specs=pl.BlockSpec((tm, d), lambda i: (i, 0)),
        out_shape=jax.ShapeDtypeStruct((t, d), F32),
        compiler_params=_params(("parallel",), 32),
        name="final_norm",
    )(h, gain.reshape(1, d))


def _even_layer_weights(w_in, shift_mu, w2, a2, width):
    pool_w = w_in.shape[1] - shift_mu.shape[0]
    c0 = pool_w + 3 * width
    c1 = c0 + RWKV_W_RANK
    c2 = c1 + RWKV_A_RANK
    d = w_in.shape[0]
    zw = jnp.zeros((d, LANES - RWKV_W_RANK), w_in.dtype)
    za = jnp.zeros((d, LANES - RWKV_A_RANK), w_in.dtype)
    w_cat = jnp.concatenate([w_in[:, :c0], w_in[:, c0:c1], zw, w_in[:, c1:c2], za, w_in[:, c2:]],
                            axis=1).astype(BF16)
    s0 = 3 * width
    s1 = s0 + RWKV_W_RANK
    s2 = s1 + RWKV_A_RANK
    mu = jnp.concatenate([
        jnp.zeros((pool_w,), F32), shift_mu[:s0], shift_mu[s0:s1],
        jnp.zeros((LANES - RWKV_W_RANK,), F32), shift_mu[s1:s2],
        jnp.zeros((LANES - RWKV_A_RANK,), F32), shift_mu[s2:]]).reshape(1, -1)
    w2p = jnp.pad(w2, ((0, LANES - RWKV_W_RANK), (0, 0)))
    a2p = jnp.pad(a2, ((0, LANES - RWKV_A_RANK), (0, 0)))
    return w_cat, mu, w2p, a2p, pool_w


def kernel(x, meta_tokens, mix_norm_gain, ffn_norm_gain, final_norm_gain, w_in_even, pool_w, pool_scale, shift_mu, rwkv_w0, rwkv_w2, rwkv_a0, rwkv_a2, rwkv_g2, rwkv_k_k, rwkv_k_a, rwkv_r_k, rwkv_gn_gain, rwkv_gn_bias, w_out_even, w_in_odd, ssm_lam_re, ssm_lam_im, ssm_log_dt, ssm_b_re, ssm_b_im, ssm_c_re, ssm_c_im, ssm_d, w_glu, peer_w_q, peer_sub_keys, peer_u, peer_v):
    bsz, seq, d = x.shape
    depth = mix_norm_gain.shape[0]
    real = N_META + seq
    length = -(-real // SEQ_ALIGN) * SEQ_ALIGN
    meta = jnp.broadcast_to(meta_tokens[None].astype(x.dtype), (bsz, N_META, d))
    h = jnp.concatenate([meta, x, jnp.zeros((bsz, length - real, d), x.dtype)], axis=1)
    h = h.reshape(bsz * length, d)

    for layer in range(depth):
        if layer % 2 == 0:
            i = layer // 2
            width = rwkv_w0.shape[-1]
            w_cat, mu, w2p, a2p, pool_cols = _even_layer_weights(
                w_in_even[i], shift_mu[i], rwkv_w2[i], rwkv_a2[i], width)
            z = rms_matmul(h, mix_norm_gain[layer], w_cat)
            z3 = z.reshape(bsz, length, -1)
            y_pool = pool_mixer(z3, pool_w[i], pool_scale[i])
            ob = pool_cols // 512
            wb = width // 512
            r, k, v, kap, b, lw, g = rwkv_prep(
                z3, (ob, ob + wb, ob + 2 * wb, ob + 3 * wb), mu, rwkv_w0[i], w2p, rwkv_a0[i], a2p,
                rwkv_g2[i], rwkv_k_k[i], rwkv_k_a[i])
            y_rwkv = rwkv_scan(r, k, v, kap, b, lw, g, rwkv_r_k[i], rwkv_gn_gain[i],
                               rwkv_gn_bias[i])
            h = proj2_residual(y_pool.reshape(bsz * length, -1), y_rwkv.reshape(bsz * length, -1),
                               w_out_even[i].astype(BF16), h)
        else:
            j = layer // 2
            u = rms_matmul(h, mix_norm_gain[layer], w_in_odd[j].astype(BF16), out_dtype=BF16)
            y = s5_mixer(u.reshape(bsz, length, -1), ssm_lam_re[j], ssm_lam_im[j], ssm_log_dt[j],
                         ssm_b_re[j], ssm_b_im[j], ssm_c_re[j], ssm_c_im[j], ssm_d[j])
            h = glu_residual(y, w_glu[j].astype(BF16), h)
        h = peer_ffn_residual(h, ffn_norm_gain[layer], peer_w_q[layer].astype(BF16),
                              peer_sub_keys[layer], peer_u[layer].astype(BF16),
                              peer_v[layer].T.astype(BF16))
    out = final_norm(h, final_norm_gain).reshape(bsz, length, d)
    return out[:, N_META:real]
```

```python
import functools

import jax
import jax.numpy as jnp
from jax import lax
from jax.experimental import pallas as pl
from jax.experimental.pallas import tpu as pltpu

F32 = jnp.float32
BF16 = jnp.bfloat16

NORM_EPS = 1e-6
N_META = 16
POOL_WINDOWS = (2, 4, 8, 16)
RWKV_HEAD = 64
RWKV_W_RANK = 96
RWKV_A_RANK = 96
RWKV_G_RANK = 256
RWKV_GN_EPS = 64e-5
RWKV_CHUNK = 64
SSM_GROUP = 16
SSM_STATE = 64
SSM_CHUNK = 16
PEER_KEYS = 128
PEER_HEADS = 8
PEER_TOPK = 16
LANES = 128
SEQ_ALIGN = 128

NEG_INF = float("-inf")


def _params(sem, vmem_mb):
    return pltpu.CompilerParams(dimension_semantics=sem, vmem_limit_bytes=vmem_mb << 20)


def _split2(x):
    hi = x.astype(BF16)
    lo = (x - hi.astype(F32)).astype(BF16)
    return hi, lo


def _split3(x):
    hi = x.astype(BF16)
    r = x - hi.astype(F32)
    mid = r.astype(BF16)
    lo = (r - mid.astype(F32)).astype(BF16)
    return hi, mid, lo


def _dot(a, b):
    return jnp.dot(a, b, preferred_element_type=F32)


def _dot_nt(a, b):
    return lax.dot_general(a, b, (((1,), (1,)), ((), ())), preferred_element_type=F32)


def _dot_acc(a, b):
    ah, al = _split2(a)
    bh, bl = _split2(b)
    return _dot(ah, bh) + _dot(al, bh) + _dot(ah, bl)


def _dot_nt_acc(a, b):
    ah, al = _split2(a)
    bh, bl = _split2(b)
    return _dot_nt(ah, bh) + _dot_nt(al, bh) + _dot_nt(ah, bl)


def _gelu(x):
    return 0.5 * x * (1.0 + lax.erf(x * 0.7071067811865476))


def _sigmoid(x):
    return 1.0 / (1.0 + jnp.exp(-x))


def _pick_tile(n, prefs):
    for t in prefs:
        if n % t == 0:
            return t
    return n


def _rms_mm_body(x_ref, g_ref, w_ref, o_ref, *rest, with_xn):
    if with_xn:
        xn_out_ref, xn_ref = rest
    else:
        (xn_ref,) = rest

    @pl.when(pl.program_id(1) == 0)
    def _():
        x = x_ref[...]
        inv = lax.rsqrt(jnp.mean(x * x, axis=-1, keepdims=True) + NORM_EPS)
        xn = ((x * inv) * g_ref[...]).astype(BF16)
        xn_ref[...] = xn
        if with_xn:
            xn_out_ref[...] = xn

    o_ref[...] = _dot(xn_ref[...], w_ref[...]).astype(o_ref.dtype)


def rms_matmul(x, gain, w, *, out_dtype=F32, with_xn=False):
    t, d = x.shape
    n = w.shape[1]
    tm = _pick_tile(t, (512, 384, 256, 128, 64, 32, 16, 8))
    tn = _pick_tile(n, (512, 256, 128))
    out_shape = [jax.ShapeDtypeStruct((t, n), out_dtype)]
    out_specs = [pl.BlockSpec((tm, tn), lambda i, j: (i, j))]
    if with_xn:
        out_shape.append(jax.ShapeDtypeStruct((t, d), BF16))
        out_specs.append(pl.BlockSpec((tm, d), lambda i, j: (i, 0)))
    res = pl.pallas_call(
        functools.partial(_rms_mm_body, with_xn=with_xn),
        grid=(t // tm, n // tn),
        in_specs=[
            pl.BlockSpec((tm, d), lambda i, j: (i, 0)),
            pl.BlockSpec((1, d), lambda i, j: (0, 0)),
            pl.BlockSpec((d, tn), lambda i, j: (0, j)),
        ],
        out_specs=out_specs,
        out_shape=out_shape,
        scratch_shapes=[pltpu.VMEM((tm, d), BF16)],
        compiler_params=_params(("parallel", "arbitrary"), 56),
        name="rms_matmul",
    )(x, gain.reshape(1, d), w)
    return res if with_xn else res[0]


def _proj2_body(a1_ref, a2_ref, w1_ref, w2_ref, h_ref, o_ref):
    o_ref[...] = h_ref[...] + _dot(a1_ref[...], w1_ref[...]) + _dot(a2_ref[...], w2_ref[...])


def proj2_residual(a1, a2, w, h):
    t, k1 = a1.shape
    k2 = a2.shape[1]
    n = w.shape[1]
    assert k1 == k2
    tm = _pick_tile(t, (768, 512, 384, 256, 128, 64, 32, 16))
    tn = _pick_tile(n, (512, 256, 128))
    return pl.pallas_call(
        _proj2_body,
        grid=(t // tm, n // tn),
        in_specs=[
            pl.BlockSpec((tm, k1), lambda i, j: (i, 0)),
            pl.BlockSpec((tm, k2), lambda i, j: (i, 0)),
            pl.BlockSpec((k1, tn), lambda i, j: (0, j)),
            pl.BlockSpec((k2, tn), lambda i, j: (1, j)),
            pl.BlockSpec((tm, tn), lambda i, j: (i, j)),
        ],
        out_specs=pl.BlockSpec((tm, tn), lambda i, j: (i, j)),
        out_shape=jax.ShapeDtypeStruct((t, n), F32),
        compiler_params=_params(("parallel", "arbitrary"), 56),
        name="proj2_residual",
    )(a1, a2, w, w, h)


def _glu_body(y_ref, wa_ref, wb_ref, h_ref, o_ref):
    y = y_ref[...]
    ga = _dot(y, wa_ref[...])
    gb = _dot(y, wb_ref[...])
    o_ref[...] = h_ref[...] + ga * _sigmoid(gb)


def glu_residual(y, w, h):
    t, k = y.shape
    n = w.shape[1] // 2
    tm = _pick_tile(t, (768, 512, 384, 256, 128, 64, 32, 16))
    tn = _pick_tile(n, (512, 256, 128))
    nb = n // tn
    return pl.pallas_call(
        _glu_body,
        grid=(t // tm, nb),
        in_specs=[
            pl.BlockSpec((tm, k), lambda i, j: (i, 0)),
            pl.BlockSpec((k, tn), lambda i, j: (0, j)),
            pl.BlockSpec((k, tn), lambda i, j: (0, j + nb)),
            pl.BlockSpec((tm, tn), lambda i, j: (i, j)),
        ],
        out_specs=pl.BlockSpec((tm, tn), lambda i, j: (i, j)),
        out_shape=jax.ShapeDtypeStruct((t, n), F32),
        compiler_params=_params(("parallel", "arbitrary"), 56),
        name="glu_residual",
    )(y, w, w, h)


def _pool_body(z_ref, w_ref, s_ref, o_ref):
    grp = pl.program_id(1)
    z = z_ref[...]
    length = z.shape[0]
    row = lax.broadcasted_iota(jnp.int32, (length, 1), 0)
    posf = (row + 1).astype(F32)

    def shifted(x, k):
        return jnp.where(row >= k, pltpu.roll(x, k, axis=0), 0.0)

    for gi, win in enumerate(POOL_WINDOWS):

        @pl.when(grp == gi)
        def _(win=win):
            acc = z
            span = 1
            while span < win:
                acc = acc + shifted(acc, span)
                span *= 2
            cnt = jnp.minimum(posf, float(win))
            m = acc / cnt - z
            y = _dot(m.astype(BF16), w_ref[...])
            o_ref[...] = (y * s_ref[...]).astype(o_ref.dtype)


def pool_mixer(z3, pool_w, pool_scale):
    bsz, length, _ = z3.shape
    ng, c, _ = pool_w.shape
    return pl.pallas_call(
        _pool_body,
        grid=(bsz, ng),
        in_specs=[
            pl.BlockSpec((None, length, c), lambda b, g: (b, 0, g)),
            pl.BlockSpec((None, c, c), lambda b, g: (g, 0, 0)),
            pl.BlockSpec((1, c), lambda b, g: (0, g)),
        ],
        out_specs=pl.BlockSpec((None, length, c), lambda b, g: (b, 0, g)),
        out_shape=jax.ShapeDtypeStruct((bsz, length, ng * c), BF16),
        compiler_params=_params(("parallel", "arbitrary"), 48),
        name="pool_mixer",
    )(z3, pool_w.astype(BF16), pool_scale.reshape(1, ng * c))


def _head_sum_matrix(width, head):
    shift = head.bit_length() - 1
    r = lax.broadcasted_iota(jnp.int32, (width, width), 0) >> shift
    c = lax.broadcasted_iota(jnp.int32, (width, width), 1) >> shift
    return jnp.where(r == c, 1.0, 0.0).astype(BF16)


def _head_sums(x, ones_bd):
    hi, lo = _split2(x)
    return _dot(hi, ones_bd) + _dot(lo, ones_bd)


def _rwkv_prep_body(zr_ref, zk_ref, zv_ref, zl_ref, pr_ref, pk_ref, pv_ref, pl_ref,
                    mur_ref, muk_ref, muv_ref, mul_ref, w0_ref, w2_ref, a0_ref, a2_ref, g2_ref,
                    kk_ref, ka_ref,
                    r_out, k_out, v_out, kap_out, b_out, lw_out, g_out):
    first = pl.program_id(1) == 0
    tl = zr_ref.shape[0]
    row = lax.broadcasted_iota(jnp.int32, (tl, 1), 0)

    def mix(z_ref, p_ref, mu_ref):
        z = z_ref[...]
        prev_last = jnp.where(first, 0.0, p_ref[7:8, :])
        prev = jnp.where(row == 0, prev_last, pltpu.roll(z, 1, axis=0))
        return z + (prev - z) * mu_ref[...]

    r = mix(zr_ref, pr_ref, mur_ref)
    k = mix(zk_ref, pk_ref, muk_ref)
    v = mix(zv_ref, pv_ref, muv_ref)
    lora = mix(zl_ref, pl_ref, mul_ref)
    w_lr = lora[:, 0:LANES]
    a_lr = lora[:, LANES:2 * LANES]
    g_lr = lora[:, 2 * LANES:]

    x = w0_ref[...] + _dot_acc(jnp.tanh(w_lr), w2_ref[...])
    y = -x
    softplus = jnp.maximum(y, 0.0) + jnp.log(1.0 + jnp.exp(-jnp.abs(y)))
    lw = -jnp.exp(-softplus - 0.5)
    a = _sigmoid(a0_ref[...] + _dot_acc(a_lr, a2_ref[...]))
    g = _dot_acc(_sigmoid(g_lr), g2_ref[...])

    ones_bd = _head_sum_matrix(LANES, RWKV_HEAD)
    kk = k * kk_ref[...]
    kk2 = kk * kk
    width = k.shape[1]
    ss = jnp.concatenate(
        [_head_sums(kk2[:, c:c + LANES], ones_bd) for c in range(0, width, LANES)], axis=1)
    kap = kk / jnp.maximum(jnp.sqrt(ss), 1e-12)

    r_out[...] = r
    k_out[...] = k * (1.0 + (a - 1.0) * ka_ref[...])
    v_out[...] = v
    kap_out[...] = kap
    b_out[...] = kap * a
    lw_out[...] = lw
    g_out[...] = g


def rwkv_prep(z3, off_blocks, mu, w0, w2p, a0, a2p, g2, k_k, k_a):
    bsz, length, _ = z3.shape
    width = w0.shape[-1]
    cw = 512
    ncb = width // cw
    tl = _pick_tile(length, (352, 272, 192, 176, 136, 128, 64))
    ob_r, ob_k, ob_v, ob_l = off_blocks
    t8 = tl // 8

    def cur(ob, fixed=False):
        if fixed:
            return pl.BlockSpec((None, tl, cw), lambda b, i, c: (b, i, ob))
        return pl.BlockSpec((None, tl, cw), lambda b, i, c: (b, i, ob + c))

    def prev(ob, fixed=False):
        if fixed:
            return pl.BlockSpec((None, 8, cw), lambda b, i, c: (b, jnp.maximum(i * t8 - 1, 0), ob))
        return pl.BlockSpec((None, 8, cw), lambda b, i, c: (b, jnp.maximum(i * t8 - 1, 0), ob + c))

    def vec(ob, fixed=False):
        if fixed:
            return pl.BlockSpec((1, cw), lambda b, i, c: (0, ob))
        return pl.BlockSpec((1, cw), lambda b, i, c: (0, ob + c))

    colvec = pl.BlockSpec((1, cw), lambda b, i, c: (0, c))
    out_spec = pl.BlockSpec((None, tl, cw), lambda b, i, c: (b, i, c))
    out_sds = jax.ShapeDtypeStruct((bsz, length, width), F32)
    return pl.pallas_call(
        _rwkv_prep_body,
        grid=(bsz, length // tl, ncb),
        in_specs=[
            cur(ob_r), cur(ob_k), cur(ob_v), cur(ob_l, True),
            prev(ob_r), prev(ob_k), prev(ob_v), prev(ob_l, True),
            vec(ob_r), vec(ob_k), vec(ob_v), vec(ob_l, True),
            colvec,
            pl.BlockSpec((LANES, cw), lambda b, i, c: (0, c)),
            colvec,
            pl.BlockSpec((LANES, cw), lambda b, i, c: (0, c)),
            pl.BlockSpec((RWKV_G_RANK, cw), lambda b, i, c: (0, c)),
            colvec, colvec,
        ],
        out_specs=[out_spec] * 7,
        out_shape=[out_sds] * 7,
        compiler_params=_params(("parallel", "parallel", "arbitrary"), 56),
        name="rwkv_prep",
    )(z3, z3, z3, z3, z3, z3, z3, z3, mu, mu, mu, mu,
      w0.reshape(1, width), w2p, a0.reshape(1, width), a2p, g2,
      k_k.reshape(1, width), k_a.reshape(1, width))


def _rwkv_scan_body(r_ref, k_ref, v_ref, kap_ref, b_ref, lw_ref, g_ref, rk_ref, gain_ref, bias_ref,
                    o_ref, h_ref, *, pairs):
    cn = RWKV_CHUNK
    hd = RWKV_HEAD

    @pl.when(pl.program_id(2) == 0)
    def _():
        h_ref[...] = jnp.zeros_like(h_ref)

    lane = lax.broadcasted_iota(jnp.int32, (1, LANES), 1)
    m0 = lane < hd
    rowi = lax.broadcasted_iota(jnp.int32, (cn, LANES), 0)
    coli = lax.broadcasted_iota(jnp.int32, (cn, LANES), 1) & (hd - 1)
    strict = rowi > coli
    incl = rowi >= coli
    eye2 = jnp.where(rowi == coli, 1.0, 0.0)
    tri = jnp.where(lax.broadcasted_iota(jnp.int32, (cn, cn), 0)
                    >= lax.broadcasted_iota(jnp.int32, (cn, cn), 1), 1.0, 0.0).astype(BF16)
    br = lax.broadcasted_iota(jnp.int32, (LANES, LANES), 0) >> (hd.bit_length() - 1)
    bc = lax.broadcasted_iota(jnp.int32, (LANES, LANES), 1) >> (hd.bit_length() - 1)
    bd_mask = br == bc
    ones_bd = jnp.where(bd_mask, 1.0, 0.0).astype(BF16)

    def stk(x):
        return jnp.concatenate([jnp.where(m0, x, 0.0), jnp.where(m0, 0.0, x)], axis=0).astype(BF16)

    ps = range(pairs)
    sls = [slice(p * LANES, (p + 1) * LANES) for p in ps]
    r = [r_ref[:, s] for s in sls]
    k = [k_ref[:, s] for s in sls]
    v = [v_ref[:, s] for s in sls]
    kap = [kap_ref[:, s] for s in sls]
    b = [b_ref[:, s] for s in sls]
    lw = [lw_ref[:, s] for s in sls]
    hbd = [h_ref[p] for p in ps]

    lsp = [_split3(x) for x in lw]
    lp = [_dot(tri, a) + _dot(tri, m) + _dot(tri, c) for a, m, c in lsp]
    lpc = [x[cn - 1:cn, :] for x in lp]
    pinv = [jnp.exp(-x) for x in lp]
    pend = [jnp.exp(c - x) for c, x in zip(lpc, lp)]
    kt = [kap[p] * jnp.exp(lp[p] - lw[p]) for p in ps]
    rt = [r[p] * jnp.exp(lp[p]) for p in ps]
    kh = [k[p] * pinv[p] for p in ps]
    bh = [b[p] * pinv[p] for p in ps]
    khc = [k[p] * pend[p] for p in ps]
    bhc = [b[p] * pend[p] for p in ps]

    ktrt = [jnp.concatenate([kt[p], rt[p]], axis=0).astype(BF16) for p in ps]
    ab_all = [_dot_nt(ktrt[p], stk(bh[p])) for p in ps]
    ak_all = [_dot_nt(ktrt[p], stk(kh[p])) for p in ps]
    n_mat = [jnp.where(strict, x[:cn], 0.0) for x in ab_all]
    a_k = [jnp.where(strict, x[:cn], 0.0) for x in ak_all]
    a_rb = [jnp.where(incl, x[cn:], 0.0) for x in ab_all]
    a_rk = [jnp.where(incl, x[cn:], 0.0) for x in ak_all]

    t_mat = [eye2 - x for x in n_mat]
    pw = n_mat
    span = 1
    while 2 * span < cn:
        pw = [_dot(x.astype(BF16), stk(x)) for x in pw]
        t_mat = [t + _dot(t.astype(BF16), stk(x)) for t, x in zip(t_mat, pw)]
        span *= 2
    t_bf = [t.astype(BF16) for t in t_mat]

    kbar = [_dot(t_bf[p], stk(kt[p])) for p in ps]
    akv = [_dot(a_k[p].astype(BF16), stk(v[p])) for p in ps]
    ubar = [_dot(t_bf[p], stk(akv[p])) for p in ps]
    hb = [x.astype(BF16) for x in hbd]
    u = [_dot(kbar[p].astype(BF16), hb[p]) + ubar[p] for p in ps]
    y = [_dot(rt[p].astype(BF16), hb[p]) + _dot(a_rk[p].astype(BF16), stk(v[p]))
         - _dot(a_rb[p].astype(BF16), stk(u[p])) for p in ps]

    kb_t = [jnp.concatenate([khc[p], -bhc[p]], axis=0).T.astype(BF16) for p in ps]
    vu = [jnp.concatenate([v[p], u[p]], axis=0).astype(BF16) for p in ps]
    g_mat = [_dot(kb_t[p], vu[p]) for p in ps]
    pc_col = [jnp.exp(jnp.broadcast_to(x, (LANES, LANES)).T) for x in lpc]
    for p in ps:
        h_ref[p] = jnp.where(bd_mask, pc_col[p] * hbd[p] + g_mat[p], 0.0)

    mean = [_head_sums(x, ones_bd) * (1.0 / hd) for x in y]
    yc = [y[p] - mean[p] for p in ps]
    var = [_head_sums(x * x, ones_bd) * (1.0 / hd) for x in yc]
    bonus = [_head_sums(r[p] * k[p] * rk_ref[:, sls[p]], ones_bd) * v[p] for p in ps]
    for p in ps:
        yn = yc[p] * lax.rsqrt(var[p] + RWKV_GN_EPS) * gain_ref[:, sls[p]] + bias_ref[:, sls[p]]
        o_ref[:, sls[p]] = ((yn + bonus[p]) * g_ref[:, sls[p]]).astype(o_ref.dtype)


def rwkv_scan(r, k, v, kap, b, lw, g, r_k, gn_gain, gn_bias):
    bsz, length, width = r.shape
    cw = _pick_tile(width, (1024, 512))
    pairs = cw // LANES
    seq = pl.BlockSpec((None, RWKV_CHUNK, cw), lambda bi, c, t: (bi, t, c))
    vec = pl.BlockSpec((1, cw), lambda bi, c, t: (0, c))
    return pl.pallas_call(
        functools.partial(_rwkv_scan_body, pairs=pairs),
        grid=(bsz, width // cw, length // RWKV_CHUNK),
        in_specs=[seq] * 7 + [vec] * 3,
        out_specs=seq,
        out_shape=jax.ShapeDtypeStruct((bsz, length, width), BF16),
        scratch_shapes=[pltpu.VMEM((pairs, LANES, LANES), F32)],
        compiler_params=_params(("parallel", "parallel", "arbitrary"), 48),
        name="rwkv_scan",
    )(r, k, v, kap, b, lw, g, r_k.reshape(1, width), gn_gain.reshape(1, width),
      gn_bias.reshape(1, width))


def _s5_param_body(lr_ref, li_ref, ldt_ref, btr_ref, bti_ref, cr_ref, ci_ref,
                   kst_ref, etr_ref, eti_ref, car_ref, cai_ref, a16r_ref, a16i_ref, *, groups):
    nc = SSM_CHUNK
    npow = 24
    tau = lax.broadcasted_iota(jnp.int32, (npow, SSM_STATE), 0).astype(F32)
    for gidx in range(groups):
        lr = lr_ref[gidx]
        li = li_ref[gidx]
        dt = jnp.exp(ldt_ref[gidx])
        mag = jnp.exp(tau * (lr * dt))
        ang = tau * (li * dt)
        pr = mag * jnp.cos(ang)
        pi = mag * jnp.sin(ang)
        ar = pr[1:2]
        ai = pi[1:2]
        den = lr * lr + li * li
        fr = ((ar - 1.0) * lr + ai * li) / den
        fi = (ai * lr - (ar - 1.0) * li) / den
        btr = btr_ref[gidx]
        bti = bti_ref[gidx]
        bbr = fr * btr - fi * bti
        bbi = fr * bti + fi * btr
        cr = cr_ref[gidx]
        ci = ci_ref[gidx]
        etr = jnp.concatenate([pr[s:s + 1] * bbr - pi[s:s + 1] * bbi for s in range(nc)], axis=0)
        eti = jnp.concatenate([pr[s:s + 1] * bbi + pi[s:s + 1] * bbr for s in range(nc)], axis=0)
        car = jnp.concatenate([pr[s:s + 1] * cr - pi[s:s + 1] * ci for s in range(1, nc + 1)], axis=0)
        cai = jnp.concatenate([pr[s:s + 1] * ci + pi[s:s + 1] * cr for s in range(1, nc + 1)], axis=0)
        kst_ref[gidx] = _dot_nt_acc(etr, cr) - _dot_nt_acc(eti, ci)
        etr_ref[gidx] = etr
        eti_ref[gidx] = eti
        car_ref[gidx] = car
        cai_ref[gidx] = cai
        a16r_ref[gidx] = jnp.broadcast_to(pr[nc:nc + 1], (8, SSM_STATE))
        a16i_ref[gidx] = jnp.broadcast_to(pi[nc:nc + 1], (8, SSM_STATE))


def s5_params(lam_re, lam_im, log_dt, b_re, b_im, c_re, c_im):
    ng, ns = lam_re.shape
    gc = SSM_GROUP
    gb = 8
    rows = SSM_CHUNK * gc

    def spec(*dims):
        return pl.BlockSpec((gb,) + dims, lambda i: (i,) + (0,) * len(dims))

    outs = [((rows, gc), F32)] + [((rows, ns), F32)] * 4 + [((8, ns), F32)] * 2
    return pl.pallas_call(
        functools.partial(_s5_param_body, groups=gb),
        grid=(ng // gb,),
        in_specs=[spec(1, ns), spec(1, ns), spec(1, 1), spec(gc, ns), spec(gc, ns), spec(gc, ns),
                  spec(gc, ns)],
        out_specs=[spec(*s) for s, _ in outs],
        out_shape=[jax.ShapeDtypeStruct((ng,) + s, dt) for s, dt in outs],
        compiler_params=_params(("parallel",), 32),
        name="s5_params",
    )(lam_re.reshape(ng, 1, ns), lam_im.reshape(ng, 1, ns), log_dt.reshape(ng, 1, 1),
      jnp.swapaxes(b_re, 1, 2), jnp.swapaxes(b_im, 1, 2), c_re, c_im)


def _s5_body(x_ref, toep_ref, wsr_ref, wsi_ref, wor_ref, woi_ref, a16r_ref, a16i_ref, d_ref,
             o_ref, sre_ref, sim_ref, xre_ref, xim_ref):
    rows = x_ref.shape[1]
    nchunk = rows // 8
    x0 = x_ref[0]
    x1 = x_ref[1]
    sre_ref[...] = _dot(x0, wsr_ref[0]) + _dot(x1, wsr_ref[1])
    sim_ref[...] = _dot(x0, wsi_ref[0]) + _dot(x1, wsi_ref[1])
    ar = a16r_ref[...]
    ai = a16i_ref[...]

    def step(c, carry):
        xr, xi = carry
        off = pl.multiple_of(c * 8, 8)
        xre_ref[pl.ds(off, 8), :] = xr
        xim_ref[pl.ds(off, 8), :] = xi
        sr = sre_ref[pl.ds(off, 8), :]
        si = sim_ref[pl.ds(off, 8), :]
        return ar * xr - ai * xi + sr, ar * xi + ai * xr + si

    zero = jnp.zeros((8, LANES), F32)
    lax.fori_loop(0, nchunk, step, (zero, zero))
    xr = xre_ref[...].astype(BF16)
    xi = xim_ref[...].astype(BF16)
    for q, xq in enumerate((x0, x1)):
        y = (_dot(xq, toep_ref[q]) + _dot(xr, wor_ref[q]) + _dot(xi, woi_ref[q])
             + xq.astype(F32) * d_ref[q])
        o_ref[q] = _gelu(y).astype(o_ref.dtype)


def s5_apply(xg, toep, wsr, wsi, wor, woi, a16r, a16i, dflat):
    ng, rows, cols = xg.shape

    def spec(*dims):
        return pl.BlockSpec((2,) + dims, lambda i: (i,) + (0,) * len(dims))

    return pl.pallas_call(
        _s5_body,
        grid=(ng // 2,),
        in_specs=[spec(rows, cols), spec(cols, cols), spec(cols, LANES), spec(cols, LANES),
                  spec(LANES, cols), spec(LANES, cols),
                  pl.BlockSpec((None, 8, LANES), lambda i: (i, 0, 0)),
                  pl.BlockSpec((None, 8, LANES), lambda i: (i, 0, 0)),
                  spec(1, cols)],
        out_specs=spec(rows, cols),
        out_shape=jax.ShapeDtypeStruct((ng, rows, cols), BF16),
        scratch_shapes=[pltpu.VMEM((rows, LANES), F32)] * 4,
        compiler_params=_params(("parallel",), 32),
        name="s5_apply",
    )(xg, toep, wsr, wsi, wor, woi, a16r, a16i, dflat)


def s5_mixer(u3, lam_re, lam_im, log_dt, b_re, b_im, c_re, c_im, d_skip):
    bsz, length, width = u3.shape
    ng, ns = lam_re.shape
    gc = SSM_GROUP
    nc = SSM_CHUNK
    kst, etr, eti, car, cai, a16r, a16i = s5_params(lam_re, lam_im, log_dt, b_re, b_im, c_re, c_im)

    k4 = kst.reshape(ng, nc, gc, gc)
    s_idx = jnp.arange(nc)[:, None]
    t_idx = jnp.arange(nc)[None, :]
    lag = t_idx - s_idx
    toep = jnp.where((lag >= 0)[None, :, :, None, None], k4[:, jnp.clip(lag, 0, nc - 1)], 0.0)
    toep = toep.transpose(0, 1, 3, 2, 4).reshape(ng, nc * gc, nc * gc).astype(BF16)
    odd = (jnp.arange(ng) % 2 == 1)[:, None, None]

    def lanes_by_parity(w):
        z = jnp.zeros_like(w)
        return jnp.where(odd, jnp.concatenate([z, w], -1), jnp.concatenate([w, z], -1))

    def flip_lag(e):
        return e.reshape(ng, nc, gc, ns)[:, ::-1].reshape(ng, nc * gc, ns)

    wsr = lanes_by_parity(flip_lag(etr)).astype(BF16)
    wsi = lanes_by_parity(flip_lag(eti)).astype(BF16)
    wor = jnp.swapaxes(lanes_by_parity(car), 1, 2).astype(BF16)
    woi = jnp.swapaxes(lanes_by_parity(-cai), 1, 2).astype(BF16)

    def pair_lanes(a):
        return a.reshape(ng // 2, 2, 8, ns).transpose(0, 2, 1, 3).reshape(ng // 2, 8, 2 * ns)

    dflat = jnp.tile(d_skip.reshape(ng, 1, gc), (1, 1, nc))

    nchunk = length // nc
    x = u3.reshape(bsz, nchunk, nc, ng, gc).transpose(3, 1, 0, 2, 4)
    x = jnp.pad(x, ((0, 0), (0, 0), (0, 8 - bsz), (0, 0), (0, 0))).reshape(ng, nchunk * 8, nc * gc)
    y = s5_apply(x, toep, wsr, wsi, wor, woi, pair_lanes(a16r), pair_lanes(a16i), dflat)
    y = y.reshape(ng, nchunk, 8, nc, gc)[:, :, :bsz].transpose(2, 1, 3, 0, 4)
    return y.reshape(bsz * length, width)


def _topk_rows(s, k):
    tops = []
    cur = s
    for _ in range(k):
        m = jnp.max(cur, axis=0, keepdims=True)
        tops.append(m)
        cur = jnp.where(cur >= m, NEG_INF, cur)
    return tops


def _peer_route_body(q_ref, keys_ref, th_ref, e1_ref, s2_ref, e2_ref):
    nk = PEER_KEYS
    for h in range(PEER_HEADS):
        q1 = q_ref[:, (2 * h) * nk:(2 * h + 1) * nk]
        q2 = q_ref[:, (2 * h + 1) * nk:(2 * h + 2) * nk]
        s1 = _dot_nt_acc(keys_ref[h, 0], q1)
        s2 = _dot_nt_acc(keys_ref[h, 1], q2)
        t1 = _topk_rows(s1, PEER_TOPK)
        t2 = _topk_rows(s2, PEER_TOPK)
        top2 = jnp.concatenate(t2, axis=0)
        cand = jnp.concatenate([a + top2 for a in t1], axis=0)
        best = t1[0] + t2[0]
        zsum = jnp.zeros_like(best)
        cur = cand
        m = best
        for _ in range(PEER_TOPK):
            m = jnp.max(cur, axis=0, keepdims=True)
            zsum = zsum + jnp.exp(m - best)
            cur = jnp.where(cur >= m, NEG_INF, cur)
        tau = m
        theta = jnp.full_like(s1, jnp.inf)
        for a in range(PEER_TOPK):
            kept = cand[a * PEER_TOPK:(a + 1) * PEER_TOPK] >= tau
            theta_a = jnp.min(jnp.where(kept, top2, jnp.inf), axis=0, keepdims=True)
            theta = jnp.where(s1 == t1[a], theta_a, theta)
        th_ref[h] = theta
        s2_ref[h] = s2
        e1_ref[h] = jnp.exp(s1 - t1[0]) / zsum
        e2_ref[h] = jnp.exp(s2 - t2[0])


def peer_route(q, keys):
    t = q.shape[0]
    tm = _pick_tile(t, (256, 128))
    big = pl.BlockSpec((PEER_HEADS, PEER_KEYS, tm), lambda i: (0, 0, i))
    sds = jax.ShapeDtypeStruct((PEER_HEADS, PEER_KEYS, t), F32)
    return pl.pallas_call(
        _peer_route_body,
        grid=(t // tm,),
        in_specs=[pl.BlockSpec((tm, q.shape[1]), lambda i: (i, 0)),
                  pl.BlockSpec(keys.shape, lambda i: (0, 0, 0, 0))],
        out_specs=[big, big, big, big],
        out_shape=[sds, sds, sds, sds],
        compiler_params=_params(("parallel",), 48),
        name="peer_route",
    )(q, keys)


PEER_SLAB = 256


def _peer_dense_body(hnt_ref, u_ref, v_ref, s2_ref, e2_ref, th_ref, e1_ref, o_ref, *, te):
    nk = PEER_KEYS
    rows_per_tile = te // nk
    rows_per_slab = PEER_SLAB // nk
    nslab = te // PEER_SLAB

    @pl.when(pl.program_id(1) == 0)
    def _():
        o_ref[...] = jnp.zeros_like(o_ref)

    hnt = hnt_ref[...]
    acts = [_dot(u_ref[s * PEER_SLAB:(s + 1) * PEER_SLAB, :], hnt) for s in range(nslab)]
    total = None
    for s in range(nslab):
        parts = []
        for r in range(rows_per_slab):
            ridx = s * rows_per_slab + r
            gate = None
            for h in range(PEER_HEADS):
                row = h * rows_per_tile + ridx
                keep = s2_ref[h] >= th_ref[row:row + 1, :]
                term = jnp.where(keep, e2_ref[h] * e1_ref[row:row + 1, :], 0.0)
                gate = term if gate is None else gate + term
            parts.append(gate * _gelu(acts[s][r * nk:(r + 1) * nk, :]))
        w = jnp.concatenate(parts, axis=0).T.astype(BF16)
        c = _dot(w, v_ref[s * PEER_SLAB:(s + 1) * PEER_SLAB, :])
        total = c if total is None else total + c
    o_ref[...] += total


def peer_dense(hn_t, u_bf, v_bf, s2, e2, th_rows, e1_rows):
    d, t = hn_t.shape
    ne = u_bf.shape[0]
    tm = _pick_tile(t, (512, 256, 128))
    te = ne // th_rows.shape[0]
    big = pl.BlockSpec((PEER_HEADS, PEER_KEYS, tm), lambda i, j: (0, 0, i))
    rows = pl.BlockSpec((None, th_rows.shape[1], tm), lambda i, j: (j, 0, i))
    return pl.pallas_call(
        functools.partial(_peer_dense_body, te=te),
        grid=(t // tm, ne // te),
        in_specs=[
            pl.BlockSpec((d, tm), lambda i, j: (0, i)),
            pl.BlockSpec((te, d), lambda i, j: (j, 0)),
            pl.BlockSpec((te, d), lambda i, j: (j, 0)),
            big, big, rows, rows,
        ],
        out_specs=pl.BlockSpec((tm, d), lambda i, j: (i, 0)),
        out_shape=jax.ShapeDtypeStruct((t, d), F32),
        compiler_params=_params(("parallel", "arbitrary"), 60),
        name="peer_dense",
    )(hn_t, u_bf, v_bf, s2, e2, th_rows, e1_rows)


def _add_body(a_ref, b_ref, o_ref):
    o_ref[...] = a_ref[...] + b_ref[...]


def add(a, b):
    t, d = a.shape
    tm = _pick_tile(t, (512, 256, 128, 64, 32, 16, 8))
    spec = pl.BlockSpec((tm, d), lambda i: (i, 0))
    return pl.pallas_call(
        _add_body,
        grid=(t // tm,),
        in_specs=[spec, spec],
        out_specs=spec,
        out_shape=jax.ShapeDtypeStruct((t, d), F32),
        compiler_params=_params(("parallel",), 56),
        name="add",
    )(a, b)


PEER_TILE = 512


def peer_ffn_residual(h, gain, w_q_bf, keys, u_bf, v_bf):
    q, hn = rms_matmul(h, gain, w_q_bf, with_xn=True)
    theta, e1, s2, e2 = peer_route(q, keys)
    ne = u_bf.shape[0]
    te = min(PEER_TILE, ne)
    ntile = ne // te
    rows_per_tile = te // PEER_KEYS

    def tile_rows(a):
        t = a.shape[-1]
        a = a.reshape(PEER_HEADS, ntile, rows_per_tile, t).transpose(1, 0, 2, 3)
        return a.reshape(ntile, PEER_HEADS * rows_per_tile, t)

    out = peer_dense(hn.T, u_bf, v_bf, s2, e2, tile_rows(theta), tile_rows(e1))
    return add(h, out)


def _final_norm_body(x_ref, g_ref, o_ref):
    x = x_ref[...]
    inv = lax.rsqrt(jnp.mean(x * x, axis=-1, keepdims=True) + NORM_EPS)
    o_ref[...] = (x * inv) * g_ref[...]


def final_norm(h, gain):
    t, d = h.shape
    tm = _pick_tile(t, (256, 128, 64, 32, 16, 8))
    return pl.pallas_call(
        _final_norm_body,
        grid=(t // tm,),
        in_specs=[pl.BlockSpec((tm, d), lambda i: (i, 0)), pl.BlockSpec((1, d), lambda i: (0, 0))],
        out_specs=pl.BlockSpec((tm, d), lambda i: (i, 0)),
        out_shape=jax.ShapeDtypeStruct((t, d), F32),
        compiler_params=_params(("parallel",), 32),
        name="final_norm",
    )(h, gain.reshape(1, d))


def _even_layer_weights(w_in, shift_mu, w2, a2, width):
    pool_w = w_in.shape[1] - shift_mu.shape[0]
    c0 = pool_w + 3 * width
    c1 = c0 + RWKV_W_RANK
    c2 = c1 + RWKV_A_RANK
    d = w_in.shape[0]
    zw = jnp.zeros((d, LANES - RWKV_W_RANK), w_in.dtype)
    za = jnp.zeros((d, LANES - RWKV_A_RANK), w_in.dtype)
    w_cat = jnp.concatenate([w_in[:, :c0], w_in[:, c0:c1], zw, w_in[:, c1:c2], za, w_in[:, c2:]],
                            axis=1).astype(BF16)
    s0 = 3 * width
    s1 = s0 + RWKV_W_RANK
    s2 = s1 + RWKV_A_RANK
    mu = jnp.concatenate([
        jnp.zeros((pool_w,), F32), shift_mu[:s0], shift_mu[s0:s1],
        jnp.zeros((LANES - RWKV_W_RANK,), F32), shift_mu[s1:s2],
        jnp.zeros((LANES - RWKV_A_RANK,), F32), shift_mu[s2:]]).reshape(1, -1)
    w2p = jnp.pad(w2, ((0, LANES - RWKV_W_RANK), (0, 0)))
    a2p = jnp.pad(a2, ((0, LANES - RWKV_A_RANK), (0, 0)))
    return w_cat, mu, w2p, a2p, pool_w


def kernel(x, meta_tokens, mix_norm_gain, ffn_norm_gain, final_norm_gain, w_in_even, pool_w, pool_scale, shift_mu, rwkv_w0, rwkv_w2, rwkv_a0, rwkv_a2, rwkv_g2, rwkv_k_k, rwkv_k_a, rwkv_r_k, rwkv_gn_gain, rwkv_gn_bias, w_out_even, w_in_odd, ssm_lam_re, ssm_lam_im, ssm_log_dt, ssm_b_re, ssm_b_im, ssm_c_re, ssm_c_im, ssm_d, w_glu, peer_w_q, peer_sub_keys, peer_u, peer_v):
    bsz, seq, d = x.shape
    depth = mix_norm_gain.shape[0]
    real = N_META + seq
    length = -(-real // SEQ_ALIGN) * SEQ_ALIGN
    meta = jnp.broadcast_to(meta_tokens[None].astype(x.dtype), (bsz, N_META, d))
    h = jnp.concatenate([meta, x, jnp.zeros((bsz, length - real, d), x.dtype)], axis=1)
    h = h.reshape(bsz * length, d)

    for layer in range(depth):
        if layer % 2 == 0:
            i = layer // 2
            width = rwkv_w0.shape[-1]
            w_cat, mu, w2p, a2p, pool_cols = _even_layer_weights(
                w_in_even[i], shift_mu[i], rwkv_w2[i], rwkv_a2[i], width)
            z = rms_matmul(h, mix_norm_gain[layer], w_cat)
            z3 = z.reshape(bsz, length, -1)
            y_pool = pool_mixer(z3, pool_w[i], pool_scale[i])
            ob = pool_cols // 512
            wb = width // 512
            r, k, v, kap, b, lw, g = rwkv_prep(
                z3, (ob, ob + wb, ob + 2 * wb, ob + 3 * wb), mu, rwkv_w0[i], w2p, rwkv_a0[i], a2p,
                rwkv_g2[i], rwkv_k_k[i], rwkv_k_a[i])
            y_rwkv = rwkv_scan(r, k, v, kap, b, lw, g, rwkv_r_k[i], rwkv_gn_gain[i],
                               rwkv_gn_bias[i])
            h = proj2_residual(y_pool.reshape(bsz * length, -1), y_rwkv.reshape(bsz * length, -1),
                               w_out_even[i].astype(BF16), h)
        else:
            j = layer // 2
            u = rms_matmul(h, mix_norm_gain[layer], w_in_odd[j].astype(BF16), out_dtype=BF16)
            y = s5_mixer(u.reshape(bsz, length, -1), ssm_lam_re[j], ssm_lam_im[j], ssm_log_dt[j],
                         ssm_b_re[j], ssm_b_im[j], ssm_c_re[j], ssm_c_im[j], ssm_d[j])
            h = glu_residual(y, w_glu[j].astype(BF16), h)
        h = peer_ffn_residual(h, ffn_norm_gain[layer], peer_w_q[layer].astype(BF16),
                              peer_sub_keys[layer], peer_u[layer].astype(BF16),
                              peer_v[layer].astype(BF16))
    out = final_norm(h, final_norm_gain).reshape(bsz, length, d)
    return out[:, N_META:real]
```

```python
import functools

import jax
import jax.numpy as jnp
from jax import lax
from jax.experimental import pallas as pl
from jax.experimental.pallas import tpu as pltpu

F32 = jnp.float32
BF16 = jnp.bfloat16

NORM_EPS = 1e-6
N_META = 16
POOL_WINDOWS = (2, 4, 8, 16)
RWKV_HEAD = 64
RWKV_W_RANK = 96
RWKV_A_RANK = 96
RWKV_G_RANK = 256
RWKV_GN_EPS = 64e-5
RWKV_CHUNK = 64
SSM_GROUP = 16
SSM_STATE = 64
SSM_CHUNK = 16
PEER_KEYS = 128
PEER_HEADS = 8
PEER_TOPK = 16
LANES = 128
SEQ_ALIGN = 128

NEG_INF = float("-inf")


def _params(sem, vmem_mb):
    return pltpu.CompilerParams(dimension_semantics=sem, vmem_limit_bytes=vmem_mb << 20)


def _split2(x):
    hi = x.astype(BF16)
    lo = (x - hi.astype(F32)).astype(BF16)
    return hi, lo


def _split3(x):
    hi = x.astype(BF16)
    r = x - hi.astype(F32)
    mid = r.astype(BF16)
    lo = (r - mid.astype(F32)).astype(BF16)
    return hi, mid, lo


def _dot(a, b):
    return jnp.dot(a, b, preferred_element_type=F32)


def _dot_nt(a, b):
    return lax.dot_general(a, b, (((1,), (1,)), ((), ())), preferred_element_type=F32)


def _dot_acc(a, b):
    ah, al = _split2(a)
    bh, bl = _split2(b)
    return _dot(ah, bh) + _dot(al, bh) + _dot(ah, bl)


def _dot_nt_acc(a, b):
    ah, al = _split2(a)
    bh, bl = _split2(b)
    return _dot_nt(ah, bh) + _dot_nt(al, bh) + _dot_nt(ah, bl)


def _gelu(x):
    return 0.5 * x * (1.0 + lax.erf(x * 0.7071067811865476))


def _sigmoid(x):
    return 1.0 / (1.0 + jnp.exp(-x))


def _pick_tile(n, prefs):
    for t in prefs:
        if n % t == 0:
            return t
    return n


def _rms_norm_body(x_ref, g_ref, o_ref, *rest):
    x = x_ref[...]
    inv = lax.rsqrt(jnp.mean(x * x, axis=-1, keepdims=True) + NORM_EPS)
    xn = (x * inv) * g_ref[...]
    o_ref[...] = xn.astype(BF16)
    if rest:
        rest[0][...] = xn.T.astype(BF16)


def rms_norm_bf16(x, gain, *, with_transpose=False):
    t, d = x.shape
    tm = _pick_tile(t, (256, 128, 64, 32, 16, 8))
    out_shape = [jax.ShapeDtypeStruct((t, d), BF16)]
    out_specs = [pl.BlockSpec((tm, d), lambda i: (i, 0))]
    if with_transpose:
        out_shape.append(jax.ShapeDtypeStruct((d, t), BF16))
        out_specs.append(pl.BlockSpec((d, tm), lambda i: (0, i)))
    res = pl.pallas_call(
        _rms_norm_body,
        grid=(t // tm,),
        in_specs=[pl.BlockSpec((tm, d), lambda i: (i, 0)), pl.BlockSpec((1, d), lambda i: (0, 0))],
        out_specs=out_specs,
        out_shape=out_shape,
        compiler_params=_params(("parallel",), 48),
        name="rms_norm",
    )(x, gain.reshape(1, d))
    return res if with_transpose else res[0]


def _row_tile(t):
    for cand in (1088, 1024, 768, 512, 384, 256, 128, 64, 32, 16):
        if t % cand == 0:
            return cand
    return t


def _matmul_body(a_ref, w_ref, o_ref):
    o_ref[...] = _dot(a_ref[...], w_ref[...]).astype(o_ref.dtype)


def matmul(a, w, *, out_dtype=F32):
    t, k = a.shape
    n = w.shape[1]
    tm = _row_tile(t)
    tn = _pick_tile(n, (512, 256, 128))
    return pl.pallas_call(
        _matmul_body,
        grid=(t // tm, n // tn),
        in_specs=[pl.BlockSpec((tm, k), lambda i, j: (i, 0)), pl.BlockSpec((k, tn), lambda i, j: (0, j))],
        out_specs=pl.BlockSpec((tm, tn), lambda i, j: (i, j)),
        out_shape=jax.ShapeDtypeStruct((t, n), out_dtype),
        compiler_params=_params(("parallel", "arbitrary"), 48),
        name="matmul",
    )(a, w)


def _proj2_body(a1_ref, a2_ref, w1_ref, w2_ref, h_ref, o_ref):
    o_ref[...] = h_ref[...] + _dot(a1_ref[...], w1_ref[...]) + _dot(a2_ref[...], w2_ref[...])


def proj2_residual(a1, a2, w, h):
    t, k1 = a1.shape
    k2 = a2.shape[1]
    n = w.shape[1]
    assert k1 == k2
    tm = _row_tile(t)
    tn = _pick_tile(n, (512, 256, 128))
    return pl.pallas_call(
        _proj2_body,
        grid=(t // tm, n // tn),
        in_specs=[
            pl.BlockSpec((tm, k1), lambda i, j: (i, 0)),
            pl.BlockSpec((tm, k2), lambda i, j: (i, 0)),
            pl.BlockSpec((k1, tn), lambda i, j: (0, j)),
            pl.BlockSpec((k2, tn), lambda i, j: (1, j)),
            pl.BlockSpec((tm, tn), lambda i, j: (i, j)),
        ],
        out_specs=pl.BlockSpec((tm, tn), lambda i, j: (i, j)),
        out_shape=jax.ShapeDtypeStruct((t, n), F32),
        compiler_params=_params(("parallel", "arbitrary"), 56),
        name="proj2_residual",
    )(a1, a2, w, w, h)


def _glu_body(y_ref, wa_ref, wb_ref, h_ref, o_ref):
    y = y_ref[...]
    ga = _dot(y, wa_ref[...])
    gb = _dot(y, wb_ref[...])
    o_ref[...] = h_ref[...] + ga * _sigmoid(gb)


def glu_residual(y, w, h):
    t, k = y.shape
    n = w.shape[1] // 2
    tm = _row_tile(t)
    tn = _pick_tile(n, (512, 256, 128))
    nb = n // tn
    return pl.pallas_call(
        _glu_body,
        grid=(t // tm, nb),
        in_specs=[
            pl.BlockSpec((tm, k), lambda i, j: (i, 0)),
            pl.BlockSpec((k, tn), lambda i, j: (0, j)),
            pl.BlockSpec((k, tn), lambda i, j: (0, j + nb)),
            pl.BlockSpec((tm, tn), lambda i, j: (i, j)),
        ],
        out_specs=pl.BlockSpec((tm, tn), lambda i, j: (i, j)),
        out_shape=jax.ShapeDtypeStruct((t, n), F32),
        compiler_params=_params(("parallel", "arbitrary"), 56),
        name="glu_residual",
    )(y, w, w, h)


def _pool_body(z_ref, w_ref, s_ref, o_ref):
    grp = pl.program_id(1)
    z = z_ref[...]
    length = z.shape[0]
    row = lax.broadcasted_iota(jnp.int32, (length, 1), 0)
    posf = (row + 1).astype(F32)

    def shifted(x, k):
        return jnp.where(row >= k, pltpu.roll(x, k, axis=0), 0.0)

    for gi, win in enumerate(POOL_WINDOWS):

        @pl.when(grp == gi)
        def _(win=win):
            acc = z
            span = 1
            while span < win:
                acc = acc + shifted(acc, span)
                span *= 2
            cnt = jnp.minimum(posf, float(win))
            m = acc / cnt - z
            y = _dot(m.astype(BF16), w_ref[...])
            o_ref[...] = (y * s_ref[...]).astype(o_ref.dtype)


def pool_mixer(z3, pool_w, pool_scale):
    bsz, length, _ = z3.shape
    ng, c, _ = pool_w.shape
    return pl.pallas_call(
        _pool_body,
        grid=(bsz, ng),
        in_specs=[
            pl.BlockSpec((None, length, c), lambda b, g: (b, 0, g)),
            pl.BlockSpec((None, c, c), lambda b, g: (g, 0, 0)),
            pl.BlockSpec((1, c), lambda b, g: (0, g)),
        ],
        out_specs=pl.BlockSpec((None, length, c), lambda b, g: (b, 0, g)),
        out_shape=jax.ShapeDtypeStruct((bsz, length, ng * c), BF16),
        compiler_params=_params(("parallel", "arbitrary"), 48),
        name="pool_mixer",
    )(z3, pool_w.astype(BF16), pool_scale.reshape(1, ng * c))


def _head_sum_matrix(width, head):
    shift = head.bit_length() - 1
    r = lax.broadcasted_iota(jnp.int32, (width, width), 0) >> shift
    c = lax.broadcasted_iota(jnp.int32, (width, width), 1) >> shift
    return jnp.where(r == c, 1.0, 0.0).astype(BF16)


def _head_sums(x, ones_bd):
    hi, lo = _split2(x)
    return _dot(hi, ones_bd) + _dot(lo, ones_bd)


def _rwkv_prep_body(zr_ref, zk_ref, zv_ref, zl_ref, pr_ref, pk_ref, pv_ref, pl_ref,
                    mur_ref, muk_ref, muv_ref, mul_ref, w0_ref, w2_ref, a0_ref, a2_ref, g2_ref,
                    kk_ref, ka_ref,
                    r_out, k_out, v_out, kap_out, b_out, lw_out, g_out):
    first = pl.program_id(1) == 0
    tl = zr_ref.shape[0]
    row = lax.broadcasted_iota(jnp.int32, (tl, 1), 0)

    def mix(z_ref, p_ref, mu_ref):
        z = z_ref[...]
        prev_last = jnp.where(first, 0.0, p_ref[7:8, :])
        prev = jnp.where(row == 0, prev_last, pltpu.roll(z, 1, axis=0))
        return z + (prev - z) * mu_ref[...]

    r = mix(zr_ref, pr_ref, mur_ref)
    k = mix(zk_ref, pk_ref, muk_ref)
    v = mix(zv_ref, pv_ref, muv_ref)
    lora = mix(zl_ref, pl_ref, mul_ref)
    w_lr = lora[:, 0:LANES]
    a_lr = lora[:, LANES:2 * LANES]
    g_lr = lora[:, 2 * LANES:]

    x = w0_ref[...] + _dot_acc(jnp.tanh(w_lr), w2_ref[...])
    y = -x
    softplus = jnp.maximum(y, 0.0) + jnp.log(1.0 + jnp.exp(-jnp.abs(y)))
    lw = -jnp.exp(-softplus - 0.5)
    a = _sigmoid(a0_ref[...] + _dot_acc(a_lr, a2_ref[...]))
    g = _dot_acc(_sigmoid(g_lr), g2_ref[...])

    ones_bd = _head_sum_matrix(LANES, RWKV_HEAD)
    kk = k * kk_ref[...]
    kk2 = kk * kk
    width = k.shape[1]
    ss = jnp.concatenate(
        [_head_sums(kk2[:, c:c + LANES], ones_bd) for c in range(0, width, LANES)], axis=1)
    kap = kk / jnp.maximum(jnp.sqrt(ss), 1e-12)

    r_out[...] = r
    k_out[...] = k * (1.0 + (a - 1.0) * ka_ref[...])
    v_out[...] = v
    kap_out[...] = kap
    b_out[...] = kap * a
    lw_out[...] = lw
    g_out[...] = g


def rwkv_prep(z3, off_blocks, mu, w0, w2p, a0, a2p, g2, k_k, k_a):
    bsz, length, _ = z3.shape
    width = w0.shape[-1]
    cw = 512
    ncb = width // cw
    tl = _pick_tile(length, (352, 272, 192, 176, 136, 128, 64))
    ob_r, ob_k, ob_v, ob_l = off_blocks
    t8 = tl // 8

    def cur(ob, fixed=False):
        if fixed:
            return pl.BlockSpec((None, tl, cw), lambda b, i, c: (b, i, ob))
        return pl.BlockSpec((None, tl, cw), lambda b, i, c: (b, i, ob + c))

    def prev(ob, fixed=False):
        if fixed:
            return pl.BlockSpec((None, 8, cw), lambda b, i, c: (b, jnp.maximum(i * t8 - 1, 0), ob))
        return pl.BlockSpec((None, 8, cw), lambda b, i, c: (b, jnp.maximum(i * t8 - 1, 0), ob + c))

    def vec(ob, fixed=False):
        if fixed:
            return pl.BlockSpec((1, cw), lambda b, i, c: (0, ob))
        return pl.BlockSpec((1, cw), lambda b, i, c: (0, ob + c))

    colvec = pl.BlockSpec((1, cw), lambda b, i, c: (0, c))
    out_spec = pl.BlockSpec((None, tl, cw), lambda b, i, c: (b, i, c))
    out_sds = jax.ShapeDtypeStruct((bsz, length, width), F32)
    return pl.pallas_call(
        _rwkv_prep_body,
        grid=(bsz, length // tl, ncb),
        in_specs=[
            cur(ob_r), cur(ob_k), cur(ob_v), cur(ob_l, True),
            prev(ob_r), prev(ob_k), prev(ob_v), prev(ob_l, True),
            vec(ob_r), vec(ob_k), vec(ob_v), vec(ob_l, True),
            colvec,
            pl.BlockSpec((LANES, cw), lambda b, i, c: (0, c)),
            colvec,
            pl.BlockSpec((LANES, cw), lambda b, i, c: (0, c)),
            pl.BlockSpec((RWKV_G_RANK, cw), lambda b, i, c: (0, c)),
            colvec, colvec,
        ],
        out_specs=[out_spec] * 7,
        out_shape=[out_sds] * 7,
        compiler_params=_params(("parallel", "parallel", "arbitrary"), 56),
        name="rwkv_prep",
    )(z3, z3, z3, z3, z3, z3, z3, z3, mu, mu, mu, mu,
      w0.reshape(1, width), w2p, a0.reshape(1, width), a2p, g2,
      k_k.reshape(1, width), k_a.reshape(1, width))


def _rwkv_scan_body(r_ref, k_ref, v_ref, kap_ref, b_ref, lw_ref, g_ref, rk_ref, gain_ref, bias_ref,
                    o_ref, h_ref, *, pairs):
    cn = RWKV_CHUNK
    hd = RWKV_HEAD

    @pl.when(pl.program_id(2) == 0)
    def _():
        h_ref[...] = jnp.zeros_like(h_ref)

    lane = lax.broadcasted_iota(jnp.int32, (1, LANES), 1)
    m0 = lane < hd
    rowi = lax.broadcasted_iota(jnp.int32, (cn, LANES), 0)
    coli = lax.broadcasted_iota(jnp.int32, (cn, LANES), 1) & (hd - 1)
    strict = rowi > coli
    incl = rowi >= coli
    eye2 = jnp.where(rowi == coli, 1.0, 0.0)
    tri = jnp.where(lax.broadcasted_iota(jnp.int32, (cn, cn), 0)
                    >= lax.broadcasted_iota(jnp.int32, (cn, cn), 1), 1.0, 0.0).astype(BF16)
    br = lax.broadcasted_iota(jnp.int32, (LANES, LANES), 0) >> (hd.bit_length() - 1)
    bc = lax.broadcasted_iota(jnp.int32, (LANES, LANES), 1) >> (hd.bit_length() - 1)
    bd_mask = br == bc
    ones_bd = jnp.where(bd_mask, 1.0, 0.0).astype(BF16)

    def stk(x):
        return jnp.concatenate([jnp.where(m0, x, 0.0), jnp.where(m0, 0.0, x)], axis=0).astype(BF16)

    ps = range(pairs)
    sls = [slice(p * LANES, (p + 1) * LANES) for p in ps]
    r = [r_ref[:, s] for s in sls]
    k = [k_ref[:, s] for s in sls]
    v = [v_ref[:, s] for s in sls]
    kap = [kap_ref[:, s] for s in sls]
    b = [b_ref[:, s] for s in sls]
    lw = [lw_ref[:, s] for s in sls]
    hbd = [h_ref[p] for p in ps]

    lsp = [_split3(x) for x in lw]
    lp = [_dot(tri, a) + _dot(tri, m) + _dot(tri, c) for a, m, c in lsp]
    lpc = [x[cn - 1:cn, :] for x in lp]
    pinv = [jnp.exp(-x) for x in lp]
    pend = [jnp.exp(c - x) for c, x in zip(lpc, lp)]
    kt = [kap[p] * jnp.exp(lp[p] - lw[p]) for p in ps]
    rt = [r[p] * jnp.exp(lp[p]) for p in ps]
    kh = [k[p] * pinv[p] for p in ps]
    bh = [b[p] * pinv[p] for p in ps]
    khc = [k[p] * pend[p] for p in ps]
    bhc = [b[p] * pend[p] for p in ps]

    ktrt = [jnp.concatenate([kt[p], rt[p]], axis=0).astype(BF16) for p in ps]
    ab_all = [_dot_nt(ktrt[p], stk(bh[p])) for p in ps]
    ak_all = [_dot_nt(ktrt[p], stk(kh[p])) for p in ps]
    n_mat = [jnp.where(strict, x[:cn], 0.0) for x in ab_all]
    a_k = [jnp.where(strict, x[:cn], 0.0) for x in ak_all]
    a_rb = [jnp.where(incl, x[cn:], 0.0) for x in ab_all]
    a_rk = [jnp.where(incl, x[cn:], 0.0) for x in ak_all]

    t_mat = [eye2 - x for x in n_mat]
    pw = n_mat
    span = 1
    while 2 * span < cn:
        pw = [_dot(x.astype(BF16), stk(x)) for x in pw]
        t_mat = [t + _dot(t.astype(BF16), stk(x)) for t, x in zip(t_mat, pw)]
        span *= 2
    t_bf = [t.astype(BF16) for t in t_mat]

    kbar = [_dot(t_bf[p], stk(kt[p])) for p in ps]
    akv = [_dot(a_k[p].astype(BF16), stk(v[p])) for p in ps]
    ubar = [_dot(t_bf[p], stk(akv[p])) for p in ps]
    hb = [x.astype(BF16) for x in hbd]
    u = [_dot(kbar[p].astype(BF16), hb[p]) + ubar[p] for p in ps]
    y = [_dot(rt[p].astype(BF16), hb[p]) + _dot(a_rk[p].astype(BF16), stk(v[p]))
         - _dot(a_rb[p].astype(BF16), stk(u[p])) for p in ps]

    kb_t = [jnp.concatenate([khc[p], -bhc[p]], axis=0).T.astype(BF16) for p in ps]
    vu = [jnp.concatenate([v[p], u[p]], axis=0).astype(BF16) for p in ps]
    g_mat = [_dot(kb_t[p], vu[p]) for p in ps]
    pc_col = [jnp.exp(jnp.broadcast_to(x, (LANES, LANES)).T) for x in lpc]
    for p in ps:
        h_ref[p] = jnp.where(bd_mask, pc_col[p] * hbd[p] + g_mat[p], 0.0)

    mean = [_head_sums(x, ones_bd) * (1.0 / hd) for x in y]
    yc = [y[p] - mean[p] for p in ps]
    var = [_head_sums(x * x, ones_bd) * (1.0 / hd) for x in yc]
    bonus = [_head_sums(r[p] * k[p] * rk_ref[:, sls[p]], ones_bd) * v[p] for p in ps]
    for p in ps:
        yn = yc[p] * lax.rsqrt(var[p] + RWKV_GN_EPS) * gain_ref[:, sls[p]] + bias_ref[:, sls[p]]
        o_ref[:, sls[p]] = ((yn + bonus[p]) * g_ref[:, sls[p]]).astype(o_ref.dtype)


def rwkv_scan(r, k, v, kap, b, lw, g, r_k, gn_gain, gn_bias):
    bsz, length, width = r.shape
    cw = _pick_tile(width, (1024, 512))
    pairs = cw // LANES
    seq = pl.BlockSpec((None, RWKV_CHUNK, cw), lambda bi, c, t: (bi, t, c))
    vec = pl.BlockSpec((1, cw), lambda bi, c, t: (0, c))
    return pl.pallas_call(
        functools.partial(_rwkv_scan_body, pairs=pairs),
        grid=(bsz, width // cw, length // RWKV_CHUNK),
        in_specs=[seq] * 7 + [vec] * 3,
        out_specs=seq,
        out_shape=jax.ShapeDtypeStruct((bsz, length, width), BF16),
        scratch_shapes=[pltpu.VMEM((pairs, LANES, LANES), F32)],
        compiler_params=_params(("parallel", "parallel", "arbitrary"), 48),
        name="rwkv_scan",
    )(r, k, v, kap, b, lw, g, r_k.reshape(1, width), gn_gain.reshape(1, width),
      gn_bias.reshape(1, width))


def _s5_param_body(lr_ref, li_ref, ldt_ref, btr_ref, bti_ref, cr_ref, ci_ref,
                   kst_ref, etr_ref, eti_ref, car_ref, cai_ref, a16r_ref, a16i_ref, *, groups):
    nc = SSM_CHUNK
    npow = 24
    tau = lax.broadcasted_iota(jnp.int32, (npow, SSM_STATE), 0).astype(F32)
    for gidx in range(groups):
        lr = lr_ref[gidx]
        li = li_ref[gidx]
        dt = jnp.exp(ldt_ref[gidx])
        mag = jnp.exp(tau * (lr * dt))
        ang = tau * (li * dt)
        pr = mag * jnp.cos(ang)
        pi = mag * jnp.sin(ang)
        ar = pr[1:2]
        ai = pi[1:2]
        den = lr * lr + li * li
        fr = ((ar - 1.0) * lr + ai * li) / den
        fi = (ai * lr - (ar - 1.0) * li) / den
        btr = btr_ref[gidx]
        bti = bti_ref[gidx]
        bbr = fr * btr - fi * bti
        bbi = fr * bti + fi * btr
        cr = cr_ref[gidx]
        ci = ci_ref[gidx]
        etr = jnp.concatenate([pr[s:s + 1] * bbr - pi[s:s + 1] * bbi for s in range(nc)], axis=0)
        eti = jnp.concatenate([pr[s:s + 1] * bbi + pi[s:s + 1] * bbr for s in range(nc)], axis=0)
        car = jnp.concatenate([pr[s:s + 1] * cr - pi[s:s + 1] * ci for s in range(1, nc + 1)], axis=0)
        cai = jnp.concatenate([pr[s:s + 1] * ci + pi[s:s + 1] * cr for s in range(1, nc + 1)], axis=0)
        kst_ref[gidx] = _dot_nt_acc(etr, cr) - _dot_nt_acc(eti, ci)
        etr_ref[gidx] = etr
        eti_ref[gidx] = eti
        car_ref[gidx] = car
        cai_ref[gidx] = cai
        a16r_ref[gidx] = jnp.broadcast_to(pr[nc:nc + 1], (8, SSM_STATE))
        a16i_ref[gidx] = jnp.broadcast_to(pi[nc:nc + 1], (8, SSM_STATE))


def s5_params(lam_re, lam_im, log_dt, b_re, b_im, c_re, c_im):
    ng, ns = lam_re.shape
    gc = SSM_GROUP
    gb = 8
    rows = SSM_CHUNK * gc

    def spec(*dims):
        return pl.BlockSpec((gb,) + dims, lambda i: (i,) + (0,) * len(dims))

    outs = [((rows, gc), F32)] + [((rows, ns), F32)] * 4 + [((8, ns), F32)] * 2
    return pl.pallas_call(
        functools.partial(_s5_param_body, groups=gb),
        grid=(ng // gb,),
        in_specs=[spec(1, ns), spec(1, ns), spec(1, 1), spec(gc, ns), spec(gc, ns), spec(gc, ns),
                  spec(gc, ns)],
        out_specs=[spec(*s) for s, _ in outs],
        out_shape=[jax.ShapeDtypeStruct((ng,) + s, dt) for s, dt in outs],
        compiler_params=_params(("parallel",), 32),
        name="s5_params",
    )(lam_re.reshape(ng, 1, ns), lam_im.reshape(ng, 1, ns), log_dt.reshape(ng, 1, 1),
      jnp.swapaxes(b_re, 1, 2), jnp.swapaxes(b_im, 1, 2), c_re, c_im)


def _s5_body(x_ref, toep_ref, wsr_ref, wsi_ref, wor_ref, woi_ref, a16r_ref, a16i_ref, d_ref,
             o_ref, sre_ref, sim_ref, xre_ref, xim_ref):
    rows = x_ref.shape[1]
    nchunk = rows // 8
    x0 = x_ref[0]
    x1 = x_ref[1]
    sre_ref[...] = _dot(x0, wsr_ref[0]) + _dot(x1, wsr_ref[1])
    sim_ref[...] = _dot(x0, wsi_ref[0]) + _dot(x1, wsi_ref[1])
    ar = a16r_ref[...]
    ai = a16i_ref[...]

    def step(c, carry):
        xr, xi = carry
        off = pl.multiple_of(c * 8, 8)
        xre_ref[pl.ds(off, 8), :] = xr
        xim_ref[pl.ds(off, 8), :] = xi
        sr = sre_ref[pl.ds(off, 8), :]
        si = sim_ref[pl.ds(off, 8), :]
        return ar * xr - ai * xi + sr, ar * xi + ai * xr + si

    zero = jnp.zeros((8, LANES), F32)
    lax.fori_loop(0, nchunk, step, (zero, zero))
    xr = xre_ref[...].astype(BF16)
    xi = xim_ref[...].astype(BF16)
    for q, xq in enumerate((x0, x1)):
        y = (_dot(xq, toep_ref[q]) + _dot(xr, wor_ref[q]) + _dot(xi, woi_ref[q])
             + xq.astype(F32) * d_ref[q])
        o_ref[q] = _gelu(y).astype(o_ref.dtype)


def s5_apply(xg, toep, wsr, wsi, wor, woi, a16r, a16i, dflat):
    ng, rows, cols = xg.shape

    def spec(*dims):
        return pl.BlockSpec((2,) + dims, lambda i: (i,) + (0,) * len(dims))

    return pl.pallas_call(
        _s5_body,
        grid=(ng // 2,),
        in_specs=[spec(rows, cols), spec(cols, cols), spec(cols, LANES), spec(cols, LANES),
                  spec(LANES, cols), spec(LANES, cols),
                  pl.BlockSpec((None, 8, LANES), lambda i: (i, 0, 0)),
                  pl.BlockSpec((None, 8, LANES), lambda i: (i, 0, 0)),
                  spec(1, cols)],
        out_specs=spec(rows, cols),
        out_shape=jax.ShapeDtypeStruct((ng, rows, cols), BF16),
        scratch_shapes=[pltpu.VMEM((rows, LANES), F32)] * 4,
        compiler_params=_params(("parallel",), 32),
        name="s5_apply",
    )(xg, toep, wsr, wsi, wor, woi, a16r, a16i, dflat)


def s5_mixer(u3, lam_re, lam_im, log_dt, b_re, b_im, c_re, c_im, d_skip):
    bsz, length, width = u3.shape
    ng, ns = lam_re.shape
    gc = SSM_GROUP
    nc = SSM_CHUNK
    kst, etr, eti, car, cai, a16r, a16i = s5_params(lam_re, lam_im, log_dt, b_re, b_im, c_re, c_im)

    k4 = kst.reshape(ng, nc, gc, gc)
    s_idx = jnp.arange(nc)[:, None]
    t_idx = jnp.arange(nc)[None, :]
    lag = t_idx - s_idx
    toep = jnp.where((lag >= 0)[None, :, :, None, None], k4[:, jnp.clip(lag, 0, nc - 1)], 0.0)
    toep = toep.transpose(0, 1, 3, 2, 4).reshape(ng, nc * gc, nc * gc).astype(BF16)
    odd = (jnp.arange(ng) % 2 == 1)[:, None, None]

    def lanes_by_parity(w):
        z = jnp.zeros_like(w)
        return jnp.where(odd, jnp.concatenate([z, w], -1), jnp.concatenate([w, z], -1))

    def flip_lag(e):
        return e.reshape(ng, nc, gc, ns)[:, ::-1].reshape(ng, nc * gc, ns)

    wsr = lanes_by_parity(flip_lag(etr)).astype(BF16)
    wsi = lanes_by_parity(flip_lag(eti)).astype(BF16)
    wor = jnp.swapaxes(lanes_by_parity(car), 1, 2).astype(BF16)
    woi = jnp.swapaxes(lanes_by_parity(-cai), 1, 2).astype(BF16)

    def pair_lanes(a):
        return a.reshape(ng // 2, 2, 8, ns).transpose(0, 2, 1, 3).reshape(ng // 2, 8, 2 * ns)

    dflat = jnp.tile(d_skip.reshape(ng, 1, gc), (1, 1, nc))

    nchunk = length // nc
    x = u3.reshape(bsz, nchunk, nc, ng, gc).transpose(3, 1, 0, 2, 4)
    x = jnp.pad(x, ((0, 0), (0, 0), (0, 8 - bsz), (0, 0), (0, 0))).reshape(ng, nchunk * 8, nc * gc)
    y = s5_apply(x, toep, wsr, wsi, wor, woi, pair_lanes(a16r), pair_lanes(a16i), dflat)
    y = y.reshape(ng, nchunk, 8, nc, gc)[:, :, :bsz].transpose(2, 1, 3, 0, 4)
    return y.reshape(bsz * length, width)


def _topk_rows(s, k):
    tops = []
    cur = s
    for _ in range(k):
        m = jnp.max(cur, axis=0, keepdims=True)
        tops.append(m)
        cur = jnp.where(cur >= m, NEG_INF, cur)
    return tops


def _peer_route_body(q_ref, keys_ref, th_ref, e1_ref, s2_ref, e2_ref):
    nk = PEER_KEYS
    k = PEER_TOPK
    tm = q_ref.shape[0]
    wide, tall = 4, 3
    assert k == 16 and (wide + 1) * (tall + 1) > k and 2 * 9 > k
    a_idx = lax.broadcasted_iota(jnp.int32, (k, 1), 0)
    for c in range(tm // LANES):
        ts = slice(c * LANES, (c + 1) * LANES)
        for h in range(PEER_HEADS):
            q1 = q_ref[ts, (2 * h) * nk:(2 * h + 1) * nk]
            q2 = q_ref[ts, (2 * h + 1) * nk:(2 * h + 2) * nk]
            s1 = _dot_nt_acc(keys_ref[h, 0], q1)
            s2 = _dot_nt_acc(keys_ref[h, 1], q2)
            t1 = _topk_rows(s1, k)
            t2 = _topk_rows(s2, k)
            top1 = jnp.concatenate(t1, axis=0)
            top2 = jnp.concatenate(t2, axis=0)
            row_blocks = [t1[0] + top2] + [t1[a] + top2[0:8] for a in range(1, wide)]
            col_blocks = [jnp.where(a_idx >= wide, top1 + t2[b], NEG_INF) for b in range(tall)]
            cand = jnp.concatenate(row_blocks + col_blocks, axis=0)
            best = t1[0] + t2[0]
            zsum = jnp.zeros_like(best)
            cur = cand
            tau = best
            for _ in range(k):
                tau = jnp.max(cur, axis=0, keepdims=True)
                zsum = zsum + jnp.exp(tau - best)
                cur = jnp.where(cur >= tau, NEG_INF, cur)
            th_rows = [jnp.min(jnp.where(row_blocks[0] >= tau, top2, jnp.inf), axis=0, keepdims=True)]
            for a in range(1, wide):
                th_rows.append(jnp.min(jnp.where(row_blocks[a] >= tau, top2[0:8], jnp.inf),
                                       axis=0, keepdims=True))
            th_cols = jnp.where(col_blocks[0] >= tau, t2[0], jnp.inf)
            for b in range(1, tall):
                th_cols = jnp.minimum(th_cols, jnp.where(col_blocks[b] >= tau, t2[b], jnp.inf))
            theta = jnp.full_like(s1, jnp.inf)
            for a in range(k):
                theta_a = th_rows[a] if a < wide else th_cols[a:a + 1]
                theta = jnp.where(s1 == t1[a], theta_a, theta)
            th_ref[h, :, ts] = theta
            s2_ref[h, :, ts] = s2
            e1_ref[h, :, ts] = jnp.exp(s1 - t1[0]) / zsum
            e2_ref[h, :, ts] = jnp.exp(s2 - t2[0])


def peer_route(q, keys):
    t = q.shape[0]
    tm = _pick_tile(t, (256, 128))
    big = pl.BlockSpec((PEER_HEADS, PEER_KEYS, tm), lambda i: (0, 0, i))
    sds = jax.ShapeDtypeStruct((PEER_HEADS, PEER_KEYS, t), F32)
    return pl.pallas_call(
        _peer_route_body,
        grid=(t // tm,),
        in_specs=[pl.BlockSpec((tm, q.shape[1]), lambda i: (i, 0)),
                  pl.BlockSpec(keys.shape, lambda i: (0, 0, 0, 0))],
        out_specs=[big, big, big, big],
        out_shape=[sds, sds, sds, sds],
        compiler_params=_params(("parallel",), 48),
        name="peer_route",
    )(q, keys)


PEER_SLAB = 256


PEER_TILE = 512
SUBLANES = 8


def _peer_dense_body(hnt_ref, u_ref, v_ref, s2_ref, e2_ref, th_ref, e1_ref, o_ref, *, te):
    nk = PEER_KEYS
    rows_per_tile = te // nk
    rows_per_slab = PEER_SLAB // nk
    nslab = te // PEER_SLAB
    j = pl.program_id(1)
    base = (j % (SUBLANES // rows_per_tile)) * rows_per_tile

    @pl.when(j == 0)
    def _():
        o_ref[...] = jnp.zeros_like(o_ref)

    hnt = hnt_ref[...]
    acts = [_dot(u_ref[s * PEER_SLAB:(s + 1) * PEER_SLAB, :], hnt) for s in range(nslab)]
    total = None
    for s in range(nslab):
        parts = []
        for r in range(rows_per_slab):
            row = base + s * rows_per_slab + r
            gate = None
            for h in range(PEER_HEADS):
                keep = s2_ref[h] >= th_ref[h, pl.ds(row, 1), :]
                term = jnp.where(keep, e2_ref[h] * e1_ref[h, pl.ds(row, 1), :], 0.0)
                gate = term if gate is None else gate + term
            parts.append(gate * _gelu(acts[s][r * nk:(r + 1) * nk, :]))
        w = jnp.concatenate(parts, axis=0).T.astype(BF16)
        c = _dot(w, v_ref[s * PEER_SLAB:(s + 1) * PEER_SLAB, :])
        total = c if total is None else total + c
    o_ref[...] += total


def peer_dense(hn_t, u_all, v_all, layer, s2, e2, theta, e1):
    d, t = hn_t.shape
    ne = u_all.shape[1]
    tm = _pick_tile(t, (512, 256, 128))
    te = min(PEER_TILE, ne)
    rows_per_tile = te // PEER_KEYS
    tiles_per_block = SUBLANES // rows_per_tile
    big = pl.BlockSpec((PEER_HEADS, PEER_KEYS, tm), lambda i, j: (0, 0, i))
    rows = pl.BlockSpec((PEER_HEADS, SUBLANES, tm), lambda i, j: (0, j // tiles_per_block, i))
    table = pl.BlockSpec((None, te, d), lambda i, j: (layer, j, 0))
    return pl.pallas_call(
        functools.partial(_peer_dense_body, te=te),
        grid=(t // tm, ne // te),
        in_specs=[pl.BlockSpec((d, tm), lambda i, j: (0, i)), table, table, big, big, rows, rows],
        out_specs=pl.BlockSpec((tm, d), lambda i, j: (i, 0)),
        out_shape=jax.ShapeDtypeStruct((t, d), F32),
        compiler_params=_params(("parallel", "arbitrary"), 60),
        name="peer_dense",
    )(hn_t, u_all, v_all, s2, e2, theta, e1)


def _add_body(a_ref, b_ref, o_ref):
    o_ref[...] = a_ref[...] + b_ref[...]


def add(a, b):
    t, d = a.shape
    tm = _pick_tile(t, (512, 256, 128, 64, 32, 16, 8))
    spec = pl.BlockSpec((tm, d), lambda i: (i, 0))
    return pl.pallas_call(
        _add_body,
        grid=(t // tm,),
        in_specs=[spec, spec],
        out_specs=spec,
        out_shape=jax.ShapeDtypeStruct((t, d), F32),
        compiler_params=_params(("parallel",), 56),
        name="add",
    )(a, b)


def peer_ffn_residual(h, gain, w_q_bf, keys, u_all, v_all, layer):
    hn, hn_t = rms_norm_bf16(h, gain, with_transpose=True)
    q = matmul(hn, w_q_bf)
    theta, e1, s2, e2 = peer_route(q, keys)
    out = peer_dense(hn_t, u_all, v_all, layer, s2, e2, theta, e1)
    return add(h, out)


def _final_norm_body(x_ref, g_ref, o_ref):
    x = x_ref[...]
    inv = lax.rsqrt(jnp.mean(x * x, axis=-1, keepdims=True) + NORM_EPS)
    o_ref[...] = (x * inv) * g_ref[...]


def final_norm(h, gain):
    t, d = h.shape
    tm = _pick_tile(t, (256, 128, 64, 32, 16, 8))
    return pl.pallas_call(
        _final_norm_body,
        grid=(t // tm,),
        in_specs=[pl.BlockSpec((tm, d), lambda i: (i, 0)), pl.BlockSpec((1, d), lambda i: (0, 0))],
        out_specs=pl.BlockSpec((tm, d), lambda i: (i, 0)),
        out_shape=jax.ShapeDtypeStruct((t, d), F32),
        compiler_params=_params(("parallel",), 32),
        name="final_norm",
    )(h, gain.reshape(1, d))


def _even_layer_weights(w_in, shift_mu, w2, a2, width):
    pool_w = w_in.shape[1] - shift_mu.shape[0]
    c0 = pool_w + 3 * width
    c1 = c0 + RWKV_W_RANK
    c2 = c1 + RWKV_A_RANK
    d = w_in.shape[0]
    zw = jnp.zeros((d, LANES - RWKV_W_RANK), w_in.dtype)
    za = jnp.zeros((d, LANES - RWKV_A_RANK), w_in.dtype)
    w_cat = jnp.concatenate([w_in[:, :c0], w_in[:, c0:c1], zw, w_in[:, c1:c2], za, w_in[:, c2:]],
                            axis=1).astype(BF16)
    s0 = 3 * width
    s1 = s0 + RWKV_W_RANK
    s2 = s1 + RWKV_A_RANK
    mu = jnp.concatenate([
        jnp.zeros((pool_w,), F32), shift_mu[:s0], shift_mu[s0:s1],
        jnp.zeros((LANES - RWKV_W_RANK,), F32), shift_mu[s1:s2],
        jnp.zeros((LANES - RWKV_A_RANK,), F32), shift_mu[s2:]]).reshape(1, -1)
    w2p = jnp.pad(w2, ((0, LANES - RWKV_W_RANK), (0, 0)))
    a2p = jnp.pad(a2, ((0, LANES - RWKV_A_RANK), (0, 0)))
    return w_cat, mu, w2p, a2p, pool_w


def kernel(x, meta_tokens, mix_norm_gain, ffn_norm_gain, final_norm_gain, w_in_even, pool_w, pool_scale, shift_mu, rwkv_w0, rwkv_w2, rwkv_a0, rwkv_a2, rwkv_g2, rwkv_k_k, rwkv_k_a, rwkv_r_k, rwkv_gn_gain, rwkv_gn_bias, w_out_even, w_in_odd, ssm_lam_re, ssm_lam_im, ssm_log_dt, ssm_b_re, ssm_b_im, ssm_c_re, ssm_c_im, ssm_d, w_glu, peer_w_q, peer_sub_keys, peer_u, peer_v):
    bsz, seq, d = x.shape
    depth = mix_norm_gain.shape[0]
    real = N_META + seq
    length = -(-real // SEQ_ALIGN) * SEQ_ALIGN
    meta = jnp.broadcast_to(meta_tokens[None].astype(x.dtype), (bsz, N_META, d))
    h = jnp.concatenate([meta, x, jnp.zeros((bsz, length - real, d), x.dtype)], axis=1)
    h = h.reshape(bsz * length, d)
    peer_u_bf = peer_u.astype(BF16)
    peer_v_bf = peer_v.astype(BF16)

    for layer in range(depth):
        if layer % 2 == 0:
            i = layer // 2
            width = rwkv_w0.shape[-1]
            w_cat, mu, w2p, a2p, pool_cols = _even_layer_weights(
                w_in_even[i], shift_mu[i], rwkv_w2[i], rwkv_a2[i], width)
            z = matmul(rms_norm_bf16(h, mix_norm_gain[layer]), w_cat)
            z3 = z.reshape(bsz, length, -1)
            y_pool = pool_mixer(z3, pool_w[i], pool_scale[i])
            ob = pool_cols // 512
            wb = width // 512
            r, k, v, kap, b, lw, g = rwkv_prep(
                z3, (ob, ob + wb, ob + 2 * wb, ob + 3 * wb), mu, rwkv_w0[i], w2p, rwkv_a0[i], a2p,
                rwkv_g2[i], rwkv_k_k[i], rwkv_k_a[i])
            y_rwkv = rwkv_scan(r, k, v, kap, b, lw, g, rwkv_r_k[i], rwkv_gn_gain[i],
                               rwkv_gn_bias[i])
            h = proj2_residual(y_pool.reshape(bsz * length, -1), y_rwkv.reshape(bsz * length, -1),
                               w_out_even[i].astype(BF16), h)
        else:
            j = layer // 2
            u = matmul(rms_norm_bf16(h, mix_norm_gain[layer]), w_in_odd[j].astype(BF16),
                       out_dtype=BF16)
            y = s5_mixer(u.reshape(bsz, length, -1), ssm_lam_re[j], ssm_lam_im[j], ssm_log_dt[j],
                         ssm_b_re[j], ssm_b_im[j], ssm_c_re[j], ssm_c_im[j], ssm_d[j])
            h = glu_residual(y, w_glu[j].astype(BF16), h)
        h = peer_ffn_residual(h, ffn_norm_gain[layer], peer_w_q[layer].astype(BF16),
                              peer_sub_keys[layer], peer_u_bf, peer_v_bf, layer)
    out = final_norm(h, final_norm_gain).reshape(bsz, length, d)
    return out[:, N_META:real]
```

```python
import functools

import jax
import jax.numpy as jnp
from jax import lax
from jax.experimental import pallas as pl
from jax.experimental.pallas import tpu as pltpu

F32 = jnp.float32
BF16 = jnp.bfloat16

NORM_EPS = 1e-6
N_META = 16
POOL_WINDOWS = (2, 4, 8, 16)
RWKV_HEAD = 64
RWKV_W_RANK = 96
RWKV_A_RANK = 96
RWKV_G_RANK = 256
RWKV_GN_EPS = 64e-5
RWKV_CHUNK = 64
SSM_GROUP = 16
SSM_STATE = 64
SSM_CHUNK = 16
PEER_KEYS = 128
PEER_HEADS = 8
PEER_TOPK = 16
LANES = 128
SEQ_ALIGN = 128

NEG_INF = float("-inf")


def _params(sem, vmem_mb):
    return pltpu.CompilerParams(dimension_semantics=sem, vmem_limit_bytes=vmem_mb << 20)


def _split2(x):
    hi = x.astype(BF16)
    lo = (x - hi.astype(F32)).astype(BF16)
    return hi, lo


def _split3(x):
    hi = x.astype(BF16)
    r = x - hi.astype(F32)
    mid = r.astype(BF16)
    lo = (r - mid.astype(F32)).astype(BF16)
    return hi, mid, lo


def _dot(a, b):
    return jnp.dot(a, b, preferred_element_type=F32)


def _dot_nt(a, b):
    return lax.dot_general(a, b, (((1,), (1,)), ((), ())), preferred_element_type=F32)


def _dot_acc(a, b):
    ah, al = _split2(a)
    bh, bl = _split2(b)
    return _dot(ah, bh) + _dot(al, bh) + _dot(ah, bl)


def _dot_nt_acc(a, b):
    ah, al = _split2(a)
    bh, bl = _split2(b)
    return _dot_nt(ah, bh) + _dot_nt(al, bh) + _dot_nt(ah, bl)


def _gelu(x):
    return 0.5 * x * (1.0 + lax.erf(x * 0.7071067811865476))


def _sigmoid(x):
    return 1.0 / (1.0 + jnp.exp(-x))


def _pick_tile(n, prefs):
    for t in prefs:
        if n % t == 0:
            return t
    return n


def _rms_norm_body(x_ref, g_ref, o_ref, *rest):
    x = x_ref[...]
    inv = lax.rsqrt(jnp.mean(x * x, axis=-1, keepdims=True) + NORM_EPS)
    xn = (x * inv) * g_ref[...]
    o_ref[...] = xn.astype(BF16)
    if rest:
        rest[0][...] = xn.T.astype(BF16)


def rms_norm_bf16(x, gain, *, with_transpose=False):
    t, d = x.shape
    tm = _pick_tile(t, (256, 128, 64, 32, 16, 8))
    out_shape = [jax.ShapeDtypeStruct((t, d), BF16)]
    out_specs = [pl.BlockSpec((tm, d), lambda i: (i, 0))]
    if with_transpose:
        out_shape.append(jax.ShapeDtypeStruct((d, t), BF16))
        out_specs.append(pl.BlockSpec((d, tm), lambda i: (0, i)))
    res = pl.pallas_call(
        _rms_norm_body,
        grid=(t // tm,),
        in_specs=[pl.BlockSpec((tm, d), lambda i: (i, 0)), pl.BlockSpec((1, d), lambda i: (0, 0))],
        out_specs=out_specs,
        out_shape=out_shape,
        compiler_params=_params(("parallel",), 48),
        name="rms_norm",
    )(x, gain.reshape(1, d))
    return res if with_transpose else res[0]


def _row_tile(t):
    for cand in (1088, 1024, 768, 512, 384, 256, 128, 64, 32, 16):
        if t % cand == 0:
            return cand
    return t


def _matmul_body(a_ref, w_ref, o_ref):
    o_ref[...] = _dot(a_ref[...], w_ref[...]).astype(o_ref.dtype)


def matmul(a, w, *, out_dtype=F32):
    t, k = a.shape
    n = w.shape[1]
    tm = _row_tile(t)
    tn = _pick_tile(n, (512, 256, 128))
    return pl.pallas_call(
        _matmul_body,
        grid=(t // tm, n // tn),
        in_specs=[pl.BlockSpec((tm, k), lambda i, j: (i, 0)), pl.BlockSpec((k, tn), lambda i, j: (0, j))],
        out_specs=pl.BlockSpec((tm, tn), lambda i, j: (i, j)),
        out_shape=jax.ShapeDtypeStruct((t, n), out_dtype),
        compiler_params=_params(("parallel", "arbitrary"), 48),
        name="matmul",
    )(a, w)


def _proj2_body(a1_ref, a2_ref, w1_ref, w2_ref, h_ref, o_ref):
    o_ref[...] = h_ref[...] + _dot(a1_ref[...], w1_ref[...]) + _dot(a2_ref[...], w2_ref[...])


def proj2_residual(a1, a2, w, h):
    t, k1 = a1.shape
    k2 = a2.shape[1]
    n = w.shape[1]
    assert k1 == k2
    tm = _row_tile(t)
    tn = _pick_tile(n, (512, 256, 128))
    return pl.pallas_call(
        _proj2_body,
        grid=(t // tm, n // tn),
        in_specs=[
            pl.BlockSpec((tm, k1), lambda i, j: (i, 0)),
            pl.BlockSpec((tm, k2), lambda i, j: (i, 0)),
            pl.BlockSpec((k1, tn), lambda i, j: (0, j)),
            pl.BlockSpec((k2, tn), lambda i, j: (1, j)),
            pl.BlockSpec((tm, tn), lambda i, j: (i, j)),
        ],
        out_specs=pl.BlockSpec((tm, tn), lambda i, j: (i, j)),
        out_shape=jax.ShapeDtypeStruct((t, n), F32),
        compiler_params=_params(("parallel", "arbitrary"), 56),
        name="proj2_residual",
    )(a1, a2, w, w, h)


def _glu_body(y_ref, wa_ref, wb_ref, h_ref, o_ref):
    y = y_ref[...]
    ga = _dot(y, wa_ref[...])
    gb = _dot(y, wb_ref[...])
    o_ref[...] = h_ref[...] + ga * _sigmoid(gb)


def glu_residual(y, w, h):
    t, k = y.shape
    n = w.shape[1] // 2
    tm = _row_tile(t)
    tn = _pick_tile(n, (512, 256, 128))
    nb = n // tn
    return pl.pallas_call(
        _glu_body,
        grid=(t // tm, nb),
        in_specs=[
            pl.BlockSpec((tm, k), lambda i, j: (i, 0)),
            pl.BlockSpec((k, tn), lambda i, j: (0, j)),
            pl.BlockSpec((k, tn), lambda i, j: (0, j + nb)),
            pl.BlockSpec((tm, tn), lambda i, j: (i, j)),
        ],
        out_specs=pl.BlockSpec((tm, tn), lambda i, j: (i, j)),
        out_shape=jax.ShapeDtypeStruct((t, n), F32),
        compiler_params=_params(("parallel", "arbitrary"), 56),
        name="glu_residual",
    )(y, w, w, h)


def _pool_body(z_ref, w_ref, s_ref, o_ref):
    grp = pl.program_id(1)
    z = z_ref[...]
    length = z.shape[0]
    row = lax.broadcasted_iota(jnp.int32, (length, 1), 0)
    posf = (row + 1).astype(F32)

    def shifted(x, k):
        return jnp.where(row >= k, pltpu.roll(x, k, axis=0), 0.0)

    for gi, win in enumerate(POOL_WINDOWS):

        @pl.when(grp == gi)
        def _(win=win):
            acc = z
            span = 1
            while span < win:
                acc = acc + shifted(acc, span)
                span *= 2
            cnt = jnp.minimum(posf, float(win))
            m = acc / cnt - z
            y = _dot(m.astype(BF16), w_ref[...])
            o_ref[...] = (y * s_ref[...]).astype(o_ref.dtype)


def pool_mixer(z3, pool_w, pool_scale):
    bsz, length, _ = z3.shape
    ng, c, _ = pool_w.shape
    return pl.pallas_call(
        _pool_body,
        grid=(bsz, ng),
        in_specs=[
            pl.BlockSpec((None, length, c), lambda b, g: (b, 0, g)),
            pl.BlockSpec((None, c, c), lambda b, g: (g, 0, 0)),
            pl.BlockSpec((1, c), lambda b, g: (0, g)),
        ],
        out_specs=pl.BlockSpec((None, length, c), lambda b, g: (b, 0, g)),
        out_shape=jax.ShapeDtypeStruct((bsz, length, ng * c), BF16),
        compiler_params=_params(("parallel", "arbitrary"), 48),
        name="pool_mixer",
    )(z3, pool_w.astype(BF16), pool_scale.reshape(1, ng * c))


def _head_sum_matrix(width, head):
    shift = head.bit_length() - 1
    r = lax.broadcasted_iota(jnp.int32, (width, width), 0) >> shift
    c = lax.broadcasted_iota(jnp.int32, (width, width), 1) >> shift
    return jnp.where(r == c, 1.0, 0.0).astype(BF16)


def _head_sums(x, ones_bd):
    hi, lo = _split2(x)
    return _dot(hi, ones_bd) + _dot(lo, ones_bd)


def _rwkv_prep_body(zr_ref, zk_ref, zv_ref, zl_ref, pr_ref, pk_ref, pv_ref, pl_ref,
                    mur_ref, muk_ref, muv_ref, mul_ref, w0_ref, w2_ref, a0_ref, a2_ref, g2_ref,
                    kk_ref, ka_ref,
                    r_out, k_out, v_out, kap_out, b_out, lw_out, g_out):
    first = pl.program_id(1) == 0
    tl = zr_ref.shape[0]
    row = lax.broadcasted_iota(jnp.int32, (tl, 1), 0)

    def mix(z_ref, p_ref, mu_ref):
        z = z_ref[...]
        prev_last = jnp.where(first, 0.0, p_ref[7:8, :])
        prev = jnp.where(row == 0, prev_last, pltpu.roll(z, 1, axis=0))
        return z + (prev - z) * mu_ref[...]

    r = mix(zr_ref, pr_ref, mur_ref)
    k = mix(zk_ref, pk_ref, muk_ref)
    v = mix(zv_ref, pv_ref, muv_ref)
    lora = mix(zl_ref, pl_ref, mul_ref)
    w_lr = lora[:, 0:LANES]
    a_lr = lora[:, LANES:2 * LANES]
    g_lr = lora[:, 2 * LANES:]

    x = w0_ref[...] + _dot_acc(jnp.tanh(w_lr), w2_ref[...])
    y = -x
    softplus = jnp.maximum(y, 0.0) + jnp.log(1.0 + jnp.exp(-jnp.abs(y)))
    lw = -jnp.exp(-softplus - 0.5)
    a = _sigmoid(a0_ref[...] + _dot_acc(a_lr, a2_ref[...]))
    g = _dot_acc(_sigmoid(g_lr), g2_ref[...])

    ones_bd = _head_sum_matrix(LANES, RWKV_HEAD)
    kk = k * kk_ref[...]
    kk2 = kk * kk
    width = k.shape[1]
    ss = jnp.concatenate(
        [_head_sums(kk2[:, c:c + LANES], ones_bd) for c in range(0, width, LANES)], axis=1)
    kap = kk / jnp.maximum(jnp.sqrt(ss), 1e-12)

    r_out[...] = r
    k_out[...] = k * (1.0 + (a - 1.0) * ka_ref[...])
    v_out[...] = v
    kap_out[...] = kap
    b_out[...] = kap * a
    lw_out[...] = lw
    g_out[...] = g


def rwkv_prep(z3, off_blocks, mu, w0, w2p, a0, a2p, g2, k_k, k_a):
    bsz, length, _ = z3.shape
    width = w0.shape[-1]
    cw = 512
    ncb = width // cw
    tl = _pick_tile(length, (352, 272, 192, 176, 136, 128, 64))
    ob_r, ob_k, ob_v, ob_l = off_blocks
    t8 = tl // 8

    def cur(ob, fixed=False):
        if fixed:
            return pl.BlockSpec((None, tl, cw), lambda b, i, c: (b, i, ob))
        return pl.BlockSpec((None, tl, cw), lambda b, i, c: (b, i, ob + c))

    def prev(ob, fixed=False):
        if fixed:
            return pl.BlockSpec((None, 8, cw), lambda b, i, c: (b, jnp.maximum(i * t8 - 1, 0), ob))
        return pl.BlockSpec((None, 8, cw), lambda b, i, c: (b, jnp.maximum(i * t8 - 1, 0), ob + c))

    def vec(ob, fixed=False):
        if fixed:
            return pl.BlockSpec((1, cw), lambda b, i, c: (0, ob))
        return pl.BlockSpec((1, cw), lambda b, i, c: (0, ob + c))

    colvec = pl.BlockSpec((1, cw), lambda b, i, c: (0, c))
    out_spec = pl.BlockSpec((None, tl, cw), lambda b, i, c: (b, i, c))
    out_sds = jax.ShapeDtypeStruct((bsz, length, width), F32)
    return pl.pallas_call(
        _rwkv_prep_body,
        grid=(bsz, length // tl, ncb),
        in_specs=[
            cur(ob_r), cur(ob_k), cur(ob_v), cur(ob_l, True),
            prev(ob_r), prev(ob_k), prev(ob_v), prev(ob_l, True),
            vec(ob_r), vec(ob_k), vec(ob_v), vec(ob_l, True),
            colvec,
            pl.BlockSpec((LANES, cw), lambda b, i, c: (0, c)),
            colvec,
            pl.BlockSpec((LANES, cw), lambda b, i, c: (0, c)),
            pl.BlockSpec((RWKV_G_RANK, cw), lambda b, i, c: (0, c)),
            colvec, colvec,
        ],
        out_specs=[out_spec] * 7,
        out_shape=[out_sds] * 7,
        compiler_params=_params(("parallel", "parallel", "arbitrary"), 56),
        name="rwkv_prep",
    )(z3, z3, z3, z3, z3, z3, z3, z3, mu, mu, mu, mu,
      w0.reshape(1, width), w2p, a0.reshape(1, width), a2p, g2,
      k_k.reshape(1, width), k_a.reshape(1, width))


def _rwkv_scan_body(r_ref, k_ref, v_ref, kap_ref, b_ref, lw_ref, g_ref, rk_ref, gain_ref, bias_ref,
                    o_ref, h_ref, *, pairs):
    cn = RWKV_CHUNK
    hd = RWKV_HEAD

    @pl.when(pl.program_id(2) == 0)
    def _():
        h_ref[...] = jnp.zeros_like(h_ref)

    lane = lax.broadcasted_iota(jnp.int32, (1, LANES), 1)
    m0 = lane < hd
    rowi = lax.broadcasted_iota(jnp.int32, (cn, LANES), 0)
    coli = lax.broadcasted_iota(jnp.int32, (cn, LANES), 1) & (hd - 1)
    strict = rowi > coli
    incl = rowi >= coli
    eye2 = jnp.where(rowi == coli, 1.0, 0.0)
    tri = jnp.where(lax.broadcasted_iota(jnp.int32, (cn, cn), 0)
                    >= lax.broadcasted_iota(jnp.int32, (cn, cn), 1), 1.0, 0.0).astype(BF16)
    br = lax.broadcasted_iota(jnp.int32, (LANES, LANES), 0) >> (hd.bit_length() - 1)
    bc = lax.broadcasted_iota(jnp.int32, (LANES, LANES), 1) >> (hd.bit_length() - 1)
    bd_mask = br == bc
    ones_bd = jnp.where(bd_mask, 1.0, 0.0).astype(BF16)

    def stk(x):
        return jnp.concatenate([jnp.where(m0, x, 0.0), jnp.where(m0, 0.0, x)], axis=0).astype(BF16)

    ps = range(pairs)
    sls = [slice(p * LANES, (p + 1) * LANES) for p in ps]
    r = [r_ref[:, s] for s in sls]
    k = [k_ref[:, s] for s in sls]
    v = [v_ref[:, s] for s in sls]
    kap = [kap_ref[:, s] for s in sls]
    b = [b_ref[:, s] for s in sls]
    lw = [lw_ref[:, s] for s in sls]
    hbd = [h_ref[p] for p in ps]

    lsp = [_split3(x) for x in lw]
    lp = [_dot(tri, a) + _dot(tri, m) + _dot(tri, c) for a, m, c in lsp]
    lpc = [x[cn - 1:cn, :] for x in lp]
    pinv = [jnp.exp(-x) for x in lp]
    pend = [jnp.exp(c - x) for c, x in zip(lpc, lp)]
    kt = [kap[p] * jnp.exp(lp[p] - lw[p]) for p in ps]
    rt = [r[p] * jnp.exp(lp[p]) for p in ps]
    kh = [k[p] * pinv[p] for p in ps]
    bh = [b[p] * pinv[p] for p in ps]
    khc = [k[p] * pend[p] for p in ps]
    bhc = [b[p] * pend[p] for p in ps]

    ktrt = [jnp.concatenate([kt[p], rt[p]], axis=0).astype(BF16) for p in ps]
    ab_all = [_dot_nt(ktrt[p], stk(bh[p])) for p in ps]
    ak_all = [_dot_nt(ktrt[p], stk(kh[p])) for p in ps]
    n_mat = [jnp.where(strict, x[:cn], 0.0) for x in ab_all]
    a_k = [jnp.where(strict, x[:cn], 0.0) for x in ak_all]
    a_rb = [jnp.where(incl, x[cn:], 0.0) for x in ab_all]
    a_rk = [jnp.where(incl, x[cn:], 0.0) for x in ak_all]

    t_mat = [eye2 - x for x in n_mat]
    pw = n_mat
    span = 1
    while 2 * span < cn:
        pw = [_dot(x.astype(BF16), stk(x)) for x in pw]
        t_mat = [t + _dot(t.astype(BF16), stk(x)) for t, x in zip(t_mat, pw)]
        span *= 2
    t_bf = [t.astype(BF16) for t in t_mat]

    kbar = [_dot(t_bf[p], stk(kt[p])) for p in ps]
    akv = [_dot(a_k[p].astype(BF16), stk(v[p])) for p in ps]
    ubar = [_dot(t_bf[p], stk(akv[p])) for p in ps]
    hb = [x.astype(BF16) for x in hbd]
    u = [_dot(kbar[p].astype(BF16), hb[p]) + ubar[p] for p in ps]
    y = [_dot(rt[p].astype(BF16), hb[p]) + _dot(a_rk[p].astype(BF16), stk(v[p]))
         - _dot(a_rb[p].astype(BF16), stk(u[p])) for p in ps]

    kb_t = [jnp.concatenate([khc[p], -bhc[p]], axis=0).T.astype(BF16) for p in ps]
    vu = [jnp.concatenate([v[p], u[p]], axis=0).astype(BF16) for p in ps]
    g_mat = [_dot(kb_t[p], vu[p]) for p in ps]
    pc_col = [jnp.exp(jnp.broadcast_to(x, (LANES, LANES)).T) for x in lpc]
    for p in ps:
        h_ref[p] = jnp.where(bd_mask, pc_col[p] * hbd[p] + g_mat[p], 0.0)

    mean = [_head_sums(x, ones_bd) * (1.0 / hd) for x in y]
    yc = [y[p] - mean[p] for p in ps]
    var = [_head_sums(x * x, ones_bd) * (1.0 / hd) for x in yc]
    bonus = [_head_sums(r[p] * k[p] * rk_ref[:, sls[p]], ones_bd) * v[p] for p in ps]
    for p in ps:
        yn = yc[p] * lax.rsqrt(var[p] + RWKV_GN_EPS) * gain_ref[:, sls[p]] + bias_ref[:, sls[p]]
        o_ref[:, sls[p]] = ((yn + bonus[p]) * g_ref[:, sls[p]]).astype(o_ref.dtype)


def rwkv_scan(r, k, v, kap, b, lw, g, r_k, gn_gain, gn_bias):
    bsz, length, width = r.shape
    cw = _pick_tile(width, (2048, 1024, 512))
    pairs = cw // LANES
    seq = pl.BlockSpec((None, RWKV_CHUNK, cw), lambda bi, c, t: (bi, t, c))
    vec = pl.BlockSpec((1, cw), lambda bi, c, t: (0, c))
    return pl.pallas_call(
        functools.partial(_rwkv_scan_body, pairs=pairs),
        grid=(bsz, width // cw, length // RWKV_CHUNK),
        in_specs=[seq] * 7 + [vec] * 3,
        out_specs=seq,
        out_shape=jax.ShapeDtypeStruct((bsz, length, width), BF16),
        scratch_shapes=[pltpu.VMEM((pairs, LANES, LANES), F32)],
        compiler_params=_params(("parallel", "parallel", "arbitrary"), 48),
        name="rwkv_scan",
    )(r, k, v, kap, b, lw, g, r_k.reshape(1, width), gn_gain.reshape(1, width),
      gn_bias.reshape(1, width))


def _s5_param_body(lr_ref, li_ref, ldt_ref, btr_ref, bti_ref, cr_ref, ci_ref,
                   kst_ref, etr_ref, eti_ref, car_ref, cai_ref, a16r_ref, a16i_ref, *, groups):
    nc = SSM_CHUNK
    npow = 24
    tau = lax.broadcasted_iota(jnp.int32, (npow, SSM_STATE), 0).astype(F32)
    for gidx in range(groups):
        lr = lr_ref[gidx]
        li = li_ref[gidx]
        dt = jnp.exp(ldt_ref[gidx])
        mag = jnp.exp(tau * (lr * dt))
        ang = tau * (li * dt)
        pr = mag * jnp.cos(ang)
        pi = mag * jnp.sin(ang)
        ar = pr[1:2]
        ai = pi[1:2]
        den = lr * lr + li * li
        fr = ((ar - 1.0) * lr + ai * li) / den
        fi = (ai * lr - (ar - 1.0) * li) / den
        btr = btr_ref[gidx]
        bti = bti_ref[gidx]
        bbr = fr * btr - fi * bti
        bbi = fr * bti + fi * btr
        cr = cr_ref[gidx]
        ci = ci_ref[gidx]
        etr = jnp.concatenate([pr[s:s + 1] * bbr - pi[s:s + 1] * bbi for s in range(nc)], axis=0)
        eti = jnp.concatenate([pr[s:s + 1] * bbi + pi[s:s + 1] * bbr for s in range(nc)], axis=0)
        car = jnp.concatenate([pr[s:s + 1] * cr - pi[s:s + 1] * ci for s in range(1, nc + 1)], axis=0)
        cai = jnp.concatenate([pr[s:s + 1] * ci + pi[s:s + 1] * cr for s in range(1, nc + 1)], axis=0)
        kst_ref[gidx] = _dot_nt_acc(etr, cr) - _dot_nt_acc(eti, ci)
        etr_ref[gidx] = etr
        eti_ref[gidx] = eti
        car_ref[gidx] = car
        cai_ref[gidx] = cai
        a16r_ref[gidx] = jnp.broadcast_to(pr[nc:nc + 1], (8, SSM_STATE))
        a16i_ref[gidx] = jnp.broadcast_to(pi[nc:nc + 1], (8, SSM_STATE))


def s5_params(lam_re, lam_im, log_dt, b_re, b_im, c_re, c_im):
    ng, ns = lam_re.shape
    gc = SSM_GROUP
    gb = 8
    rows = SSM_CHUNK * gc

    def spec(*dims):
        return pl.BlockSpec((gb,) + dims, lambda i: (i,) + (0,) * len(dims))

    outs = [((rows, gc), F32)] + [((rows, ns), F32)] * 4 + [((8, ns), F32)] * 2
    return pl.pallas_call(
        functools.partial(_s5_param_body, groups=gb),
        grid=(ng // gb,),
        in_specs=[spec(1, ns), spec(1, ns), spec(1, 1), spec(gc, ns), spec(gc, ns), spec(gc, ns),
                  spec(gc, ns)],
        out_specs=[spec(*s) for s, _ in outs],
        out_shape=[jax.ShapeDtypeStruct((ng,) + s, dt) for s, dt in outs],
        compiler_params=_params(("parallel",), 32),
        name="s5_params",
    )(lam_re.reshape(ng, 1, ns), lam_im.reshape(ng, 1, ns), log_dt.reshape(ng, 1, 1),
      jnp.swapaxes(b_re, 1, 2), jnp.swapaxes(b_im, 1, 2), c_re, c_im)


def _s5_body(x_ref, toep_ref, wsr_ref, wsi_ref, wor_ref, woi_ref, a16r_ref, a16i_ref, d_ref,
             o_ref, sre_ref, sim_ref, xre_ref, xim_ref):
    rows = x_ref.shape[1]
    nchunk = rows // 8
    x0 = x_ref[0]
    x1 = x_ref[1]
    sre_ref[...] = _dot(x0, wsr_ref[0]) + _dot(x1, wsr_ref[1])
    sim_ref[...] = _dot(x0, wsi_ref[0]) + _dot(x1, wsi_ref[1])
    ar = a16r_ref[...]
    ai = a16i_ref[...]

    def step(c, carry):
        xr, xi = carry
        off = pl.multiple_of(c * 8, 8)
        xre_ref[pl.ds(off, 8), :] = xr
        xim_ref[pl.ds(off, 8), :] = xi
        sr = sre_ref[pl.ds(off, 8), :]
        si = sim_ref[pl.ds(off, 8), :]
        return ar * xr - ai * xi + sr, ar * xi + ai * xr + si

    zero = jnp.zeros((8, LANES), F32)
    lax.fori_loop(0, nchunk, step, (zero, zero))
    xr = xre_ref[...].astype(BF16)
    xi = xim_ref[...].astype(BF16)
    for q, xq in enumerate((x0, x1)):
        y = (_dot(xq, toep_ref[q]) + _dot(xr, wor_ref[q]) + _dot(xi, woi_ref[q])
             + xq.astype(F32) * d_ref[q])
        o_ref[q] = _gelu(y).astype(o_ref.dtype)


def s5_apply(xg, toep, wsr, wsi, wor, woi, a16r, a16i, dflat):
    ng, rows, cols = xg.shape

    def spec(*dims):
        return pl.BlockSpec((2,) + dims, lambda i: (i,) + (0,) * len(dims))

    return pl.pallas_call(
        _s5_body,
        grid=(ng // 2,),
        in_specs=[spec(rows, cols), spec(cols, cols), spec(cols, LANES), spec(cols, LANES),
                  spec(LANES, cols), spec(LANES, cols),
                  pl.BlockSpec((None, 8, LANES), lambda i: (i, 0, 0)),
                  pl.BlockSpec((None, 8, LANES), lambda i: (i, 0, 0)),
                  spec(1, cols)],
        out_specs=spec(rows, cols),
        out_shape=jax.ShapeDtypeStruct((ng, rows, cols), BF16),
        scratch_shapes=[pltpu.VMEM((rows, LANES), F32)] * 4,
        compiler_params=_params(("parallel",), 32),
        name="s5_apply",
    )(xg, toep, wsr, wsi, wor, woi, a16r, a16i, dflat)


def s5_mixer(u3, lam_re, lam_im, log_dt, b_re, b_im, c_re, c_im, d_skip):
    bsz, length, width = u3.shape
    ng, ns = lam_re.shape
    gc = SSM_GROUP
    nc = SSM_CHUNK
    kst, etr, eti, car, cai, a16r, a16i = s5_params(lam_re, lam_im, log_dt, b_re, b_im, c_re, c_im)

    k4 = kst.reshape(ng, nc, gc, gc)
    s_idx = jnp.arange(nc)[:, None]
    t_idx = jnp.arange(nc)[None, :]
    lag = t_idx - s_idx
    toep = jnp.where((lag >= 0)[None, :, :, None, None], k4[:, jnp.clip(lag, 0, nc - 1)], 0.0)
    toep = toep.transpose(0, 1, 3, 2, 4).reshape(ng, nc * gc, nc * gc).astype(BF16)
    odd = (jnp.arange(ng) % 2 == 1)[:, None, None]

    def lanes_by_parity(w):
        z = jnp.zeros_like(w)
        return jnp.where(odd, jnp.concatenate([z, w], -1), jnp.concatenate([w, z], -1))

    def flip_lag(e):
        return e.reshape(ng, nc, gc, ns)[:, ::-1].reshape(ng, nc * gc, ns)

    wsr = lanes_by_parity(flip_lag(etr)).astype(BF16)
    wsi = lanes_by_parity(flip_lag(eti)).astype(BF16)
    wor = jnp.swapaxes(lanes_by_parity(car), 1, 2).astype(BF16)
    woi = jnp.swapaxes(lanes_by_parity(-cai), 1, 2).astype(BF16)

    def pair_lanes(a):
        return a.reshape(ng // 2, 2, 8, ns).transpose(0, 2, 1, 3).reshape(ng // 2, 8, 2 * ns)

    dflat = jnp.tile(d_skip.reshape(ng, 1, gc), (1, 1, nc))

    nchunk = length // nc
    x = u3.reshape(bsz, nchunk, nc, ng, gc).transpose(3, 1, 0, 2, 4)
    x = jnp.pad(x, ((0, 0), (0, 0), (0, 8 - bsz), (0, 0), (0, 0))).reshape(ng, nchunk * 8, nc * gc)
    y = s5_apply(x, toep, wsr, wsi, wor, woi, pair_lanes(a16r), pair_lanes(a16i), dflat)
    y = y.reshape(ng, nchunk, 8, nc, gc)[:, :, :bsz].transpose(2, 1, 3, 0, 4)
    return y.reshape(bsz * length, width)


def _topk_rows(s, k):
    tops = []
    cur = s
    rank = jnp.full_like(s, float(PEER_KEYS - 1))
    for i in range(k):
        m = jnp.max(cur, axis=0, keepdims=True)
        tops.append(m)
        hit = cur >= m
        rank = jnp.where(hit, float(i), rank)
        cur = jnp.where(hit, NEG_INF, cur)
    return tops, rank


def _peer_route_body(q_ref, keys_ref, cnt_ref, e1_ref, r2_ref, e2_ref):
    nk = PEER_KEYS
    k = PEER_TOPK
    tm = q_ref.shape[0]
    wide, tall = 4, 3
    assert k == 16 and (wide + 1) * (tall + 1) > k and 2 * 9 > k
    a_idx = lax.broadcasted_iota(jnp.int32, (k, 1), 0)
    for c in range(tm // LANES):
        ts = slice(c * LANES, (c + 1) * LANES)
        for h in range(PEER_HEADS):
            q1 = q_ref[ts, (2 * h) * nk:(2 * h + 1) * nk]
            q2 = q_ref[ts, (2 * h + 1) * nk:(2 * h + 2) * nk]
            s1 = _dot_nt_acc(keys_ref[h, 0], q1)
            s2 = _dot_nt_acc(keys_ref[h, 1], q2)
            t1, _ = _topk_rows(s1, k)
            t2, rank2 = _topk_rows(s2, k)
            top1 = jnp.concatenate(t1, axis=0)
            top2 = jnp.concatenate(t2, axis=0)
            row_blocks = [t1[0] + top2] + [t1[a] + top2[0:8] for a in range(1, wide)]
            col_blocks = [jnp.where(a_idx >= wide, top1 + t2[b], NEG_INF) for b in range(tall)]
            cand = jnp.concatenate(row_blocks + col_blocks, axis=0)
            best = t1[0] + t2[0]
            zsum = jnp.zeros_like(best)
            cur = cand
            tau = best
            for _ in range(k):
                tau = jnp.max(cur, axis=0, keepdims=True)
                zsum = zsum + jnp.exp(tau - best)
                cur = jnp.where(cur >= tau, NEG_INF, cur)
            kept_rows = [jnp.sum(jnp.where(blk >= tau, 1.0, 0.0), axis=0, keepdims=True)
                         for blk in row_blocks]
            kept_cols = jnp.where(col_blocks[0] >= tau, 1.0, 0.0)
            for b in range(1, tall):
                kept_cols = kept_cols + jnp.where(col_blocks[b] >= tau, 1.0, 0.0)
            cnt = jnp.zeros_like(s1)
            for a in range(k):
                kept_a = kept_rows[a] if a < wide else kept_cols[a:a + 1]
                cnt = jnp.where(s1 == t1[a], kept_a, cnt)
            cnt_ref[h, :, ts] = cnt
            r2_ref[h, :, ts] = rank2.astype(BF16)
            e1_ref[h, :, ts] = jnp.exp(s1 - t1[0]) / zsum
            e2_ref[h, :, ts] = jnp.exp(s2 - t2[0]).astype(BF16)


def peer_route(q, keys):
    t = q.shape[0]
    tm = _pick_tile(t, (256, 128))
    big = pl.BlockSpec((PEER_HEADS, PEER_KEYS, tm), lambda i: (0, 0, i))
    sds = jax.ShapeDtypeStruct((PEER_HEADS, PEER_KEYS, t), F32)
    sds_bf = jax.ShapeDtypeStruct((PEER_HEADS, PEER_KEYS, t), BF16)
    return pl.pallas_call(
        _peer_route_body,
        grid=(t // tm,),
        in_specs=[pl.BlockSpec((tm, q.shape[1]), lambda i: (i, 0)),
                  pl.BlockSpec(keys.shape, lambda i: (0, 0, 0, 0))],
        out_specs=[big, big, big, big],
        out_shape=[sds, sds, sds_bf, sds_bf],
        compiler_params=_params(("parallel",), 48),
        name="peer_route",
    )(q, keys)


PEER_SLAB = 256


PEER_TILE = 512
SUBLANES = 8


def _peer_dense_body(hnt_ref, u_ref, v_ref, r2_ref, e2_ref, cnt_ref, e1_ref, o_ref, *, te):
    nk = PEER_KEYS
    rows_per_tile = te // nk
    rows_per_slab = PEER_SLAB // nk
    nslab = te // PEER_SLAB
    j = pl.program_id(1)
    base = (j % (SUBLANES // rows_per_tile)) * rows_per_tile

    @pl.when(j == 0)
    def _():
        o_ref[...] = jnp.zeros_like(o_ref)

    def gates(s, hold):
        out = []
        zero = jnp.zeros((), BF16)
        for r in range(rows_per_slab):
            row = base + s * rows_per_slab + r
            gate = None
            for h in range(PEER_HEADS):
                cnt_row = cnt_ref[h, pl.ds(row, 1), :]
                if hold is not None:
                    cnt_row = cnt_row + hold
                keep = r2_ref[h] < cnt_row.astype(BF16)
                term = jnp.where(keep, e2_ref[h] * e1_ref[h, pl.ds(row, 1), :].astype(BF16), zero)
                gate = term if gate is None else gate + term
            out.append(gate.astype(F32))
        return out

    hnt = hnt_ref[...]
    acts = [_dot(u_ref[s * PEER_SLAB:(s + 1) * PEER_SLAB, :], hnt) for s in range(nslab)]
    total = None
    hold = None
    for s in range(nslab):
        gate_blocks = gates(s, hold)
        parts = [gate_blocks[r] * _gelu(acts[s][r * nk:(r + 1) * nk, :])
                 for r in range(rows_per_slab)]
        w = jnp.concatenate(parts, axis=0).T.astype(BF16)
        c = _dot(w, v_ref[s * PEER_SLAB:(s + 1) * PEER_SLAB, :])
        hold = 0.0 * parts[-1][0:1, :]
        total = c if total is None else total + c
    o_ref[...] += total


def peer_dense(hn_t, u_all, v_all, layer, rank2, e2, cnt, e1):
    d, t = hn_t.shape
    ne = u_all.shape[1]
    tm = _pick_tile(t, (512, 256, 128))
    te = min(PEER_TILE, ne)
    rows_per_tile = te // PEER_KEYS
    tiles_per_block = SUBLANES // rows_per_tile
    big = pl.BlockSpec((PEER_HEADS, PEER_KEYS, tm), lambda i, j: (0, 0, i))
    rows = pl.BlockSpec((PEER_HEADS, SUBLANES, tm), lambda i, j: (0, j // tiles_per_block, i))
    table = pl.BlockSpec((None, te, d), lambda i, j: (layer, j, 0))
    return pl.pallas_call(
        functools.partial(_peer_dense_body, te=te),
        grid=(t // tm, ne // te),
        in_specs=[pl.BlockSpec((d, tm), lambda i, j: (0, i)), table, table, big, big, rows, rows],
        out_specs=pl.BlockSpec((tm, d), lambda i, j: (i, 0)),
        out_shape=jax.ShapeDtypeStruct((t, d), F32),
        compiler_params=_params(("parallel", "arbitrary"), 60),
        name="peer_dense",
    )(hn_t, u_all, v_all, rank2, e2, cnt, e1)


def _add_body(a_ref, b_ref, o_ref):
    o_ref[...] = a_ref[...] + b_ref[...]


def add(a, b):
    t, d = a.shape
    tm = _pick_tile(t, (512, 256, 128, 64, 32, 16, 8))
    spec = pl.BlockSpec((tm, d), lambda i: (i, 0))
    return pl.pallas_call(
        _add_body,
        grid=(t // tm,),
        in_specs=[spec, spec],
        out_specs=spec,
        out_shape=jax.ShapeDtypeStruct((t, d), F32),
        compiler_params=_params(("parallel",), 56),
        name="add",
    )(a, b)


def peer_ffn_residual(h, gain, w_q_bf, keys, u_all, v_all, layer):
    hn, hn_t = rms_norm_bf16(h, gain, with_transpose=True)
    q = matmul(hn, w_q_bf)
    cnt, e1, rank2, e2 = peer_route(q, keys)
    out = peer_dense(hn_t, u_all, v_all, layer, rank2, e2, cnt, e1)
    return add(h, out)


def _final_norm_body(x_ref, g_ref, o_ref):
    x = x_ref[...]
    inv = lax.rsqrt(jnp.mean(x * x, axis=-1, keepdims=True) + NORM_EPS)
    o_ref[...] = (x * inv) * g_ref[...]


def final_norm(h, gain):
    t, d = h.shape
    tm = _pick_tile(t, (256, 128, 64, 32, 16, 8))
    return pl.pallas_call(
        _final_norm_body,
        grid=(t // tm,),
        in_specs=[pl.BlockSpec((tm, d), lambda i: (i, 0)), pl.BlockSpec((1, d), lambda i: (0, 0))],
        out_specs=pl.BlockSpec((tm, d), lambda i: (i, 0)),
        out_shape=jax.ShapeDtypeStruct((t, d), F32),
        compiler_params=_params(("parallel",), 32),
        name="final_norm",
    )(h, gain.reshape(1, d))


def _even_layer_weights(w_in, shift_mu, w2, a2, width):
    pool_w = w_in.shape[1] - shift_mu.shape[0]
    c0 = pool_w + 3 * width
    c1 = c0 + RWKV_W_RANK
    c2 = c1 + RWKV_A_RANK
    d = w_in.shape[0]
    zw = jnp.zeros((d, LANES - RWKV_W_RANK), w_in.dtype)
    za = jnp.zeros((d, LANES - RWKV_A_RANK), w_in.dtype)
    w_cat = jnp.concatenate([w_in[:, :c0], w_in[:, c0:c1], zw, w_in[:, c1:c2], za, w_in[:, c2:]],
                            axis=1).astype(BF16)
    s0 = 3 * width
    s1 = s0 + RWKV_W_RANK
    s2 = s1 + RWKV_A_RANK
    mu = jnp.concatenate([
        jnp.zeros((pool_w,), F32), shift_mu[:s0], shift_mu[s0:s1],
        jnp.zeros((LANES - RWKV_W_RANK,), F32), shift_mu[s1:s2],
        jnp.zeros((LANES - RWKV_A_RANK,), F32), shift_mu[s2:]]).reshape(1, -1)
    w2p = jnp.pad(w2, ((0, LANES - RWKV_W_RANK), (0, 0)))
    a2p = jnp.pad(a2, ((0, LANES - RWKV_A_RANK), (0, 0)))
    return w_cat, mu, w2p, a2p, pool_w


def kernel(x, meta_tokens, mix_norm_gain, ffn_norm_gain, final_norm_gain, w_in_even, pool_w, pool_scale, shift_mu, rwkv_w0, rwkv_w2, rwkv_a0, rwkv_a2, rwkv_g2, rwkv_k_k, rwkv_k_a, rwkv_r_k, rwkv_gn_gain, rwkv_gn_bias, w_out_even, w_in_odd, ssm_lam_re, ssm_lam_im, ssm_log_dt, ssm_b_re, ssm_b_im, ssm_c_re, ssm_c_im, ssm_d, w_glu, peer_w_q, peer_sub_keys, peer_u, peer_v):
    bsz, seq, d = x.shape
    depth = mix_norm_gain.shape[0]
    real = N_META + seq
    length = -(-real // SEQ_ALIGN) * SEQ_ALIGN
    meta = jnp.broadcast_to(meta_tokens[None].astype(x.dtype), (bsz, N_META, d))
    h = jnp.concatenate([meta, x, jnp.zeros((bsz, length - real, d), x.dtype)], axis=1)
    h = h.reshape(bsz * length, d)
    peer_u_bf = peer_u.astype(BF16)
    peer_v_bf = peer_v.astype(BF16)

    for layer in range(depth):
        if layer % 2 == 0:
            i = layer // 2
            width = rwkv_w0.shape[-1]
            w_cat, mu, w2p, a2p, pool_cols = _even_layer_weights(
                w_in_even[i], shift_mu[i], rwkv_w2[i], rwkv_a2[i], width)
            z = matmul(rms_norm_bf16(h, mix_norm_gain[layer]), w_cat)
            z3 = z.reshape(bsz, length, -1)
            y_pool = pool_mixer(z3, pool_w[i], pool_scale[i])
            ob = pool_cols // 512
            wb = width // 512
            r, k, v, kap, b, lw, g = rwkv_prep(
                z3, (ob, ob + wb, ob + 2 * wb, ob + 3 * wb), mu, rwkv_w0[i], w2p, rwkv_a0[i], a2p,
                rwkv_g2[i], rwkv_k_k[i], rwkv_k_a[i])
            y_rwkv = rwkv_scan(r, k, v, kap, b, lw, g, rwkv_r_k[i], rwkv_gn_gain[i],
                               rwkv_gn_bias[i])
            h = proj2_residual(y_pool.reshape(bsz * length, -1), y_rwkv.reshape(bsz * length, -1),
                               w_out_even[i].astype(BF16), h)
        else:
            j = layer // 2
            u = matmul(rms_norm_bf16(h, mix_norm_gain[layer]), w_in_odd[j].astype(BF16),
                       out_dtype=BF16)
            y = s5_mixer(u.reshape(bsz, length, -1), ssm_lam_re[j], ssm_lam_im[j], ssm_log_dt[j],
                         ssm_b_re[j], ssm_b_im[j], ssm_c_re[j], ssm_c_im[j], ssm_d[j])
            h = glu_residual(y, w_glu[j].astype(BF16), h)
        h = peer_ffn_residual(h, ffn_norm_gain[layer], peer_w_q[layer].astype(BF16),
                              peer_sub_keys[layer], peer_u_bf, peer_v_bf, layer)
    out = final_norm(h, final_norm_gain).reshape(bsz, length, d)
    return out[:, N_META:real]
```

```python
import functools

import jax
import jax.numpy as jnp
from jax import lax
from jax.experimental import pallas as pl
from jax.experimental.pallas import tpu as pltpu

F32 = jnp.float32
BF16 = jnp.bfloat16

NORM_EPS = 1e-6
N_META = 16
POOL_WINDOWS = (2, 4, 8, 16)
RWKV_HEAD = 64
RWKV_W_RANK = 96
RWKV_A_RANK = 96
RWKV_G_RANK = 256
RWKV_GN_EPS = 64e-5
RWKV_CHUNK = 64
SSM_GROUP = 16
SSM_STATE = 64
SSM_CHUNK = 16
PEER_KEYS = 128
PEER_HEADS = 8
PEER_TOPK = 16
LANES = 128
SEQ_ALIGN = 128

NEG_INF = float("-inf")


def _params(sem, vmem_mb):
    return pltpu.CompilerParams(dimension_semantics=sem, vmem_limit_bytes=vmem_mb << 20)


def _split2(x):
    hi = x.astype(BF16)
    lo = (x - hi.astype(F32)).astype(BF16)
    return hi, lo


def _split3(x):
    hi = x.astype(BF16)
    r = x - hi.astype(F32)
    mid = r.astype(BF16)
    lo = (r - mid.astype(F32)).astype(BF16)
    return hi, mid, lo


def _dot(a, b):
    return jnp.dot(a, b, preferred_element_type=F32)


def _dot_nt(a, b):
    return lax.dot_general(a, b, (((1,), (1,)), ((), ())), preferred_element_type=F32)


def _dot_acc(a, b):
    ah, al = _split2(a)
    bh, bl = _split2(b)
    return _dot(ah, bh) + _dot(al, bh) + _dot(ah, bl)


def _dot_nt_acc(a, b):
    ah, al = _split2(a)
    bh, bl = _split2(b)
    return _dot_nt(ah, bh) + _dot_nt(al, bh) + _dot_nt(ah, bl)


def _gelu(x):
    return 0.5 * x * (1.0 + lax.erf(x * 0.7071067811865476))


def _sigmoid(x):
    return 1.0 / (1.0 + jnp.exp(-x))


def _pick_tile(n, prefs):
    for t in prefs:
        if n % t == 0:
            return t
    return n


def _rms_norm_body(x_ref, g_ref, o_ref, *rest):
    x = x_ref[...]
    inv = lax.rsqrt(jnp.mean(x * x, axis=-1, keepdims=True) + NORM_EPS)
    xn = (x * inv) * g_ref[...]
    o_ref[...] = xn.astype(BF16)
    if rest:
        rest[0][...] = xn.T.astype(BF16)


def rms_norm_bf16(x, gain, *, with_transpose=False):
    t, d = x.shape
    tm = _pick_tile(t, (256, 128, 64, 32, 16, 8))
    out_shape = [jax.ShapeDtypeStruct((t, d), BF16)]
    out_specs = [pl.BlockSpec((tm, d), lambda i: (i, 0))]
    if with_transpose:
        out_shape.append(jax.ShapeDtypeStruct((d, t), BF16))
        out_specs.append(pl.BlockSpec((d, tm), lambda i: (0, i)))
    res = pl.pallas_call(
        _rms_norm_body,
        grid=(t // tm,),
        in_specs=[pl.BlockSpec((tm, d), lambda i: (i, 0)), pl.BlockSpec((1, d), lambda i: (0, 0))],
        out_specs=out_specs,
        out_shape=out_shape,
        compiler_params=_params(("parallel",), 48),
        name="rms_norm",
    )(x, gain.reshape(1, d))
    return res if with_transpose else res[0]


def _row_tile(t):
    for cand in (1088, 1024, 768, 512, 384, 256, 128, 64, 32, 16):
        if t % cand == 0:
            return cand
    return t


def _matmul_body(a_ref, w_ref, o_ref):
    o_ref[...] = _dot(a_ref[...], w_ref[...]).astype(o_ref.dtype)


def matmul(a, w, *, out_dtype=F32):
    t, k = a.shape
    n = w.shape[1]
    tm = _row_tile(t)
    tn = _pick_tile(n, (512, 256, 128))
    return pl.pallas_call(
        _matmul_body,
        grid=(t // tm, n // tn),
        in_specs=[pl.BlockSpec((tm, k), lambda i, j: (i, 0)), pl.BlockSpec((k, tn), lambda i, j: (0, j))],
        out_specs=pl.BlockSpec((tm, tn), lambda i, j: (i, j)),
        out_shape=jax.ShapeDtypeStruct((t, n), out_dtype),
        compiler_params=_params(("parallel", "arbitrary"), 48),
        name="matmul",
    )(a, w)


def _proj2_body(a1_ref, a2_ref, w1_ref, w2_ref, h_ref, o_ref):
    o_ref[...] = h_ref[...] + _dot(a1_ref[...], w1_ref[...]) + _dot(a2_ref[...], w2_ref[...])


def proj2_residual(a1, a2, w, h):
    t, k1 = a1.shape
    k2 = a2.shape[1]
    n = w.shape[1]
    assert k1 == k2
    tm = _row_tile(t)
    tn = _pick_tile(n, (512, 256, 128))
    return pl.pallas_call(
        _proj2_body,
        grid=(t // tm, n // tn),
        in_specs=[
            pl.BlockSpec((tm, k1), lambda i, j: (i, 0)),
            pl.BlockSpec((tm, k2), lambda i, j: (i, 0)),
            pl.BlockSpec((k1, tn), lambda i, j: (0, j)),
            pl.BlockSpec((k2, tn), lambda i, j: (1, j)),
            pl.BlockSpec((tm, tn), lambda i, j: (i, j)),
        ],
        out_specs=pl.BlockSpec((tm, tn), lambda i, j: (i, j)),
        out_shape=jax.ShapeDtypeStruct((t, n), F32),
        compiler_params=_params(("parallel", "arbitrary"), 56),
        name="proj2_residual",
    )(a1, a2, w, w, h)


def _glu_body(y_ref, wa_ref, wb_ref, h_ref, o_ref):
    y = y_ref[...].astype(BF16)
    ga = _dot(y, wa_ref[...])
    gb = _dot(y, wb_ref[...])
    o_ref[...] = h_ref[...] + ga * _sigmoid(gb)


def glu_residual(y, w, h):
    t, k = y.shape
    n = w.shape[1] // 2
    tm = _row_tile(t)
    tn = _pick_tile(n, (512, 256, 128))
    nb = n // tn
    return pl.pallas_call(
        _glu_body,
        grid=(t // tm, nb),
        in_specs=[
            pl.BlockSpec((tm, k), lambda i, j: (i, 0)),
            pl.BlockSpec((k, tn), lambda i, j: (0, j)),
            pl.BlockSpec((k, tn), lambda i, j: (0, j + nb)),
            pl.BlockSpec((tm, tn), lambda i, j: (i, j)),
        ],
        out_specs=pl.BlockSpec((tm, tn), lambda i, j: (i, j)),
        out_shape=jax.ShapeDtypeStruct((t, n), F32),
        compiler_params=_params(("parallel", "arbitrary"), 56),
        name="glu_residual",
    )(y, w, w, h)


def _pool_body(z_ref, w_ref, s_ref, o_ref):
    grp = pl.program_id(1)
    z = z_ref[...]
    length = z.shape[0]
    row = lax.broadcasted_iota(jnp.int32, (length, 1), 0)
    posf = (row + 1).astype(F32)

    def shifted(x, k):
        return jnp.where(row >= k, pltpu.roll(x, k, axis=0), 0.0)

    for gi, win in enumerate(POOL_WINDOWS):

        @pl.when(grp == gi)
        def _(win=win):
            acc = z
            span = 1
            while span < win:
                acc = acc + shifted(acc, span)
                span *= 2
            cnt = jnp.minimum(posf, float(win))
            m = acc / cnt - z
            y = _dot(m.astype(BF16), w_ref[...])
            o_ref[...] = (y * s_ref[...]).astype(o_ref.dtype)


def pool_mixer(z3, pool_w, pool_scale):
    bsz, length, _ = z3.shape
    ng, c, _ = pool_w.shape
    return pl.pallas_call(
        _pool_body,
        grid=(bsz, ng),
        in_specs=[
            pl.BlockSpec((None, length, c), lambda b, g: (b, 0, g)),
            pl.BlockSpec((None, c, c), lambda b, g: (g, 0, 0)),
            pl.BlockSpec((1, c), lambda b, g: (0, g)),
        ],
        out_specs=pl.BlockSpec((None, length, c), lambda b, g: (b, 0, g)),
        out_shape=jax.ShapeDtypeStruct((bsz, length, ng * c), BF16),
        compiler_params=_params(("parallel", "arbitrary"), 48),
        name="pool_mixer",
    )(z3, pool_w.astype(BF16), pool_scale.reshape(1, ng * c))


def _head_sum_matrix(width, head):
    shift = head.bit_length() - 1
    r = lax.broadcasted_iota(jnp.int32, (width, width), 0) >> shift
    c = lax.broadcasted_iota(jnp.int32, (width, width), 1) >> shift
    return jnp.where(r == c, 1.0, 0.0).astype(BF16)


def _head_sums(x, ones_bd):
    hi, lo = _split2(x)
    return _dot(hi, ones_bd) + _dot(lo, ones_bd)


def _rwkv_prep_body(zr_ref, zk_ref, zv_ref, zl_ref, pr_ref, pk_ref, pv_ref, pl_ref,
                    mur_ref, muk_ref, muv_ref, mul_ref, w0_ref, w2_ref, a0_ref, a2_ref, g2_ref,
                    kk_ref, ka_ref,
                    r_out, k_out, v_out, kap_out, b_out, lw_out, g_out):
    first = pl.program_id(1) == 0
    tl = zr_ref.shape[0]
    row = lax.broadcasted_iota(jnp.int32, (tl, 1), 0)

    def mix(z_ref, p_ref, mu_ref):
        z = z_ref[...]
        prev_last = jnp.where(first, 0.0, p_ref[7:8, :])
        prev = jnp.where(row == 0, prev_last, pltpu.roll(z, 1, axis=0))
        return z + (prev - z) * mu_ref[...]

    r = mix(zr_ref, pr_ref, mur_ref)
    k = mix(zk_ref, pk_ref, muk_ref)
    v = mix(zv_ref, pv_ref, muv_ref)
    lora = mix(zl_ref, pl_ref, mul_ref)
    w_lr = lora[:, 0:LANES]
    a_lr = lora[:, LANES:2 * LANES]
    g_lr = lora[:, 2 * LANES:]

    x = w0_ref[...] + _dot_acc(jnp.tanh(w_lr), w2_ref[...])
    y = -x
    softplus = jnp.maximum(y, 0.0) + jnp.log(1.0 + jnp.exp(-jnp.abs(y)))
    lw = -jnp.exp(-softplus - 0.5)
    a = _sigmoid(a0_ref[...] + _dot_acc(a_lr, a2_ref[...]))
    g = _dot_acc(_sigmoid(g_lr), g2_ref[...])

    ones_bd = _head_sum_matrix(LANES, RWKV_HEAD)
    kk = k * kk_ref[...]
    kk2 = kk * kk
    width = k.shape[1]
    ss = jnp.concatenate(
        [_head_sums(kk2[:, c:c + LANES], ones_bd) for c in range(0, width, LANES)], axis=1)
    kap = kk / jnp.maximum(jnp.sqrt(ss), 1e-12)

    r_out[...] = r
    k_out[...] = k * (1.0 + (a - 1.0) * ka_ref[...])
    v_out[...] = v
    kap_out[...] = kap
    b_out[...] = kap * a
    lw_out[...] = lw
    g_out[...] = g


def rwkv_prep(z3, off_blocks, mu, w0, w2p, a0, a2p, g2, k_k, k_a):
    bsz, length, _ = z3.shape
    width = w0.shape[-1]
    cw = 512
    ncb = width // cw
    tl = _pick_tile(length, (352, 272, 192, 176, 136, 128, 64))
    ob_r, ob_k, ob_v, ob_l = off_blocks
    t8 = tl // 8

    def cur(ob, fixed=False):
        if fixed:
            return pl.BlockSpec((None, tl, cw), lambda b, i, c: (b, i, ob))
        return pl.BlockSpec((None, tl, cw), lambda b, i, c: (b, i, ob + c))

    def prev(ob, fixed=False):
        if fixed:
            return pl.BlockSpec((None, 8, cw), lambda b, i, c: (b, jnp.maximum(i * t8 - 1, 0), ob))
        return pl.BlockSpec((None, 8, cw), lambda b, i, c: (b, jnp.maximum(i * t8 - 1, 0), ob + c))

    def vec(ob, fixed=False):
        if fixed:
            return pl.BlockSpec((1, cw), lambda b, i, c: (0, ob))
        return pl.BlockSpec((1, cw), lambda b, i, c: (0, ob + c))

    colvec = pl.BlockSpec((1, cw), lambda b, i, c: (0, c))
    out_spec = pl.BlockSpec((None, tl, cw), lambda b, i, c: (b, i, c))
    out_sds = jax.ShapeDtypeStruct((bsz, length, width), F32)
    return pl.pallas_call(
        _rwkv_prep_body,
        grid=(bsz, length // tl, ncb),
        in_specs=[
            cur(ob_r), cur(ob_k), cur(ob_v), cur(ob_l, True),
            prev(ob_r), prev(ob_k), prev(ob_v), prev(ob_l, True),
            vec(ob_r), vec(ob_k), vec(ob_v), vec(ob_l, True),
            colvec,
            pl.BlockSpec((LANES, cw), lambda b, i, c: (0, c)),
            colvec,
            pl.BlockSpec((LANES, cw), lambda b, i, c: (0, c)),
            pl.BlockSpec((RWKV_G_RANK, cw), lambda b, i, c: (0, c)),
            colvec, colvec,
        ],
        out_specs=[out_spec] * 7,
        out_shape=[out_sds] * 7,
        compiler_params=_params(("parallel", "parallel", "arbitrary"), 56),
        name="rwkv_prep",
    )(z3, z3, z3, z3, z3, z3, z3, z3, mu, mu, mu, mu,
      w0.reshape(1, width), w2p, a0.reshape(1, width), a2p, g2,
      k_k.reshape(1, width), k_a.reshape(1, width))


def _rwkv_scan_body(r_ref, k_ref, v_ref, kap_ref, b_ref, lw_ref, g_ref, rk_ref, gain_ref, bias_ref,
                    o_ref, h_ref, *, pairs):
    cn = RWKV_CHUNK
    hd = RWKV_HEAD

    @pl.when(pl.program_id(2) == 0)
    def _():
        h_ref[...] = jnp.zeros_like(h_ref)

    lane = lax.broadcasted_iota(jnp.int32, (1, LANES), 1)
    m0 = lane < hd
    rowi = lax.broadcasted_iota(jnp.int32, (cn, LANES), 0)
    coli = lax.broadcasted_iota(jnp.int32, (cn, LANES), 1) & (hd - 1)
    strict = rowi > coli
    incl = rowi >= coli
    eye2 = jnp.where(rowi == coli, 1.0, 0.0)
    tri = jnp.where(lax.broadcasted_iota(jnp.int32, (cn, cn), 0)
                    >= lax.broadcasted_iota(jnp.int32, (cn, cn), 1), 1.0, 0.0).astype(BF16)
    br = lax.broadcasted_iota(jnp.int32, (LANES, LANES), 0) >> (hd.bit_length() - 1)
    bc = lax.broadcasted_iota(jnp.int32, (LANES, LANES), 1) >> (hd.bit_length() - 1)
    bd_mask = br == bc
    ones_bd = jnp.where(bd_mask, 1.0, 0.0).astype(BF16)

    def stk(x):
        return jnp.concatenate([jnp.where(m0, x, 0.0), jnp.where(m0, 0.0, x)], axis=0).astype(BF16)

    ps = range(pairs)
    sls = [slice(p * LANES, (p + 1) * LANES) for p in ps]
    r = [r_ref[:, s] for s in sls]
    k = [k_ref[:, s] for s in sls]
    v = [v_ref[:, s] for s in sls]
    kap = [kap_ref[:, s] for s in sls]
    b = [b_ref[:, s] for s in sls]
    lw = [lw_ref[:, s] for s in sls]
    hbd = [h_ref[p] for p in ps]

    lsp = [_split3(x) for x in lw]
    lp = [_dot(tri, a) + _dot(tri, m) + _dot(tri, c) for a, m, c in lsp]
    lpc = [x[cn - 1:cn, :] for x in lp]
    pinv = [jnp.exp(-x) for x in lp]
    pend = [jnp.exp(c - x) for c, x in zip(lpc, lp)]
    kt = [kap[p] * jnp.exp(lp[p] - lw[p]) for p in ps]
    rt = [r[p] * jnp.exp(lp[p]) for p in ps]
    kh = [k[p] * pinv[p] for p in ps]
    bh = [b[p] * pinv[p] for p in ps]
    khc = [k[p] * pend[p] for p in ps]
    bhc = [b[p] * pend[p] for p in ps]

    ktrt = [jnp.concatenate([kt[p], rt[p]], axis=0).astype(BF16) for p in ps]
    ab_all = [_dot_nt(ktrt[p], stk(bh[p])) for p in ps]
    ak_all = [_dot_nt(ktrt[p], stk(kh[p])) for p in ps]
    n_mat = [jnp.where(strict, x[:cn], 0.0) for x in ab_all]
    a_k = [jnp.where(strict, x[:cn], 0.0) for x in ak_all]
    a_rb = [jnp.where(incl, x[cn:], 0.0) for x in ab_all]
    a_rk = [jnp.where(incl, x[cn:], 0.0) for x in ak_all]

    t_mat = [eye2 - x for x in n_mat]
    pw = n_mat
    span = 1
    while 2 * span < cn:
        pw = [_dot(x.astype(BF16), stk(x)) for x in pw]
        t_mat = [t + _dot(t.astype(BF16), stk(x)) for t, x in zip(t_mat, pw)]
        span *= 2
    t_bf = [t.astype(BF16) for t in t_mat]

    kbar = [_dot(t_bf[p], stk(kt[p])) for p in ps]
    akv = [_dot(a_k[p].astype(BF16), stk(v[p])) for p in ps]
    ubar = [_dot(t_bf[p], stk(akv[p])) for p in ps]
    hb = [x.astype(BF16) for x in hbd]
    u = [_dot(kbar[p].astype(BF16), hb[p]) + ubar[p] for p in ps]
    y = [_dot(rt[p].astype(BF16), hb[p]) + _dot(a_rk[p].astype(BF16), stk(v[p]))
         - _dot(a_rb[p].astype(BF16), stk(u[p])) for p in ps]

    kb_t = [jnp.concatenate([khc[p], -bhc[p]], axis=0).T.astype(BF16) for p in ps]
    vu = [jnp.concatenate([v[p], u[p]], axis=0).astype(BF16) for p in ps]
    g_mat = [_dot(kb_t[p], vu[p]) for p in ps]
    pc_col = [jnp.exp(jnp.broadcast_to(x, (LANES, LANES)).T) for x in lpc]
    for p in ps:
        h_ref[p] = jnp.where(bd_mask, pc_col[p] * hbd[p] + g_mat[p], 0.0)

    mean = [_head_sums(x, ones_bd) * (1.0 / hd) for x in y]
    yc = [y[p] - mean[p] for p in ps]
    var = [_head_sums(x * x, ones_bd) * (1.0 / hd) for x in yc]
    bonus = [_head_sums(r[p] * k[p] * rk_ref[:, sls[p]], ones_bd) * v[p] for p in ps]
    for p in ps:
        yn = yc[p] * lax.rsqrt(var[p] + RWKV_GN_EPS) * gain_ref[:, sls[p]] + bias_ref[:, sls[p]]
        o_ref[:, sls[p]] = ((yn + bonus[p]) * g_ref[:, sls[p]]).astype(o_ref.dtype)


def rwkv_scan(r, k, v, kap, b, lw, g, r_k, gn_gain, gn_bias):
    bsz, length, width = r.shape
    cw = _pick_tile(width, (2048, 1024, 512))
    pairs = cw // LANES
    seq = pl.BlockSpec((None, RWKV_CHUNK, cw), lambda bi, c, t: (bi, t, c))
    vec = pl.BlockSpec((1, cw), lambda bi, c, t: (0, c))
    return pl.pallas_call(
        functools.partial(_rwkv_scan_body, pairs=pairs),
        grid=(bsz, width // cw, length // RWKV_CHUNK),
        in_specs=[seq] * 7 + [vec] * 3,
        out_specs=seq,
        out_shape=jax.ShapeDtypeStruct((bsz, length, width), BF16),
        scratch_shapes=[pltpu.VMEM((pairs, LANES, LANES), F32)],
        compiler_params=_params(("parallel", "parallel", "arbitrary"), 48),
        name="rwkv_scan",
    )(r, k, v, kap, b, lw, g, r_k.reshape(1, width), gn_gain.reshape(1, width),
      gn_bias.reshape(1, width))


def _s5_param_body(lr_ref, li_ref, ldt_ref, btr_ref, bti_ref, cr_ref, ci_ref,
                   kst_ref, etr_ref, eti_ref, car_ref, cai_ref, a16r_ref, a16i_ref, *, groups):
    nc = SSM_CHUNK
    npow = 24
    tau = lax.broadcasted_iota(jnp.int32, (npow, SSM_STATE), 0).astype(F32)
    for gidx in range(groups):
        lr = lr_ref[gidx]
        li = li_ref[gidx]
        dt = jnp.exp(ldt_ref[gidx])
        mag = jnp.exp(tau * (lr * dt))
        ang = tau * (li * dt)
        pr = mag * jnp.cos(ang)
        pi = mag * jnp.sin(ang)
        ar = pr[1:2]
        ai = pi[1:2]
        den = lr * lr + li * li
        fr = ((ar - 1.0) * lr + ai * li) / den
        fi = (ai * lr - (ar - 1.0) * li) / den
        btr = btr_ref[gidx]
        bti = bti_ref[gidx]
        bbr = fr * btr - fi * bti
        bbi = fr * bti + fi * btr
        cr = cr_ref[gidx]
        ci = ci_ref[gidx]
        etr = jnp.concatenate([pr[s:s + 1] * bbr - pi[s:s + 1] * bbi for s in range(nc)], axis=0)
        eti = jnp.concatenate([pr[s:s + 1] * bbi + pi[s:s + 1] * bbr for s in range(nc)], axis=0)
        car = jnp.concatenate([pr[s:s + 1] * cr - pi[s:s + 1] * ci for s in range(1, nc + 1)], axis=0)
        cai = jnp.concatenate([pr[s:s + 1] * ci + pi[s:s + 1] * cr for s in range(1, nc + 1)], axis=0)
        kst_ref[gidx] = _dot_nt_acc(etr, cr) - _dot_nt_acc(eti, ci)
        etr_ref[gidx] = etr
        eti_ref[gidx] = eti
        car_ref[gidx] = car
        cai_ref[gidx] = cai
        a16r_ref[gidx] = jnp.broadcast_to(pr[nc:nc + 1], (8, SSM_STATE))
        a16i_ref[gidx] = jnp.broadcast_to(pi[nc:nc + 1], (8, SSM_STATE))


def s5_params(lam_re, lam_im, log_dt, b_re, b_im, c_re, c_im):
    ng, ns = lam_re.shape
    gc = SSM_GROUP
    gb = 8
    rows = SSM_CHUNK * gc

    def spec(*dims):
        return pl.BlockSpec((gb,) + dims, lambda i: (i,) + (0,) * len(dims))

    outs = [((rows, gc), F32)] + [((rows, ns), F32)] * 4 + [((8, ns), F32)] * 2
    return pl.pallas_call(
        functools.partial(_s5_param_body, groups=gb),
        grid=(ng // gb,),
        in_specs=[spec(1, ns), spec(1, ns), spec(1, 1), spec(gc, ns), spec(gc, ns), spec(gc, ns),
                  spec(gc, ns)],
        out_specs=[spec(*s) for s, _ in outs],
        out_shape=[jax.ShapeDtypeStruct((ng,) + s, dt) for s, dt in outs],
        compiler_params=_params(("parallel",), 32),
        name="s5_params",
    )(lam_re.reshape(ng, 1, ns), lam_im.reshape(ng, 1, ns), log_dt.reshape(ng, 1, 1),
      jnp.swapaxes(b_re, 1, 2), jnp.swapaxes(b_im, 1, 2), c_re, c_im)


S5_BLOCK_GROUPS = 16


def _s5_state_body(u_ref, wb_ref, ar_ref, ai_ref, xpr_ref, xpi_ref, sr_ref, si_ref, *, nch, bsz):
    t = pl.program_id(1)
    half = sr_ref.shape[1]
    part = _dot(u_ref[...].astype(BF16), wb_ref[...])

    @pl.when(t == 0)
    def _():
        sr_ref[...] = part[:, :half]
        si_ref[...] = part[:, half:]

    @pl.when(t > 0)
    def _():
        sr_ref[...] += part[:, :half]
        si_ref[...] += part[:, half:]

    @pl.when(t == SSM_CHUNK - 1)
    def _():
        ar = ar_ref[...]
        ai = ai_ref[...]

        def step(c, xs):
            new = []
            for b in range(bsz):
                xr, xi = xs[2 * b], xs[2 * b + 1]
                row = b * nch + c
                xpr_ref[pl.ds(row, 1), :] = xr
                xpi_ref[pl.ds(row, 1), :] = xi
                new.append(ar * xr - ai * xi + sr_ref[pl.ds(row, 1), :])
                new.append(ar * xi + ai * xr + si_ref[pl.ds(row, 1), :])
            return tuple(new)

        zero = jnp.zeros((1, half), F32)
        lax.fori_loop(0, nch, step, (zero,) * (2 * bsz))


def _s5_out_body(u_ref, xpr_ref, xpi_ref, wc_ref, bd_ref, d_ref, o_ref, acc_ref):
    tp = pl.program_id(1)
    half = xpr_ref.shape[1]
    acc_ref[...] = (_dot(xpr_ref[...].astype(BF16), wc_ref[:half, :])
                    + _dot(xpi_ref[...].astype(BF16), wc_ref[half:, :]))

    def lag_term(t, carry):
        acc_ref[...] += _dot(u_ref[t].astype(BF16), bd_ref[tp - t])
        return carry

    lax.fori_loop(0, tp + 1, lag_term, 0)
    o_ref[...] = _gelu(acc_ref[...] + u_ref[tp] * d_ref[...])


def s5_mixer(u3, lam_re, lam_im, log_dt, b_re, b_im, c_re, c_im, d_skip):
    bsz, length, width = u3.shape
    ng, ns = lam_re.shape
    gc = SSM_GROUP
    nc = SSM_CHUNK
    gb = S5_BLOCK_GROUPS
    nb = ng // gb
    cw = gb * gc
    half = gb * ns
    nch = length // nc
    kst, etr, eti, car, cai, a16r, a16i = s5_params(lam_re, lam_im, log_dt, b_re, b_im, c_re, c_im)

    eye = jnp.eye(gb, dtype=F32)
    k5 = kst.reshape(nb, gb, nc, gc, gc)
    bd = jnp.einsum("bgtio,gh->btgiho", k5, eye).reshape(nb, nc, cw, cw).astype(BF16)

    def state_in(e):
        e5 = e.reshape(nb, gb, nc, gc, ns)[:, :, ::-1]
        return jnp.einsum("bgtip,gh->btgihp", e5, eye).reshape(nb, nc, cw, half)

    def state_out(c):
        c5 = c.reshape(nb, gb, nc, gc, ns)
        return jnp.einsum("bgtop,gh->btgpho", c5, eye).reshape(nb, nc, half, cw)

    wb = jnp.concatenate([state_in(etr), state_in(eti)], axis=-1).astype(BF16)
    wc = jnp.concatenate([state_out(car), state_out(-cai)], axis=2).astype(BF16)
    a_r = a16r[:, 0].reshape(nb, 1, half)
    a_i = a16i[:, 0].reshape(nb, 1, half)
    d3 = d_skip.reshape(nb, 1, cw)

    rows = bsz * nch
    u_pos = u3.reshape(bsz, nch, nc, width).transpose(2, 0, 1, 3).reshape(nc, rows, width)
    row_spec = pl.BlockSpec((None, rows, cw), lambda c, t: (t, 0, c))
    vec = pl.BlockSpec((None, 1, half), lambda c, t: (c, 0, 0))
    xp_spec = pl.BlockSpec((rows, half), lambda c, t: (0, c))
    xp_sds = jax.ShapeDtypeStruct((rows, nb * half), F32)
    xpr, xpi = pl.pallas_call(
        functools.partial(_s5_state_body, nch=nch, bsz=bsz),
        grid=(nb, nc),
        in_specs=[row_spec, pl.BlockSpec((None, None, cw, 2 * half), lambda c, t: (c, t, 0, 0)),
                  vec, vec],
        out_specs=[xp_spec, xp_spec],
        out_shape=[xp_sds, xp_sds],
        scratch_shapes=[pltpu.VMEM((rows, half), F32)] * 2,
        compiler_params=_params(("parallel", "arbitrary"), 56),
        name="s5_state",
    )(u_pos, wb, a_r, a_i)
    y_pos = pl.pallas_call(
        _s5_out_body,
        grid=(nb, nc),
        in_specs=[pl.BlockSpec((nc, rows, cw), lambda c, t: (0, 0, c)), xp_spec, xp_spec,
                  pl.BlockSpec((None, None, 2 * half, cw), lambda c, t: (c, t, 0, 0)),
                  pl.BlockSpec((None, nc, cw, cw), lambda c, t: (c, 0, 0, 0)),
                  pl.BlockSpec((None, 1, cw), lambda c, t: (c, 0, 0))],
        out_specs=row_spec,
        out_shape=jax.ShapeDtypeStruct((nc, rows, width), F32),
        scratch_shapes=[pltpu.VMEM((rows, cw), F32)],
        compiler_params=_params(("parallel", "arbitrary"), 56),
        name="s5_output",
    )(u_pos, xpr, xpi, wc, bd, d3)
    y3 = y_pos.reshape(nc, bsz, nch, width).transpose(1, 2, 0, 3)
    return y3.reshape(bsz * length, width)


def _topk_rows(s, k):
    tops = []
    cur = s
    rank = jnp.full_like(s, float(PEER_KEYS - 1))
    for i in range(k):
        m = jnp.max(cur, axis=0, keepdims=True)
        tops.append(m)
        hit = cur >= m
        rank = jnp.where(hit, float(i), rank)
        cur = jnp.where(hit, NEG_INF, cur)
    return tops, rank


def _peer_route_body(q_ref, keys_ref, cnt_ref, e1_ref, r2_ref, e2_ref):
    nk = PEER_KEYS
    k = PEER_TOPK
    tm = q_ref.shape[0]
    wide, tall = 4, 3
    assert k == 16 and (wide + 1) * (tall + 1) > k and 2 * 9 > k
    a_idx = lax.broadcasted_iota(jnp.int32, (k, 1), 0)
    for c in range(tm // LANES):
        ts = slice(c * LANES, (c + 1) * LANES)
        for h in range(PEER_HEADS):
            q1 = q_ref[ts, (2 * h) * nk:(2 * h + 1) * nk]
            q2 = q_ref[ts, (2 * h + 1) * nk:(2 * h + 2) * nk]
            s1 = _dot_nt_acc(keys_ref[h, 0], q1)
            s2 = _dot_nt_acc(keys_ref[h, 1], q2)
            t1, _ = _topk_rows(s1, k)
            t2, rank2 = _topk_rows(s2, k)
            top1 = jnp.concatenate(t1, axis=0)
            top2 = jnp.concatenate(t2, axis=0)
            row_blocks = [t1[0] + top2] + [t1[a] + top2[0:8] for a in range(1, wide)]
            col_blocks = [jnp.where(a_idx >= wide, top1 + t2[b], NEG_INF) for b in range(tall)]
            cand = jnp.concatenate(row_blocks + col_blocks, axis=0)
            best = t1[0] + t2[0]
            zsum = jnp.zeros_like(best)
            cur = cand
            tau = best
            for _ in range(k):
                tau = jnp.max(cur, axis=0, keepdims=True)
                zsum = zsum + jnp.exp(tau - best)
                cur = jnp.where(cur >= tau, NEG_INF, cur)
            kept_rows = [jnp.sum(jnp.where(blk >= tau, 1.0, 0.0), axis=0, keepdims=True)
                         for blk in row_blocks]
            kept_cols = jnp.where(col_blocks[0] >= tau, 1.0, 0.0)
            for b in range(1, tall):
                kept_cols = kept_cols + jnp.where(col_blocks[b] >= tau, 1.0, 0.0)
            cnt = jnp.zeros_like(s1)
            for a in range(k):
                kept_a = kept_rows[a] if a < wide else kept_cols[a:a + 1]
                cnt = jnp.where(s1 == t1[a], kept_a, cnt)
            cnt_ref[h, :, ts] = cnt
            r2_ref[h, :, ts] = rank2.astype(BF16)
            e1_ref[h, :, ts] = jnp.exp(s1 - t1[0]) / zsum
            e2_ref[h, :, ts] = jnp.exp(s2 - t2[0]).astype(BF16)


def peer_route(q, keys):
    t = q.shape[0]
    tm = _pick_tile(t, (256, 128))
    big = pl.BlockSpec((PEER_HEADS, PEER_KEYS, tm), lambda i: (0, 0, i))
    sds = jax.ShapeDtypeStruct((PEER_HEADS, PEER_KEYS, t), F32)
    sds_bf = jax.ShapeDtypeStruct((PEER_HEADS, PEER_KEYS, t), BF16)
    return pl.pallas_call(
        _peer_route_body,
        grid=(t // tm,),
        in_specs=[pl.BlockSpec((tm, q.shape[1]), lambda i: (i, 0)),
                  pl.BlockSpec(keys.shape, lambda i: (0, 0, 0, 0))],
        out_specs=[big, big, big, big],
        out_shape=[sds, sds, sds_bf, sds_bf],
        compiler_params=_params(("parallel",), 48),
        name="peer_route",
    )(q, keys)


PEER_SLAB = 256


PEER_TILE = 1024
SUBLANES = 8


def _peer_dense_body(hnt_ref, u_ref, v_ref, r2_ref, e2_ref, cnt_ref, e1_ref, o_ref, *, te):
    nk = PEER_KEYS
    rows_per_tile = te // nk
    rows_per_slab = PEER_SLAB // nk
    nslab = te // PEER_SLAB
    j = pl.program_id(1)
    base = (j % (SUBLANES // rows_per_tile)) * rows_per_tile

    @pl.when(j == 0)
    def _():
        o_ref[...] = jnp.zeros_like(o_ref)

    def gates(s, hold):
        out = []
        zero = jnp.zeros((), BF16)
        for r in range(rows_per_slab):
            row = base + s * rows_per_slab + r
            gate = None
            for h in range(PEER_HEADS):
                cnt_row = cnt_ref[h, pl.ds(row, 1), :]
                if hold is not None:
                    cnt_row = cnt_row + hold
                keep = r2_ref[h] < cnt_row.astype(BF16)
                term = jnp.where(keep, e2_ref[h] * e1_ref[h, pl.ds(row, 1), :].astype(BF16), zero)
                gate = term if gate is None else gate + term
            out.append(gate.astype(F32))
        return out

    hnt = hnt_ref[...]
    acts = [_dot(u_ref[s * PEER_SLAB:(s + 1) * PEER_SLAB, :], hnt) for s in range(nslab)]
    total = None
    hold = None
    for s in range(nslab):
        gate_blocks = gates(s, hold)
        parts = [gate_blocks[r] * _gelu(acts[s][r * nk:(r + 1) * nk, :])
                 for r in range(rows_per_slab)]
        w = jnp.concatenate(parts, axis=0).T.astype(BF16)
        c = _dot(w, v_ref[s * PEER_SLAB:(s + 1) * PEER_SLAB, :])
        hold = 0.0 * parts[-1][0:1, :]
        total = c if total is None else total + c
    o_ref[...] += total


def peer_dense(hn_t, u_all, v_all, layer, rank2, e2, cnt, e1):
    d, t = hn_t.shape
    ne = u_all.shape[1]
    tm = _pick_tile(t, (512, 256, 128))
    te = min(PEER_TILE, ne)
    rows_per_tile = te // PEER_KEYS
    tiles_per_block = SUBLANES // rows_per_tile
    once = pl.Buffered(1)
    big = pl.BlockSpec((PEER_HEADS, PEER_KEYS, tm), lambda i, j: (0, 0, i), pipeline_mode=once)
    rows = pl.BlockSpec((PEER_HEADS, SUBLANES, tm), lambda i, j: (0, j // tiles_per_block, i))
    table = pl.BlockSpec((None, te, d), lambda i, j: (layer, j, 0))
    return pl.pallas_call(
        functools.partial(_peer_dense_body, te=te),
        grid=(t // tm, ne // te),
        in_specs=[pl.BlockSpec((d, tm), lambda i, j: (0, i), pipeline_mode=once), table, table,
                  big, big, rows, rows],
        out_specs=pl.BlockSpec((tm, d), lambda i, j: (i, 0), pipeline_mode=once),
        out_shape=jax.ShapeDtypeStruct((t, d), F32),
        compiler_params=_params(("parallel", "arbitrary"), 60),
        name="peer_dense",
    )(hn_t, u_all, v_all, rank2, e2, cnt, e1)


def _add_body(a_ref, b_ref, o_ref):
    o_ref[...] = a_ref[...] + b_ref[...]


def add(a, b):
    t, d = a.shape
    tm = _pick_tile(t, (512, 256, 128, 64, 32, 16, 8))
    spec = pl.BlockSpec((tm, d), lambda i: (i, 0))
    return pl.pallas_call(
        _add_body,
        grid=(t // tm,),
        in_specs=[spec, spec],
        out_specs=spec,
        out_shape=jax.ShapeDtypeStruct((t, d), F32),
        compiler_params=_params(("parallel",), 56),
        name="add",
    )(a, b)


def peer_ffn_residual(h, gain, w_q_bf, keys, u_all, v_all, layer):
    hn, hn_t = rms_norm_bf16(h, gain, with_transpose=True)
    q = matmul(hn, w_q_bf)
    cnt, e1, rank2, e2 = peer_route(q, keys)
    out = peer_dense(hn_t, u_all, v_all, layer, rank2, e2, cnt, e1)
    return add(h, out)


def _final_norm_body(x_ref, g_ref, o_ref):
    x = x_ref[...]
    inv = lax.rsqrt(jnp.mean(x * x, axis=-1, keepdims=True) + NORM_EPS)
    o_ref[...] = (x * inv) * g_ref[...]


def final_norm(h, gain):
    t, d = h.shape
    tm = _pick_tile(t, (256, 128, 64, 32, 16, 8))
    return pl.pallas_call(
        _final_norm_body,
        grid=(t // tm,),
        in_specs=[pl.BlockSpec((tm, d), lambda i: (i, 0)), pl.BlockSpec((1, d), lambda i: (0, 0))],
        out_specs=pl.BlockSpec((tm, d), lambda i: (i, 0)),
        out_shape=jax.ShapeDtypeStruct((t, d), F32),
        compiler_params=_params(("parallel",), 32),
        name="final_norm",
    )(h, gain.reshape(1, d))


def _even_layer_weights(w_in, shift_mu, w2, a2, width):
    pool_w = w_in.shape[1] - shift_mu.shape[0]
    c0 = pool_w + 3 * width
    c1 = c0 + RWKV_W_RANK
    c2 = c1 + RWKV_A_RANK
    d = w_in.shape[0]
    zw = jnp.zeros((d, LANES - RWKV_W_RANK), w_in.dtype)
    za = jnp.zeros((d, LANES - RWKV_A_RANK), w_in.dtype)
    w_cat = jnp.concatenate([w_in[:, :c0], w_in[:, c0:c1], zw, w_in[:, c1:c2], za, w_in[:, c2:]],
                            axis=1).astype(BF16)
    s0 = 3 * width
    s1 = s0 + RWKV_W_RANK
    s2 = s1 + RWKV_A_RANK
    mu = jnp.concatenate([
        jnp.zeros((pool_w,), F32), shift_mu[:s0], shift_mu[s0:s1],
        jnp.zeros((LANES - RWKV_W_RANK,), F32), shift_mu[s1:s2],
        jnp.zeros((LANES - RWKV_A_RANK,), F32), shift_mu[s2:]]).reshape(1, -1)
    w2p = jnp.pad(w2, ((0, LANES - RWKV_W_RANK), (0, 0)))
    a2p = jnp.pad(a2, ((0, LANES - RWKV_A_RANK), (0, 0)))
    return w_cat, mu, w2p, a2p, pool_w


def kernel(x, meta_tokens, mix_norm_gain, ffn_norm_gain, final_norm_gain, w_in_even, pool_w, pool_scale, shift_mu, rwkv_w0, rwkv_w2, rwkv_a0, rwkv_a2, rwkv_g2, rwkv_k_k, rwkv_k_a, rwkv_r_k, rwkv_gn_gain, rwkv_gn_bias, w_out_even, w_in_odd, ssm_lam_re, ssm_lam_im, ssm_log_dt, ssm_b_re, ssm_b_im, ssm_c_re, ssm_c_im, ssm_d, w_glu, peer_w_q, peer_sub_keys, peer_u, peer_v):
    bsz, seq, d = x.shape
    depth = mix_norm_gain.shape[0]
    real = N_META + seq
    length = -(-real // SEQ_ALIGN) * SEQ_ALIGN
    meta = jnp.broadcast_to(meta_tokens[None].astype(x.dtype), (bsz, N_META, d))
    h = jnp.concatenate([meta, x, jnp.zeros((bsz, length - real, d), x.dtype)], axis=1)
    h = h.reshape(bsz * length, d)
    peer_u_bf = peer_u.astype(BF16)
    peer_v_bf = peer_v.astype(BF16)

    for layer in range(depth):
        if layer % 2 == 0:
            i = layer // 2
            width = rwkv_w0.shape[-1]
            w_cat, mu, w2p, a2p, pool_cols = _even_layer_weights(
                w_in_even[i], shift_mu[i], rwkv_w2[i], rwkv_a2[i], width)
            z = matmul(rms_norm_bf16(h, mix_norm_gain[layer]), w_cat)
            z3 = z.reshape(bsz, length, -1)
            y_pool = pool_mixer(z3, pool_w[i], pool_scale[i])
            ob = pool_cols // 512
            wb = width // 512
            r, k, v, kap, b, lw, g = rwkv_prep(
                z3, (ob, ob + wb, ob + 2 * wb, ob + 3 * wb), mu, rwkv_w0[i], w2p, rwkv_a0[i], a2p,
                rwkv_g2[i], rwkv_k_k[i], rwkv_k_a[i])
            y_rwkv = rwkv_scan(r, k, v, kap, b, lw, g, rwkv_r_k[i], rwkv_gn_gain[i],
                               rwkv_gn_bias[i])
            h = proj2_residual(y_pool.reshape(bsz * length, -1), y_rwkv.reshape(bsz * length, -1),
                               w_out_even[i].astype(BF16), h)
        else:
            j = layer // 2
            u = matmul(rms_norm_bf16(h, mix_norm_gain[layer]), w_in_odd[j].astype(BF16))
            y = s5_mixer(u.reshape(bsz, length, -1), ssm_lam_re[j], ssm_lam_im[j], ssm_log_dt[j],
                         ssm_b_re[j], ssm_b_im[j], ssm_c_re[j], ssm_c_im[j], ssm_d[j])
            h = glu_residual(y, w_glu[j].astype(BF16), h)
        h = peer_ffn_residual(h, ffn_norm_gain[layer], peer_w_q[layer].astype(BF16),
                              peer_sub_keys[layer], peer_u_bf, peer_v_bf, layer)
    out = final_norm(h, final_norm_gain).reshape(bsz, length, d)
    return out[:, N_META:real]
```

```python
import functools

import jax
import jax.numpy as jnp
from jax import lax
from jax.experimental import pallas as pl
from jax.experimental.pallas import tpu as pltpu

F32 = jnp.float32
BF16 = jnp.bfloat16

NORM_EPS = 1e-6
N_META = 16
POOL_WINDOWS = (2, 4, 8, 16)
RWKV_HEAD = 64
RWKV_W_RANK = 96
RWKV_A_RANK = 96
RWKV_G_RANK = 256
RWKV_GN_EPS = 64e-5
RWKV_CHUNK = 64
SSM_GROUP = 16
SSM_STATE = 64
SSM_CHUNK = 16
PEER_KEYS = 128
PEER_HEADS = 8
PEER_TOPK = 16
LANES = 128
SEQ_ALIGN = 128

NEG_INF = float("-inf")


def _params(sem, vmem_mb):
    return pltpu.CompilerParams(dimension_semantics=sem, vmem_limit_bytes=vmem_mb << 20)


def _split2(x):
    hi = x.astype(BF16)
    lo = (x - hi.astype(F32)).astype(BF16)
    return hi, lo


def _split3(x):
    hi = x.astype(BF16)
    r = x - hi.astype(F32)
    mid = r.astype(BF16)
    lo = (r - mid.astype(F32)).astype(BF16)
    return hi, mid, lo


def _dot(a, b):
    return jnp.dot(a, b, preferred_element_type=F32)


def _dot_nt(a, b):
    return lax.dot_general(a, b, (((1,), (1,)), ((), ())), preferred_element_type=F32)


def _dot_acc(a, b):
    ah, al = _split2(a)
    bh, bl = _split2(b)
    return _dot(ah, bh) + _dot(al, bh) + _dot(ah, bl)


def _dot_nt_acc(a, b):
    ah, al = _split2(a)
    bh, bl = _split2(b)
    return _dot_nt(ah, bh) + _dot_nt(al, bh) + _dot_nt(ah, bl)


def _gelu(x):
    return 0.5 * x * (1.0 + lax.erf(x * 0.7071067811865476))


def _sigmoid(x):
    return 1.0 / (1.0 + jnp.exp(-x))


def _pick_tile(n, prefs):
    for t in prefs:
        if n % t == 0:
            return t
    return n


def _rms_norm_body(*refs, with_transpose, with_delta):
    refs = list(refs)
    x = refs.pop(0)[...]
    if with_delta:
        x = x + refs.pop(0)[...]
    g_ref, o_ref = refs.pop(0), refs.pop(0)
    inv = lax.rsqrt(jnp.mean(x * x, axis=-1, keepdims=True) + NORM_EPS)
    xn = (x * inv) * g_ref[...]
    o_ref[...] = xn.astype(BF16)
    if with_transpose:
        refs.pop(0)[...] = xn.T.astype(BF16)
    if with_delta:
        refs.pop(0)[...] = x


def rms_norm_bf16(x, gain, *, delta=None, with_transpose=False):
    t, d = x.shape
    tm = _pick_tile(t, (256, 128, 64, 32, 16, 8))
    row = pl.BlockSpec((tm, d), lambda i: (i, 0))
    args = [x] + ([delta] if delta is not None else []) + [gain.reshape(1, d)]
    in_specs = [row] * (len(args) - 1) + [pl.BlockSpec((1, d), lambda i: (0, 0))]
    out_shape = [jax.ShapeDtypeStruct((t, d), BF16)]
    out_specs = [row]
    if with_transpose:
        out_shape.append(jax.ShapeDtypeStruct((d, t), BF16))
        out_specs.append(pl.BlockSpec((d, tm), lambda i: (0, i)))
    if delta is not None:
        out_shape.append(jax.ShapeDtypeStruct((t, d), F32))
        out_specs.append(row)
    res = pl.pallas_call(
        functools.partial(_rms_norm_body, with_transpose=with_transpose,
                          with_delta=delta is not None),
        grid=(t // tm,),
        in_specs=in_specs,
        out_specs=out_specs,
        out_shape=out_shape,
        compiler_params=_params(("parallel",), 48),
        name="rms_norm",
    )(*args)
    return res if len(res) > 1 else res[0]


def _row_tile(t):
    for cand in (1088, 1024, 768, 512, 384, 256, 128, 64, 32, 16):
        if t % cand == 0:
            return cand
    return t


def _matmul_body(a_ref, w_ref, o_ref):
    o_ref[...] = _dot(a_ref[...], w_ref[...]).astype(o_ref.dtype)


def matmul(a, w, *, out_dtype=F32):
    t, k = a.shape
    n = w.shape[1]
    tm = _row_tile(t)
    tn = _pick_tile(n, (512, 256, 128))
    return pl.pallas_call(
        _matmul_body,
        grid=(t // tm, n // tn),
        in_specs=[pl.BlockSpec((tm, k), lambda i, j: (i, 0)), pl.BlockSpec((k, tn), lambda i, j: (0, j))],
        out_specs=pl.BlockSpec((tm, tn), lambda i, j: (i, j)),
        out_shape=jax.ShapeDtypeStruct((t, n), out_dtype),
        compiler_params=_params(("parallel", "arbitrary"), 48),
        name="matmul",
    )(a, w)


def _proj2_body(a1_ref, a2_ref, w1_ref, w2_ref, h_ref, o_ref):
    o_ref[...] = h_ref[...] + _dot(a1_ref[...], w1_ref[...]) + _dot(a2_ref[...], w2_ref[...])


def proj2_residual(a1, a2, w, h):
    t, k1 = a1.shape
    k2 = a2.shape[1]
    n = w.shape[1]
    assert k1 == k2
    tm = _row_tile(t)
    tn = _pick_tile(n, (512, 256, 128))
    return pl.pallas_call(
        _proj2_body,
        grid=(t // tm, n // tn),
        in_specs=[
            pl.BlockSpec((tm, k1), lambda i, j: (i, 0)),
            pl.BlockSpec((tm, k2), lambda i, j: (i, 0)),
            pl.BlockSpec((k1, tn), lambda i, j: (0, j)),
            pl.BlockSpec((k2, tn), lambda i, j: (1, j)),
            pl.BlockSpec((tm, tn), lambda i, j: (i, j)),
        ],
        out_specs=pl.BlockSpec((tm, tn), lambda i, j: (i, j)),
        out_shape=jax.ShapeDtypeStruct((t, n), F32),
        compiler_params=_params(("parallel", "arbitrary"), 56),
        name="proj2_residual",
    )(a1, a2, w, w, h)


def _glu_body(y_ref, wa_ref, wb_ref, h_ref, o_ref):
    y = y_ref[...]
    ga = _dot(y, wa_ref[...])
    gb = _dot(y, wb_ref[...])
    o_ref[...] = h_ref[...] + ga * _sigmoid(gb)


def glu_residual(y, w, h):
    t, k = y.shape
    n = w.shape[1] // 2
    tm = _row_tile(t)
    tn = _pick_tile(n, (512, 256, 128))
    nb = n // tn
    return pl.pallas_call(
        _glu_body,
        grid=(t // tm, nb),
        in_specs=[
            pl.BlockSpec((tm, k), lambda i, j: (i, 0)),
            pl.BlockSpec((k, tn), lambda i, j: (0, j)),
            pl.BlockSpec((k, tn), lambda i, j: (0, j + nb)),
            pl.BlockSpec((tm, tn), lambda i, j: (i, j)),
        ],
        out_specs=pl.BlockSpec((tm, tn), lambda i, j: (i, j)),
        out_shape=jax.ShapeDtypeStruct((t, n), F32),
        compiler_params=_params(("parallel", "arbitrary"), 56),
        name="glu_residual",
    )(y, w, w, h)


def _pool_body(z_ref, w_ref, s_ref, o_ref):
    grp = pl.program_id(1)
    z = z_ref[...]
    length = z.shape[0]
    row = lax.broadcasted_iota(jnp.int32, (length, 1), 0)
    posf = (row + 1).astype(F32)

    def shifted(x, k):
        return jnp.where(row >= k, pltpu.roll(x, k, axis=0), 0.0)

    for gi, win in enumerate(POOL_WINDOWS):

        @pl.when(grp == gi)
        def _(win=win):
            acc = z
            span = 1
            while span < win:
                acc = acc + shifted(acc, span)
                span *= 2
            cnt = jnp.minimum(posf, float(win))
            m = acc / cnt - z
            y = _dot(m.astype(BF16), w_ref[...])
            o_ref[...] = (y * s_ref[...]).astype(o_ref.dtype)


def pool_mixer(z3, pool_w, pool_scale):
    bsz, length, _ = z3.shape
    ng, c, _ = pool_w.shape
    return pl.pallas_call(
        _pool_body,
        grid=(bsz, ng),
        in_specs=[
            pl.BlockSpec((None, length, c), lambda b, g: (b, 0, g)),
            pl.BlockSpec((None, c, c), lambda b, g: (g, 0, 0)),
            pl.BlockSpec((1, c), lambda b, g: (0, g)),
        ],
        out_specs=pl.BlockSpec((None, length, c), lambda b, g: (b, 0, g)),
        out_shape=jax.ShapeDtypeStruct((bsz, length, ng * c), BF16),
        compiler_params=_params(("parallel", "arbitrary"), 48),
        name="pool_mixer",
    )(z3, pool_w.astype(BF16), pool_scale.reshape(1, ng * c))


def _head_sum_matrix(width, head):
    shift = head.bit_length() - 1
    r = lax.broadcasted_iota(jnp.int32, (width, width), 0) >> shift
    c = lax.broadcasted_iota(jnp.int32, (width, width), 1) >> shift
    return jnp.where(r == c, 1.0, 0.0).astype(BF16)


def _head_sums(x, ones_bd):
    hi, lo = _split2(x)
    return _dot(hi, ones_bd) + _dot(lo, ones_bd)


def _rwkv_prep_body(zr_ref, zk_ref, zv_ref, zl_ref, pr_ref, pk_ref, pv_ref, pl_ref,
                    mur_ref, muk_ref, muv_ref, mul_ref, w0_ref, w2_ref, a0_ref, a2_ref, g2_ref,
                    kk_ref, ka_ref,
                    r_out, k_out, v_out, kap_out, b_out, lw_out, g_out):
    first = pl.program_id(1) == 0
    tl = zr_ref.shape[0]
    row = lax.broadcasted_iota(jnp.int32, (tl, 1), 0)

    def mix(z_ref, p_ref, mu_ref):
        z = z_ref[...]
        prev_last = jnp.where(first, 0.0, p_ref[7:8, :])
        prev = jnp.where(row == 0, prev_last, pltpu.roll(z, 1, axis=0))
        return z + (prev - z) * mu_ref[...]

    r = mix(zr_ref, pr_ref, mur_ref)
    k = mix(zk_ref, pk_ref, muk_ref)
    v = mix(zv_ref, pv_ref, muv_ref)
    lora = mix(zl_ref, pl_ref, mul_ref)
    w_lr = lora[:, 0:LANES]
    a_lr = lora[:, LANES:2 * LANES]
    g_lr = lora[:, 2 * LANES:]

    x = w0_ref[...] + _dot_acc(jnp.tanh(w_lr), w2_ref[...])
    y = -x
    softplus = jnp.maximum(y, 0.0) + jnp.log(1.0 + jnp.exp(-jnp.abs(y)))
    lw = -jnp.exp(-softplus - 0.5)
    a = _sigmoid(a0_ref[...] + _dot_acc(a_lr, a2_ref[...]))
    g = _dot_acc(_sigmoid(g_lr), g2_ref[...])

    ones_bd = _head_sum_matrix(LANES, RWKV_HEAD)
    kk = k * kk_ref[...]
    kk2 = kk * kk
    width = k.shape[1]
    ss = jnp.concatenate(
        [_head_sums(kk2[:, c:c + LANES], ones_bd) for c in range(0, width, LANES)], axis=1)
    kap = kk / jnp.maximum(jnp.sqrt(ss), 1e-12)

    r_out[...] = r.astype(r_out.dtype)
    k_out[...] = (k * (1.0 + (a - 1.0) * ka_ref[...])).astype(k_out.dtype)
    v_out[...] = v.astype(v_out.dtype)
    kap_out[...] = kap.astype(kap_out.dtype)
    b_out[...] = (kap * a).astype(b_out.dtype)
    lw_out[...] = lw
    g_out[...] = g.astype(g_out.dtype)


def rwkv_prep(z3, zl3, off_blocks, mu, mu_l, w0, w2p, a0, a2p, g2, k_k, k_a):
    bsz, length, _ = z3.shape
    width = w0.shape[-1]
    cw = 512
    ncb = width // cw
    tl = _pick_tile(length, (352, 272, 192, 176, 136, 128, 64))
    ob_r, ob_k, ob_v = off_blocks
    ob_l = 0
    t8 = tl // 8

    def cur(ob, fixed=False):
        if fixed:
            return pl.BlockSpec((None, tl, cw), lambda b, i, c: (b, i, ob))
        return pl.BlockSpec((None, tl, cw), lambda b, i, c: (b, i, ob + c))

    def prev(ob, fixed=False):
        if fixed:
            return pl.BlockSpec((None, 8, cw), lambda b, i, c: (b, jnp.maximum(i * t8 - 1, 0), ob))
        return pl.BlockSpec((None, 8, cw), lambda b, i, c: (b, jnp.maximum(i * t8 - 1, 0), ob + c))

    def vec(ob, fixed=False):
        if fixed:
            return pl.BlockSpec((1, cw), lambda b, i, c: (0, ob))
        return pl.BlockSpec((1, cw), lambda b, i, c: (0, ob + c))

    colvec = pl.BlockSpec((1, cw), lambda b, i, c: (0, c))
    out_spec = pl.BlockSpec((None, tl, cw), lambda b, i, c: (b, i, c))
    out_dtypes = [BF16, BF16, BF16, BF16, BF16, F32, BF16]
    return pl.pallas_call(
        _rwkv_prep_body,
        grid=(bsz, length // tl, ncb),
        in_specs=[
            cur(ob_r), cur(ob_k), cur(ob_v), cur(ob_l, True),
            prev(ob_r), prev(ob_k), prev(ob_v), prev(ob_l, True),
            vec(ob_r), vec(ob_k), vec(ob_v), vec(ob_l, True),
            colvec,
            pl.BlockSpec((LANES, cw), lambda b, i, c: (0, c)),
            colvec,
            pl.BlockSpec((LANES, cw), lambda b, i, c: (0, c)),
            pl.BlockSpec((RWKV_G_RANK, cw), lambda b, i, c: (0, c)),
            colvec, colvec,
        ],
        out_specs=[out_spec] * 7,
        out_shape=[jax.ShapeDtypeStruct((bsz, length, width), dt) for dt in out_dtypes],
        compiler_params=_params(("parallel", "parallel", "arbitrary"), 56),
        name="rwkv_prep",
    )(z3, z3, z3, zl3, z3, z3, z3, zl3, mu, mu, mu, mu_l,
      w0.reshape(1, width), w2p, a0.reshape(1, width), a2p, g2,
      k_k.reshape(1, width), k_a.reshape(1, width))


def _rwkv_scan_body(r_ref, k_ref, v_ref, kap_ref, b_ref, lw_ref, g_ref, rk_ref, gain_ref, bias_ref,
                    o_ref, h_ref, *, pairs):
    cn = RWKV_CHUNK
    hd = RWKV_HEAD

    @pl.when(pl.program_id(2) == 0)
    def _():
        h_ref[...] = jnp.zeros_like(h_ref)

    lane = lax.broadcasted_iota(jnp.int32, (1, LANES), 1)
    m0 = lane < hd
    rowi = lax.broadcasted_iota(jnp.int32, (cn, LANES), 0)
    coli = lax.broadcasted_iota(jnp.int32, (cn, LANES), 1) & (hd - 1)
    strict = rowi > coli
    incl = rowi >= coli
    eye2 = jnp.where(rowi == coli, 1.0, 0.0)
    tri = jnp.where(lax.broadcasted_iota(jnp.int32, (cn, cn), 0)
                    >= lax.broadcasted_iota(jnp.int32, (cn, cn), 1), 1.0, 0.0).astype(BF16)
    br = lax.broadcasted_iota(jnp.int32, (LANES, LANES), 0) >> (hd.bit_length() - 1)
    bc = lax.broadcasted_iota(jnp.int32, (LANES, LANES), 1) >> (hd.bit_length() - 1)
    bd_mask = br == bc
    ones_bd = jnp.where(bd_mask, 1.0, 0.0).astype(BF16)

    def stk(x):
        return jnp.concatenate([jnp.where(m0, x, 0.0), jnp.where(m0, 0.0, x)], axis=0).astype(BF16)

    ps = range(pairs)
    sls = [slice(p * LANES, (p + 1) * LANES) for p in ps]
    r = [r_ref[:, s].astype(F32) for s in sls]
    k = [k_ref[:, s].astype(F32) for s in sls]
    v = [v_ref[:, s].astype(F32) for s in sls]
    kap = [kap_ref[:, s].astype(F32) for s in sls]
    b = [b_ref[:, s].astype(F32) for s in sls]
    lw = [lw_ref[:, s] for s in sls]
    hbd = [h_ref[p] for p in ps]

    lsp = [_split3(x) for x in lw]
    lp = [_dot(tri, a) + _dot(tri, m) + _dot(tri, c) for a, m, c in lsp]
    lpc = [x[cn - 1:cn, :] for x in lp]
    pinv = [jnp.exp(-x) for x in lp]
    pend = [jnp.exp(c - x) for c, x in zip(lpc, lp)]
    kt = [kap[p] * jnp.exp(lp[p] - lw[p]) for p in ps]
    rt = [r[p] * jnp.exp(lp[p]) for p in ps]
    kh = [k[p] * pinv[p] for p in ps]
    bh = [b[p] * pinv[p] for p in ps]
    khc = [k[p] * pend[p] for p in ps]
    bhc = [b[p] * pend[p] for p in ps]

    ktrt = [jnp.concatenate([kt[p], rt[p]], axis=0).astype(BF16) for p in ps]
    ab_all = [_dot_nt(ktrt[p], stk(bh[p])) for p in ps]
    ak_all = [_dot_nt(ktrt[p], stk(kh[p])) for p in ps]
    n_mat = [jnp.where(strict, x[:cn], 0.0) for x in ab_all]
    a_k = [jnp.where(strict, x[:cn], 0.0) for x in ak_all]
    a_rb = [jnp.where(incl, x[cn:], 0.0) for x in ab_all]
    a_rk = [jnp.where(incl, x[cn:], 0.0) for x in ak_all]

    t_mat = [eye2 - x for x in n_mat]
    pw = n_mat
    span = 1
    while 2 * span < cn:
        pw = [_dot(x.astype(BF16), stk(x)) for x in pw]
        t_mat = [t + _dot(t.astype(BF16), stk(x)) for t, x in zip(t_mat, pw)]
        span *= 2
    t_bf = [t.astype(BF16) for t in t_mat]

    kbar = [_dot(t_bf[p], stk(kt[p])) for p in ps]
    akv = [_dot(a_k[p].astype(BF16), stk(v[p])) for p in ps]
    ubar = [_dot(t_bf[p], stk(akv[p])) for p in ps]
    hb = [x.astype(BF16) for x in hbd]
    u = [_dot(kbar[p].astype(BF16), hb[p]) + ubar[p] for p in ps]
    y = [_dot(rt[p].astype(BF16), hb[p]) + _dot(a_rk[p].astype(BF16), stk(v[p]))
         - _dot(a_rb[p].astype(BF16), stk(u[p])) for p in ps]

    kb_t = [jnp.concatenate([khc[p], -bhc[p]], axis=0).T.astype(BF16) for p in ps]
    vu = [jnp.concatenate([v[p], u[p]], axis=0).astype(BF16) for p in ps]
    g_mat = [_dot(kb_t[p], vu[p]) for p in ps]
    pc_col = [jnp.exp(jnp.broadcast_to(x, (LANES, LANES)).T) for x in lpc]
    for p in ps:
        h_ref[p] = jnp.where(bd_mask, pc_col[p] * hbd[p] + g_mat[p], 0.0)

    mean = [_head_sums(x, ones_bd) * (1.0 / hd) for x in y]
    yc = [y[p] - mean[p] for p in ps]
    var = [_head_sums(x * x, ones_bd) * (1.0 / hd) for x in yc]
    bonus = [_head_sums(r[p] * k[p] * rk_ref[:, sls[p]], ones_bd) * v[p] for p in ps]
    for p in ps:
        yn = yc[p] * lax.rsqrt(var[p] + RWKV_GN_EPS) * gain_ref[:, sls[p]] + bias_ref[:, sls[p]]
        o_ref[:, sls[p]] = ((yn + bonus[p]) * g_ref[:, sls[p]].astype(F32)).astype(o_ref.dtype)


def rwkv_scan(r, k, v, kap, b, lw, g, r_k, gn_gain, gn_bias):
    bsz, length, width = r.shape
    cw = _pick_tile(width, (2048, 1024, 512))
    pairs = cw // LANES
    seq = pl.BlockSpec((None, RWKV_CHUNK, cw), lambda bi, c, t: (bi, t, c))
    vec = pl.BlockSpec((1, cw), lambda bi, c, t: (0, c))
    return pl.pallas_call(
        functools.partial(_rwkv_scan_body, pairs=pairs),
        grid=(bsz, width // cw, length // RWKV_CHUNK),
        in_specs=[seq] * 7 + [vec] * 3,
        out_specs=seq,
        out_shape=jax.ShapeDtypeStruct((bsz, length, width), BF16),
        scratch_shapes=[pltpu.VMEM((pairs, LANES, LANES), F32)],
        compiler_params=_params(("parallel", "parallel", "arbitrary"), 48),
        name="rwkv_scan",
    )(r, k, v, kap, b, lw, g, r_k.reshape(1, width), gn_gain.reshape(1, width),
      gn_bias.reshape(1, width))


def _s5_param_body(lr_ref, li_ref, ldt_ref, btr_ref, bti_ref, cr_ref, ci_ref,
                   kst_ref, etr_ref, eti_ref, car_ref, cai_ref, a16r_ref, a16i_ref, *, groups):
    nc = SSM_CHUNK
    npow = 24
    tau = lax.broadcasted_iota(jnp.int32, (npow, SSM_STATE), 0).astype(F32)
    for gidx in range(groups):
        lr = lr_ref[gidx]
        li = li_ref[gidx]
        dt = jnp.exp(ldt_ref[gidx])
        mag = jnp.exp(tau * (lr * dt))
        ang = tau * (li * dt)
        pr = mag * jnp.cos(ang)
        pi = mag * jnp.sin(ang)
        ar = pr[1:2]
        ai = pi[1:2]
        den = lr * lr + li * li
        fr = ((ar - 1.0) * lr + ai * li) / den
        fi = (ai * lr - (ar - 1.0) * li) / den
        btr = btr_ref[gidx]
        bti = bti_ref[gidx]
        bbr = fr * btr - fi * bti
        bbi = fr * bti + fi * btr
        cr = cr_ref[gidx]
        ci = ci_ref[gidx]
        etr = jnp.concatenate([pr[s:s + 1] * bbr - pi[s:s + 1] * bbi for s in range(nc)], axis=0)
        eti = jnp.concatenate([pr[s:s + 1] * bbi + pi[s:s + 1] * bbr for s in range(nc)], axis=0)
        car = jnp.concatenate([pr[s:s + 1] * cr - pi[s:s + 1] * ci for s in range(1, nc + 1)], axis=0)
        cai = jnp.concatenate([pr[s:s + 1] * ci + pi[s:s + 1] * cr for s in range(1, nc + 1)], axis=0)
        kst_ref[gidx] = _dot_nt_acc(etr, cr) - _dot_nt_acc(eti, ci)
        etr_ref[gidx] = etr
        eti_ref[gidx] = eti
        car_ref[gidx] = car
        cai_ref[gidx] = cai
        a16r_ref[gidx] = jnp.broadcast_to(pr[nc:nc + 1], (8, SSM_STATE))
        a16i_ref[gidx] = jnp.broadcast_to(pi[nc:nc + 1], (8, SSM_STATE))


def s5_params(lam_re, lam_im, log_dt, b_re, b_im, c_re, c_im):
    ng, ns = lam_re.shape
    gc = SSM_GROUP
    gb = 8
    rows = SSM_CHUNK * gc

    def spec(*dims):
        return pl.BlockSpec((gb,) + dims, lambda i: (i,) + (0,) * len(dims))

    outs = [((rows, gc), F32)] + [((rows, ns), F32)] * 4 + [((8, ns), F32)] * 2
    return pl.pallas_call(
        functools.partial(_s5_param_body, groups=gb),
        grid=(ng // gb,),
        in_specs=[spec(1, ns), spec(1, ns), spec(1, 1), spec(gc, ns), spec(gc, ns), spec(gc, ns),
                  spec(gc, ns)],
        out_specs=[spec(*s) for s, _ in outs],
        out_shape=[jax.ShapeDtypeStruct((ng,) + s, dt) for s, dt in outs],
        compiler_params=_params(("parallel",), 32),
        name="s5_params",
    )(lam_re.reshape(ng, 1, ns), lam_im.reshape(ng, 1, ns), log_dt.reshape(ng, 1, 1),
      jnp.swapaxes(b_re, 1, 2), jnp.swapaxes(b_im, 1, 2), c_re, c_im)


def _s5_body(x_ref, toep_ref, wsr_ref, wsi_ref, wor_ref, woi_ref, a16r_ref, a16i_ref, d_ref,
             o_ref, sre_ref, sim_ref, xre_ref, xim_ref):
    rows = x_ref.shape[1]
    nchunk = rows // 8
    x0 = x_ref[0]
    x1 = x_ref[1]
    sre_ref[...] = _dot(x0, wsr_ref[0]) + _dot(x1, wsr_ref[1])
    sim_ref[...] = _dot(x0, wsi_ref[0]) + _dot(x1, wsi_ref[1])
    ar = a16r_ref[...]
    ai = a16i_ref[...]

    def step(c, carry):
        xr, xi = carry
        off = pl.multiple_of(c * 8, 8)
        xre_ref[pl.ds(off, 8), :] = xr
        xim_ref[pl.ds(off, 8), :] = xi
        sr = sre_ref[pl.ds(off, 8), :]
        si = sim_ref[pl.ds(off, 8), :]
        return ar * xr - ai * xi + sr, ar * xi + ai * xr + si

    zero = jnp.zeros((8, LANES), F32)
    lax.fori_loop(0, nchunk, step, (zero, zero))
    xr = xre_ref[...].astype(BF16)
    xi = xim_ref[...].astype(BF16)
    for q, xq in enumerate((x0, x1)):
        y = (_dot(xq, toep_ref[q]) + _dot(xr, wor_ref[q]) + _dot(xi, woi_ref[q])
             + xq.astype(F32) * d_ref[q])
        o_ref[q] = _gelu(y).astype(o_ref.dtype)


def s5_apply(xg, toep, wsr, wsi, wor, woi, a16r, a16i, dflat):
    ng, rows, cols = xg.shape

    def spec(*dims):
        return pl.BlockSpec((2,) + dims, lambda i: (i,) + (0,) * len(dims))

    return pl.pallas_call(
        _s5_body,
        grid=(ng // 2,),
        in_specs=[spec(rows, cols), spec(cols, cols), spec(cols, LANES), spec(cols, LANES),
                  spec(LANES, cols), spec(LANES, cols),
                  pl.BlockSpec((None, 8, LANES), lambda i: (i, 0, 0)),
                  pl.BlockSpec((None, 8, LANES), lambda i: (i, 0, 0)),
                  spec(1, cols)],
        out_specs=spec(rows, cols),
        out_shape=jax.ShapeDtypeStruct((ng, rows, cols), BF16),
        scratch_shapes=[pltpu.VMEM((rows, LANES), F32)] * 4,
        compiler_params=_params(("parallel",), 32),
        name="s5_apply",
    )(xg, toep, wsr, wsi, wor, woi, a16r, a16i, dflat)


def s5_mixer(u3, lam_re, lam_im, log_dt, b_re, b_im, c_re, c_im, d_skip):
    bsz, length, width = u3.shape
    ng, ns = lam_re.shape
    gc = SSM_GROUP
    nc = SSM_CHUNK
    kst, etr, eti, car, cai, a16r, a16i = s5_params(lam_re, lam_im, log_dt, b_re, b_im, c_re, c_im)

    k4 = kst.reshape(ng, nc, gc, gc)
    s_idx = jnp.arange(nc)[:, None]
    t_idx = jnp.arange(nc)[None, :]
    lag = t_idx - s_idx
    toep = jnp.where((lag >= 0)[None, :, :, None, None], k4[:, jnp.clip(lag, 0, nc - 1)], 0.0)
    toep = toep.transpose(0, 1, 3, 2, 4).reshape(ng, nc * gc, nc * gc).astype(BF16)
    odd = (jnp.arange(ng) % 2 == 1)[:, None, None]

    def lanes_by_parity(w):
        z = jnp.zeros_like(w)
        return jnp.where(odd, jnp.concatenate([z, w], -1), jnp.concatenate([w, z], -1))

    def flip_lag(e):
        return e.reshape(ng, nc, gc, ns)[:, ::-1].reshape(ng, nc * gc, ns)

    wsr = lanes_by_parity(flip_lag(etr)).astype(BF16)
    wsi = lanes_by_parity(flip_lag(eti)).astype(BF16)
    wor = jnp.swapaxes(lanes_by_parity(car), 1, 2).astype(BF16)
    woi = jnp.swapaxes(lanes_by_parity(-cai), 1, 2).astype(BF16)

    def pair_lanes(a):
        return a.reshape(ng // 2, 2, 8, ns).transpose(0, 2, 1, 3).reshape(ng // 2, 8, 2 * ns)

    dflat = jnp.tile(d_skip.reshape(ng, 1, gc), (1, 1, nc))

    nchunk = length // nc
    x = u3.reshape(bsz, nchunk, nc, ng, gc).transpose(3, 1, 0, 2, 4)
    x = jnp.pad(x, ((0, 0), (0, 0), (0, 8 - bsz), (0, 0), (0, 0))).reshape(ng, nchunk * 8, nc * gc)
    y = s5_apply(x, toep, wsr, wsi, wor, woi, pair_lanes(a16r), pair_lanes(a16i), dflat)
    y = y.reshape(ng, nchunk, 8, nc, gc)[:, :, :bsz].transpose(2, 1, 3, 0, 4)
    return y.reshape(bsz * length, width)


def _topk_rows(s, k):
    tops = []
    cur = s
    rank = jnp.full_like(s, float(PEER_KEYS - 1))
    for i in range(k):
        m = jnp.max(cur, axis=0, keepdims=True)
        tops.append(m)
        hit = cur >= m
        rank = jnp.where(hit, float(i), rank)
        cur = jnp.where(hit, NEG_INF, cur)
    return tops, rank


def _peer_route_body(q_ref, keys_ref, cnt_ref, e1_ref, r2_ref, e2_ref):
    nk = PEER_KEYS
    k = PEER_TOPK
    tm = q_ref.shape[0]
    wide, tall = 4, 3
    assert k == 16 and (wide + 1) * (tall + 1) > k and 2 * 9 > k
    a_idx = lax.broadcasted_iota(jnp.int32, (k, 1), 0)
    for c in range(tm // LANES):
        ts = slice(c * LANES, (c + 1) * LANES)
        for h in range(PEER_HEADS):
            q1 = q_ref[ts, (2 * h) * nk:(2 * h + 1) * nk]
            q2 = q_ref[ts, (2 * h + 1) * nk:(2 * h + 2) * nk]
            s1 = _dot_nt_acc(keys_ref[h, 0], q1)
            s2 = _dot_nt_acc(keys_ref[h, 1], q2)
            t1, _ = _topk_rows(s1, k)
            t2, rank2 = _topk_rows(s2, k)
            top1 = jnp.concatenate(t1, axis=0)
            top2 = jnp.concatenate(t2, axis=0)
            row_blocks = [t1[0] + top2] + [t1[a] + top2[0:8] for a in range(1, wide)]
            col_blocks = [jnp.where(a_idx >= wide, top1 + t2[b], NEG_INF) for b in range(tall)]
            cand = jnp.concatenate(row_blocks + col_blocks, axis=0)
            best = t1[0] + t2[0]
            zsum = jnp.zeros_like(best)
            cur = cand
            tau = best
            for _ in range(k):
                tau = jnp.max(cur, axis=0, keepdims=True)
                zsum = zsum + jnp.exp(tau - best)
                cur = jnp.where(cur >= tau, NEG_INF, cur)
            kept_rows = [jnp.sum(jnp.where(blk >= tau, 1.0, 0.0), axis=0, keepdims=True)
                         for blk in row_blocks]
            kept_cols = jnp.where(col_blocks[0] >= tau, 1.0, 0.0)
            for b in range(1, tall):
                kept_cols = kept_cols + jnp.where(col_blocks[b] >= tau, 1.0, 0.0)
            cnt = jnp.zeros_like(s1)
            for a in range(k):
                kept_a = kept_rows[a] if a < wide else kept_cols[a:a + 1]
                cnt = jnp.where(s1 == t1[a], kept_a, cnt)
            cnt_ref[h, :, ts] = cnt
            r2_ref[h, :, ts] = rank2.astype(BF16)
            e1_ref[h, :, ts] = jnp.exp(s1 - t1[0]) / zsum
            e2_ref[h, :, ts] = jnp.exp(s2 - t2[0]).astype(BF16)


def peer_route(q, keys):
    t = q.shape[0]
    tm = _pick_tile(t, (256, 128))
    big = pl.BlockSpec((PEER_HEADS, PEER_KEYS, tm), lambda i: (0, 0, i))
    sds = jax.ShapeDtypeStruct((PEER_HEADS, PEER_KEYS, t), F32)
    sds_bf = jax.ShapeDtypeStruct((PEER_HEADS, PEER_KEYS, t), BF16)
    return pl.pallas_call(
        _peer_route_body,
        grid=(t // tm,),
        in_specs=[pl.BlockSpec((tm, q.shape[1]), lambda i: (i, 0)),
                  pl.BlockSpec(keys.shape, lambda i: (0, 0, 0, 0))],
        out_specs=[big, big, big, big],
        out_shape=[sds, sds, sds_bf, sds_bf],
        compiler_params=_params(("parallel",), 48),
        name="peer_route",
    )(q, keys)


PEER_SLAB = 256


PEER_TILE = 512
SUBLANES = 8


def _peer_dense_body(hnt_ref, u_ref, v_ref, r2_ref, e2_ref, cnt_ref, e1_ref, o_ref, *, te):
    nk = PEER_KEYS
    rows_per_tile = te // nk
    rows_per_slab = PEER_SLAB // nk
    nslab = te // PEER_SLAB
    j = pl.program_id(1)
    base = (j % (SUBLANES // rows_per_tile)) * rows_per_tile

    @pl.when(j == 0)
    def _():
        o_ref[...] = jnp.zeros_like(o_ref)

    def gates(s, hold):
        out = []
        zero = jnp.zeros((), BF16)
        for r in range(rows_per_slab):
            row = base + s * rows_per_slab + r
            gate = None
            for h in range(PEER_HEADS):
                cnt_row = cnt_ref[h, pl.ds(row, 1), :]
                if hold is not None:
                    cnt_row = cnt_row + hold
                keep = r2_ref[h] < cnt_row.astype(BF16)
                term = jnp.where(keep, e2_ref[h] * e1_ref[h, pl.ds(row, 1), :].astype(BF16), zero)
                gate = term if gate is None else gate + term
            out.append(gate.astype(F32))
        return out

    hnt = hnt_ref[...]
    acts = [_dot(u_ref[s * PEER_SLAB:(s + 1) * PEER_SLAB, :], hnt) for s in range(nslab)]
    total = None
    hold = None
    for s in range(nslab):
        gate_blocks = gates(s, hold)
        parts = [gate_blocks[r] * _gelu(acts[s][r * nk:(r + 1) * nk, :])
                 for r in range(rows_per_slab)]
        w = jnp.concatenate(parts, axis=0).T.astype(BF16)
        c = _dot(w, v_ref[s * PEER_SLAB:(s + 1) * PEER_SLAB, :])
        hold = 0.0 * parts[-1][0:1, :]
        total = c if total is None else total + c
    o_ref[...] += total


def peer_dense(hn_t, u_all, v_all, layer, rank2, e2, cnt, e1):
    d, t = hn_t.shape
    ne = u_all.shape[1]
    tm = _pick_tile(t, (512, 256, 128))
    te = min(PEER_TILE, ne)
    rows_per_tile = te // PEER_KEYS
    tiles_per_block = SUBLANES // rows_per_tile
    big = pl.BlockSpec((PEER_HEADS, PEER_KEYS, tm), lambda i, j: (0, 0, i))
    rows = pl.BlockSpec((PEER_HEADS, SUBLANES, tm), lambda i, j: (0, j // tiles_per_block, i))
    table = pl.BlockSpec((None, te, d), lambda i, j: (layer, j, 0))
    return pl.pallas_call(
        functools.partial(_peer_dense_body, te=te),
        grid=(t // tm, ne // te),
        in_specs=[pl.BlockSpec((d, tm), lambda i, j: (0, i)), table, table, big, big, rows, rows],
        out_specs=pl.BlockSpec((tm, d), lambda i, j: (i, 0)),
        out_shape=jax.ShapeDtypeStruct((t, d), F32),
        compiler_params=_params(("parallel", "arbitrary"), 60),
        name="peer_dense",
    )(hn_t, u_all, v_all, rank2, e2, cnt, e1)


def peer_ffn(h, gain, w_q_bf, keys, u_all, v_all, layer):
    hn, hn_t = rms_norm_bf16(h, gain, with_transpose=True)
    q = matmul(hn, w_q_bf)
    cnt, e1, rank2, e2 = peer_route(q, keys)
    return peer_dense(hn_t, u_all, v_all, layer, rank2, e2, cnt, e1)


def _final_norm_body(x_ref, dx_ref, g_ref, o_ref):
    x = x_ref[0] + dx_ref[0]
    inv = lax.rsqrt(jnp.mean(x * x, axis=-1, keepdims=True) + NORM_EPS)
    o_ref[...] = (x * inv) * g_ref[...]


def final_norm(h3, delta3, gain, first, count):
    bsz, _, d = h3.shape
    tm = _pick_tile(count, (256, 128, 64, 32, 16, 8))
    assert first % 8 == 0
    rows = pl.BlockSpec((pl.Element(1), pl.Element(tm), pl.Element(d)),
                        lambda b, i: (b, pl.multiple_of(first + i * tm, 8), 0))
    return pl.pallas_call(
        _final_norm_body,
        grid=(bsz, count // tm),
        in_specs=[rows, rows, pl.BlockSpec((1, d), lambda b, i: (0, 0))],
        out_specs=pl.BlockSpec((None, tm, d), lambda b, i: (b, i, 0)),
        out_shape=jax.ShapeDtypeStruct((bsz, count, d), F32),
        compiler_params=_params(("parallel", "parallel"), 32),
        name="final_norm",
    )(h3, delta3, gain.reshape(1, d))


def _even_layer_weights(w_in, shift_mu, w2, a2, width):
    pool_w = w_in.shape[1] - shift_mu.shape[0]
    c0 = pool_w + 3 * width
    c1 = c0 + RWKV_W_RANK
    c2 = c1 + RWKV_A_RANK
    d = w_in.shape[0]
    zw = jnp.zeros((d, LANES - RWKV_W_RANK), w_in.dtype)
    za = jnp.zeros((d, LANES - RWKV_A_RANK), w_in.dtype)
    w_main = w_in[:, :c0].astype(BF16)
    w_lora = jnp.concatenate([w_in[:, c0:c1], zw, w_in[:, c1:c2], za, w_in[:, c2:]],
                             axis=1).astype(BF16)
    s0 = 3 * width
    s1 = s0 + RWKV_W_RANK
    s2 = s1 + RWKV_A_RANK
    mu_main = jnp.concatenate([jnp.zeros((pool_w,), F32), shift_mu[:s0]]).reshape(1, -1)
    mu_lora = jnp.concatenate([
        shift_mu[s0:s1], jnp.zeros((LANES - RWKV_W_RANK,), F32), shift_mu[s1:s2],
        jnp.zeros((LANES - RWKV_A_RANK,), F32), shift_mu[s2:]]).reshape(1, -1)
    w2p = jnp.pad(w2, ((0, LANES - RWKV_W_RANK), (0, 0)))
    a2p = jnp.pad(a2, ((0, LANES - RWKV_A_RANK), (0, 0)))
    return w_main, w_lora, mu_main, mu_lora, w2p, a2p, pool_w


def kernel(x, meta_tokens, mix_norm_gain, ffn_norm_gain, final_norm_gain, w_in_even, pool_w, pool_scale, shift_mu, rwkv_w0, rwkv_w2, rwkv_a0, rwkv_a2, rwkv_g2, rwkv_k_k, rwkv_k_a, rwkv_r_k, rwkv_gn_gain, rwkv_gn_bias, w_out_even, w_in_odd, ssm_lam_re, ssm_lam_im, ssm_log_dt, ssm_b_re, ssm_b_im, ssm_c_re, ssm_c_im, ssm_d, w_glu, peer_w_q, peer_sub_keys, peer_u, peer_v):
    bsz, seq, d = x.shape
    depth = mix_norm_gain.shape[0]
    real = N_META + seq
    length = -(-real // SEQ_ALIGN) * SEQ_ALIGN
    meta = jnp.broadcast_to(meta_tokens[None].astype(x.dtype), (bsz, N_META, d))
    h = jnp.concatenate([meta, x, jnp.zeros((bsz, length - real, d), x.dtype)], axis=1)
    h = h.reshape(bsz * length, d)
    peer_u_bf = peer_u.astype(BF16)
    peer_v_bf = peer_v.astype(BF16)

    pending = None
    for layer in range(depth):
        if pending is None:
            hn = rms_norm_bf16(h, mix_norm_gain[layer])
        else:
            hn, h = rms_norm_bf16(h, mix_norm_gain[layer], delta=pending)
        if layer % 2 == 0:
            i = layer // 2
            width = rwkv_w0.shape[-1]
            w_main, w_lora, mu_main, mu_lora, w2p, a2p, pool_cols = _even_layer_weights(
                w_in_even[i], shift_mu[i], rwkv_w2[i], rwkv_a2[i], width)
            z3 = matmul(hn, w_main).reshape(bsz, length, -1)
            zl3 = matmul(hn, w_lora).reshape(bsz, length, -1)
            y_pool = pool_mixer(z3, pool_w[i], pool_scale[i])
            ob = pool_cols // 512
            wb = width // 512
            r, k, v, kap, b, lw, g = rwkv_prep(
                z3, zl3, (ob, ob + wb, ob + 2 * wb), mu_main, mu_lora, rwkv_w0[i], w2p,
                rwkv_a0[i], a2p, rwkv_g2[i], rwkv_k_k[i], rwkv_k_a[i])
            y_rwkv = rwkv_scan(r, k, v, kap, b, lw, g, rwkv_r_k[i], rwkv_gn_gain[i],
                               rwkv_gn_bias[i])
            h = proj2_residual(y_pool.reshape(bsz * length, -1), y_rwkv.reshape(bsz * length, -1),
                               w_out_even[i].astype(BF16), h)
        else:
            j = layer // 2
            u = matmul(hn, w_in_odd[j].astype(BF16), out_dtype=BF16)
            y = s5_mixer(u.reshape(bsz, length, -1), ssm_lam_re[j], ssm_lam_im[j], ssm_log_dt[j],
                         ssm_b_re[j], ssm_b_im[j], ssm_c_re[j], ssm_c_im[j], ssm_d[j])
            h = glu_residual(y, w_glu[j].astype(BF16), h)
        pending = peer_ffn(h, ffn_norm_gain[layer], peer_w_q[layer].astype(BF16),
                           peer_sub_keys[layer], peer_u_bf, peer_v_bf, layer)
    return final_norm(h.reshape(bsz, length, d), pending.reshape(bsz, length, d), final_norm_gain,
                      N_META, seq)
```

```python
import functools

import jax
import jax.numpy as jnp
from jax import lax
from jax.experimental import pallas as pl
from jax.experimental.pallas import tpu as pltpu

F32 = jnp.float32
BF16 = jnp.bfloat16

NORM_EPS = 1e-6
N_META = 16
POOL_WINDOWS = (2, 4, 8, 16)
RWKV_HEAD = 64
RWKV_W_RANK = 96
RWKV_A_RANK = 96
RWKV_G_RANK = 256
RWKV_GN_EPS = 64e-5
RWKV_CHUNK = 64
SSM_GROUP = 16
SSM_STATE = 64
SSM_CHUNK = 16
PEER_KEYS = 128
PEER_HEADS = 8
PEER_TOPK = 16
LANES = 128
SEQ_ALIGN = 128

NEG_INF = float("-inf")


def _params(sem, vmem_mb):
    return pltpu.CompilerParams(dimension_semantics=sem, vmem_limit_bytes=vmem_mb << 20)


def _split2(x):
    hi = x.astype(BF16)
    lo = (x - hi.astype(F32)).astype(BF16)
    return hi, lo


def _dot(a, b):
    return jnp.dot(a, b, preferred_element_type=F32)


def _dot_nt(a, b):
    return lax.dot_general(a, b, (((1,), (1,)), ((), ())), preferred_element_type=F32)


def _dot_acc(a, b):
    ah, al = _split2(a)
    bh, bl = _split2(b)
    return _dot(ah, bh) + _dot(al, bh) + _dot(ah, bl)


def _dot_nt_acc(a, b):
    ah, al = _split2(a)
    bh, bl = _split2(b)
    return _dot_nt(ah, bh) + _dot_nt(al, bh) + _dot_nt(ah, bl)


def _gelu(x):
    return 0.5 * x * (1.0 + lax.erf(x * 0.7071067811865476))


def _sigmoid(x):
    return 1.0 / (1.0 + jnp.exp(-x))


def _pick_tile(n, prefs):
    for t in prefs:
        if n % t == 0:
            return t
    return n


def _rms_norm_body(*refs, with_transpose, with_delta):
    refs = list(refs)
    x = refs.pop(0)[...]
    if with_delta:
        x = x + refs.pop(0)[...]
    g_ref, o_ref = refs.pop(0), refs.pop(0)
    inv = lax.rsqrt(jnp.mean(x * x, axis=-1, keepdims=True) + NORM_EPS)
    xn = (x * inv) * g_ref[...]
    o_ref[...] = xn.astype(BF16)
    if with_transpose:
        refs.pop(0)[...] = xn.T.astype(BF16)
    if with_delta:
        refs.pop(0)[...] = x


def rms_norm_bf16(x, gain, *, delta=None, with_transpose=False):
    t, d = x.shape
    tm = _pick_tile(t, (256, 128, 64, 32, 16, 8))
    row = pl.BlockSpec((tm, d), lambda i: (i, 0))
    args = [x] + ([delta] if delta is not None else []) + [gain.reshape(1, d)]
    in_specs = [row] * (len(args) - 1) + [pl.BlockSpec((1, d), lambda i: (0, 0))]
    out_shape = [jax.ShapeDtypeStruct((t, d), BF16)]
    out_specs = [row]
    if with_transpose:
        out_shape.append(jax.ShapeDtypeStruct((d, t), BF16))
        out_specs.append(pl.BlockSpec((d, tm), lambda i: (0, i)))
    if delta is not None:
        out_shape.append(jax.ShapeDtypeStruct((t, d), F32))
        out_specs.append(row)
    res = pl.pallas_call(
        functools.partial(_rms_norm_body, with_transpose=with_transpose,
                          with_delta=delta is not None),
        grid=(t // tm,),
        in_specs=in_specs,
        out_specs=out_specs,
        out_shape=out_shape,
        compiler_params=_params(("parallel",), 48),
        name="rms_norm",
    )(*args)
    return res if len(res) > 1 else res[0]


def _row_tile(t):
    for cand in (1088, 1024, 768, 512, 384, 256, 128, 64, 32, 16):
        if t % cand == 0:
            return cand
    return t


def _matmul_body(a_ref, w_ref, o_ref):
    o_ref[...] = _dot(a_ref[...], w_ref[...]).astype(o_ref.dtype)


def matmul(a, w, *, out_dtype=F32):
    t, k = a.shape
    n = w.shape[1]
    tm = _row_tile(t)
    tn = _pick_tile(n, (512, 256, 128))
    return pl.pallas_call(
        _matmul_body,
        grid=(t // tm, n // tn),
        in_specs=[pl.BlockSpec((tm, k), lambda i, j: (i, 0)), pl.BlockSpec((k, tn), lambda i, j: (0, j))],
        out_specs=pl.BlockSpec((tm, tn), lambda i, j: (i, j)),
        out_shape=jax.ShapeDtypeStruct((t, n), out_dtype),
        compiler_params=_params(("parallel", "arbitrary"), 48),
        name="matmul",
    )(a, w)


def _proj2_body(a1_ref, a2_ref, w1_ref, w2_ref, h_ref, o_ref):
    o_ref[...] = h_ref[...] + _dot(a1_ref[...], w1_ref[...]) + _dot(a2_ref[...], w2_ref[...])


def proj2_residual(a1, a2, w, h):
    t, k1 = a1.shape
    k2 = a2.shape[1]
    n = w.shape[1]
    assert k1 == k2
    tm = _row_tile(t)
    tn = _pick_tile(n, (512, 256, 128))
    return pl.pallas_call(
        _proj2_body,
        grid=(t // tm, n // tn),
        in_specs=[
            pl.BlockSpec((tm, k1), lambda i, j: (i, 0)),
            pl.BlockSpec((tm, k2), lambda i, j: (i, 0)),
            pl.BlockSpec((k1, tn), lambda i, j: (0, j)),
            pl.BlockSpec((k2, tn), lambda i, j: (1, j)),
            pl.BlockSpec((tm, tn), lambda i, j: (i, j)),
        ],
        out_specs=pl.BlockSpec((tm, tn), lambda i, j: (i, j)),
        out_shape=jax.ShapeDtypeStruct((t, n), F32),
        compiler_params=_params(("parallel", "arbitrary"), 56),
        name="proj2_residual",
    )(a1, a2, w, w, h)


def _glu_body(y_ref, wa_ref, wb_ref, h_ref, o_ref):
    y = y_ref[...]
    ga = _dot(y, wa_ref[...])
    gb = _dot(y, wb_ref[...])
    o_ref[...] = h_ref[...] + ga * _sigmoid(gb)


def glu_residual(y, w, h):
    t, k = y.shape
    n = w.shape[1] // 2
    tm = _row_tile(t)
    tn = _pick_tile(n, (512, 256, 128))
    nb = n // tn
    return pl.pallas_call(
        _glu_body,
        grid=(t // tm, nb),
        in_specs=[
            pl.BlockSpec((tm, k), lambda i, j: (i, 0)),
            pl.BlockSpec((k, tn), lambda i, j: (0, j)),
            pl.BlockSpec((k, tn), lambda i, j: (0, j + nb)),
            pl.BlockSpec((tm, tn), lambda i, j: (i, j)),
        ],
        out_specs=pl.BlockSpec((tm, tn), lambda i, j: (i, j)),
        out_shape=jax.ShapeDtypeStruct((t, n), F32),
        compiler_params=_params(("parallel", "arbitrary"), 56),
        name="glu_residual",
    )(y, w, w, h)


def _pool_body(z_ref, w_ref, s_ref, o_ref):
    grp = pl.program_id(1)
    z = z_ref[...]
    length = z.shape[0]
    row = lax.broadcasted_iota(jnp.int32, (length, 1), 0)
    posf = (row + 1).astype(F32)

    def shifted(x, k):
        return jnp.where(row >= k, pltpu.roll(x, k, axis=0), 0.0)

    for gi, win in enumerate(POOL_WINDOWS):

        @pl.when(grp == gi)
        def _(win=win):
            acc = z
            span = 1
            while span < win:
                acc = acc + shifted(acc, span)
                span *= 2
            cnt = jnp.minimum(posf, float(win))
            m = acc / cnt - z
            y = _dot(m.astype(BF16), w_ref[...])
            o_ref[...] = (y * s_ref[...]).astype(o_ref.dtype)


def pool_mixer(z3, pool_w, pool_scale):
    bsz, length, _ = z3.shape
    ng, c, _ = pool_w.shape
    return pl.pallas_call(
        _pool_body,
        grid=(bsz, ng),
        in_specs=[
            pl.BlockSpec((None, length, c), lambda b, g: (b, 0, g)),
            pl.BlockSpec((None, c, c), lambda b, g: (g, 0, 0)),
            pl.BlockSpec((1, c), lambda b, g: (0, g)),
        ],
        out_specs=pl.BlockSpec((None, length, c), lambda b, g: (b, 0, g)),
        out_shape=jax.ShapeDtypeStruct((bsz, length, ng * c), BF16),
        compiler_params=_params(("parallel", "arbitrary"), 48),
        name="pool_mixer",
    )(z3, pool_w.astype(BF16), pool_scale.reshape(1, ng * c))


def _head_sum_matrix(width, head):
    shift = head.bit_length() - 1
    r = lax.broadcasted_iota(jnp.int32, (width, width), 0) >> shift
    c = lax.broadcasted_iota(jnp.int32, (width, width), 1) >> shift
    return jnp.where(r == c, 1.0, 0.0).astype(BF16)


def _head_sums(x, ones_bd):
    hi, lo = _split2(x)
    return _dot(hi, ones_bd) + _dot(lo, ones_bd)


def _rwkv_prep_body(zr_ref, zk_ref, zv_ref, zl_ref, pr_ref, pk_ref, pv_ref, pl_ref,
                    mur_ref, muk_ref, muv_ref, mul_ref, w0_ref, w2_ref, a0_ref, a2_ref, g2_ref,
                    kk_ref, ka_ref,
                    r_out, k_out, v_out, kap_out, b_out, lw_out, g_out):
    first = pl.program_id(1) == 0
    tl = zr_ref.shape[0]
    row = lax.broadcasted_iota(jnp.int32, (tl, 1), 0)

    def mix(z_ref, p_ref, mu_ref):
        z = z_ref[...]
        prev_last = jnp.where(first, 0.0, p_ref[7:8, :])
        prev = jnp.where(row == 0, prev_last, pltpu.roll(z, 1, axis=0))
        return z + (prev - z) * mu_ref[...]

    r = mix(zr_ref, pr_ref, mur_ref)
    k = mix(zk_ref, pk_ref, muk_ref)
    v = mix(zv_ref, pv_ref, muv_ref)
    lora = mix(zl_ref, pl_ref, mul_ref)
    w_lr = lora[:, 0:LANES]
    a_lr = lora[:, LANES:2 * LANES]
    g_lr = lora[:, 2 * LANES:]

    x = w0_ref[...] + _dot_acc(jnp.tanh(w_lr), w2_ref[...])
    y = -x
    softplus = jnp.maximum(y, 0.0) + jnp.log(1.0 + jnp.exp(-jnp.abs(y)))
    lw = -jnp.exp(-softplus - 0.5)
    a = _sigmoid(a0_ref[...] + _dot_acc(a_lr, a2_ref[...]))
    g = _dot_acc(_sigmoid(g_lr), g2_ref[...])

    ones_bd = _head_sum_matrix(LANES, RWKV_HEAD)
    kk = k * kk_ref[...]
    kk2 = kk * kk
    width = k.shape[1]
    ss = jnp.concatenate(
        [_head_sums(kk2[:, c:c + LANES], ones_bd) for c in range(0, width, LANES)], axis=1)
    kap = kk / jnp.maximum(jnp.sqrt(ss), 1e-12)

    r_out[...] = r.astype(r_out.dtype)
    k_out[...] = (k * (1.0 + (a - 1.0) * ka_ref[...])).astype(k_out.dtype)
    v_out[...] = v.astype(v_out.dtype)
    kap_out[...] = kap.astype(kap_out.dtype)
    b_out[...] = (kap * a).astype(b_out.dtype)
    lw_out[...] = lw
    g_out[...] = g.astype(g_out.dtype)


def rwkv_prep(z3, zl3, off_blocks, mu, mu_l, w0, w2p, a0, a2p, g2, k_k, k_a):
    bsz, length, _ = z3.shape
    width = w0.shape[-1]
    cw = 512
    ncb = width // cw
    tl = _pick_tile(length, (352, 272, 192, 176, 136, 128, 64))
    ob_r, ob_k, ob_v = off_blocks
    ob_l = 0
    t8 = tl // 8

    def cur(ob, fixed=False):
        if fixed:
            return pl.BlockSpec((None, tl, cw), lambda b, i, c: (b, i, ob))
        return pl.BlockSpec((None, tl, cw), lambda b, i, c: (b, i, ob + c))

    def prev(ob, fixed=False):
        if fixed:
            return pl.BlockSpec((None, 8, cw), lambda b, i, c: (b, jnp.maximum(i * t8 - 1, 0), ob))
        return pl.BlockSpec((None, 8, cw), lambda b, i, c: (b, jnp.maximum(i * t8 - 1, 0), ob + c))

    def vec(ob, fixed=False):
        if fixed:
            return pl.BlockSpec((1, cw), lambda b, i, c: (0, ob))
        return pl.BlockSpec((1, cw), lambda b, i, c: (0, ob + c))

    colvec = pl.BlockSpec((1, cw), lambda b, i, c: (0, c))
    out_spec = pl.BlockSpec((None, tl, cw), lambda b, i, c: (b, i, c))
    out_dtypes = [BF16, BF16, BF16, BF16, BF16, F32, BF16]
    return pl.pallas_call(
        _rwkv_prep_body,
        grid=(bsz, length // tl, ncb),
        in_specs=[
            cur(ob_r), cur(ob_k), cur(ob_v), cur(ob_l, True),
            prev(ob_r), prev(ob_k), prev(ob_v), prev(ob_l, True),
            vec(ob_r), vec(ob_k), vec(ob_v), vec(ob_l, True),
            colvec,
            pl.BlockSpec((LANES, cw), lambda b, i, c: (0, c)),
            colvec,
            pl.BlockSpec((LANES, cw), lambda b, i, c: (0, c)),
            pl.BlockSpec((RWKV_G_RANK, cw), lambda b, i, c: (0, c)),
            colvec, colvec,
        ],
        out_specs=[out_spec] * 7,
        out_shape=[jax.ShapeDtypeStruct((bsz, length, width), dt) for dt in out_dtypes],
        compiler_params=_params(("parallel", "parallel", "arbitrary"), 56),
        name="rwkv_prep",
    )(z3, z3, z3, zl3, z3, z3, z3, zl3, mu, mu, mu, mu_l,
      w0.reshape(1, width), w2p, a0.reshape(1, width), a2p, g2,
      k_k.reshape(1, width), k_a.reshape(1, width))


def _rwkv_scan_body(r_ref, k_ref, v_ref, kap_ref, b_ref, lw_ref, g_ref, rk_ref, gain_ref, bias_ref,
                    o_ref, h_ref, *, pairs):
    cn = RWKV_CHUNK
    hd = RWKV_HEAD

    @pl.when(pl.program_id(2) == 0)
    def _():
        h_ref[...] = jnp.zeros_like(h_ref)

    lane = lax.broadcasted_iota(jnp.int32, (1, LANES), 1)
    m0 = lane < hd
    rowi = lax.broadcasted_iota(jnp.int32, (cn, LANES), 0)
    coli = lax.broadcasted_iota(jnp.int32, (cn, LANES), 1) & (hd - 1)
    strict = rowi > coli
    incl = rowi >= coli
    eye2 = jnp.where(rowi == coli, 1.0, 0.0)
    tri = jnp.where(lax.broadcasted_iota(jnp.int32, (cn, cn), 0)
                    >= lax.broadcasted_iota(jnp.int32, (cn, cn), 1), 1.0, 0.0).astype(BF16)
    br = lax.broadcasted_iota(jnp.int32, (LANES, LANES), 0) >> (hd.bit_length() - 1)
    bc = lax.broadcasted_iota(jnp.int32, (LANES, LANES), 1) >> (hd.bit_length() - 1)
    bd_mask = br == bc
    ones_bd = jnp.where(bd_mask, 1.0, 0.0).astype(BF16)

    def stk(x):
        return jnp.concatenate([jnp.where(m0, x, 0.0), jnp.where(m0, 0.0, x)], axis=0).astype(BF16)

    ps = range(pairs)
    sls = [slice(p * LANES, (p + 1) * LANES) for p in ps]
    r = [r_ref[:, s].astype(F32) for s in sls]
    k = [k_ref[:, s].astype(F32) for s in sls]
    v = [v_ref[:, s].astype(F32) for s in sls]
    kap = [kap_ref[:, s].astype(F32) for s in sls]
    b = [b_ref[:, s].astype(F32) for s in sls]
    lw = [lw_ref[:, s] for s in sls]
    hbd = [h_ref[p] for p in ps]

    lsp = [_split2(x) for x in lw]
    lp = [_dot(tri, a) + _dot(tri, c) for a, c in lsp]
    lpc = [x[cn - 1:cn, :] for x in lp]
    pinv = [jnp.exp(-x) for x in lp]
    pend = [jnp.exp(c - x) for c, x in zip(lpc, lp)]
    kt = [kap[p] * jnp.exp(lp[p] - lw[p]) for p in ps]
    rt = [r[p] * jnp.exp(lp[p]) for p in ps]
    kh = [k[p] * pinv[p] for p in ps]
    bh = [b[p] * pinv[p] for p in ps]
    khc = [k[p] * pend[p] for p in ps]
    bhc = [b[p] * pend[p] for p in ps]

    ktrt = [jnp.concatenate([kt[p], rt[p]], axis=0).astype(BF16) for p in ps]
    ab_all = [_dot_nt(ktrt[p], stk(bh[p])) for p in ps]
    ak_all = [_dot_nt(ktrt[p], stk(kh[p])) for p in ps]
    n_mat = [jnp.where(strict, x[:cn], 0.0) for x in ab_all]
    a_k = [jnp.where(strict, x[:cn], 0.0) for x in ak_all]
    a_rb = [jnp.where(incl, x[cn:], 0.0) for x in ab_all]
    a_rk = [jnp.where(incl, x[cn:], 0.0) for x in ak_all]

    t_mat = [eye2 - x for x in n_mat]
    pw = n_mat
    span = 1
    while 2 * span < cn:
        pw = [_dot(x.astype(BF16), stk(x)) for x in pw]
        t_mat = [t + _dot(t.astype(BF16), stk(x)) for t, x in zip(t_mat, pw)]
        span *= 2
    t_bf = [t.astype(BF16) for t in t_mat]

    kbar = [_dot(t_bf[p], stk(kt[p])) for p in ps]
    akv = [_dot(a_k[p].astype(BF16), stk(v[p])) for p in ps]
    ubar = [_dot(t_bf[p], stk(akv[p])) for p in ps]
    hb = [x.astype(BF16) for x in hbd]
    u = [_dot(kbar[p].astype(BF16), hb[p]) + ubar[p] for p in ps]
    y = [_dot(rt[p].astype(BF16), hb[p]) + _dot(a_rk[p].astype(BF16), stk(v[p]))
         - _dot(a_rb[p].astype(BF16), stk(u[p])) for p in ps]

    kb_t = [jnp.concatenate([khc[p], -bhc[p]], axis=0).T.astype(BF16) for p in ps]
    vu = [jnp.concatenate([v[p], u[p]], axis=0).astype(BF16) for p in ps]
    g_mat = [_dot(kb_t[p], vu[p]) for p in ps]
    pc_col = [jnp.exp(jnp.broadcast_to(x, (LANES, LANES)).T) for x in lpc]
    for p in ps:
        h_ref[p] = jnp.where(bd_mask, pc_col[p] * hbd[p] + g_mat[p], 0.0)

    mean = [_head_sums(x, ones_bd) * (1.0 / hd) for x in y]
    yc = [y[p] - mean[p] for p in ps]
    var = [_head_sums(x * x, ones_bd) * (1.0 / hd) for x in yc]
    bonus = [_head_sums(r[p] * k[p] * rk_ref[:, sls[p]], ones_bd) * v[p] for p in ps]
    for p in ps:
        yn = yc[p] * lax.rsqrt(var[p] + RWKV_GN_EPS) * gain_ref[:, sls[p]] + bias_ref[:, sls[p]]
        o_ref[:, sls[p]] = ((yn + bonus[p]) * g_ref[:, sls[p]].astype(F32)).astype(o_ref.dtype)


def rwkv_scan(r, k, v, kap, b, lw, g, r_k, gn_gain, gn_bias):
    bsz, length, width = r.shape
    cw = _pick_tile(width, (2048, 1024, 512))
    pairs = cw // LANES
    seq = pl.BlockSpec((None, RWKV_CHUNK, cw), lambda bi, c, t: (bi, t, c))
    vec = pl.BlockSpec((1, cw), lambda bi, c, t: (0, c))
    return pl.pallas_call(
        functools.partial(_rwkv_scan_body, pairs=pairs),
        grid=(bsz, width // cw, length // RWKV_CHUNK),
        in_specs=[seq] * 7 + [vec] * 3,
        out_specs=seq,
        out_shape=jax.ShapeDtypeStruct((bsz, length, width), BF16),
        scratch_shapes=[pltpu.VMEM((pairs, LANES, LANES), F32)],
        compiler_params=_params(("parallel", "parallel", "arbitrary"), 48),
        name="rwkv_scan",
    )(r, k, v, kap, b, lw, g, r_k.reshape(1, width), gn_gain.reshape(1, width),
      gn_bias.reshape(1, width))


def _s5_param_body(lr_ref, li_ref, ldt_ref, btr_ref, bti_ref, cr_ref, ci_ref,
                   kst_ref, etr_ref, eti_ref, car_ref, cai_ref, a16r_ref, a16i_ref, *, groups):
    nc = SSM_CHUNK
    npow = 24
    tau = lax.broadcasted_iota(jnp.int32, (npow, SSM_STATE), 0).astype(F32)
    for gidx in range(groups):
        lr = lr_ref[gidx]
        li = li_ref[gidx]
        dt = jnp.exp(ldt_ref[gidx])
        mag = jnp.exp(tau * (lr * dt))
        ang = tau * (li * dt)
        pr = mag * jnp.cos(ang)
        pi = mag * jnp.sin(ang)
        ar = pr[1:2]
        ai = pi[1:2]
        den = lr * lr + li * li
        fr = ((ar - 1.0) * lr + ai * li) / den
        fi = (ai * lr - (ar - 1.0) * li) / den
        btr = btr_ref[gidx]
        bti = bti_ref[gidx]
        bbr = fr * btr - fi * bti
        bbi = fr * bti + fi * btr
        cr = cr_ref[gidx]
        ci = ci_ref[gidx]
        etr = jnp.concatenate([pr[s:s + 1] * bbr - pi[s:s + 1] * bbi for s in range(nc)], axis=0)
        eti = jnp.concatenate([pr[s:s + 1] * bbi + pi[s:s + 1] * bbr for s in range(nc)], axis=0)
        car = jnp.concatenate([pr[s:s + 1] * cr - pi[s:s + 1] * ci for s in range(1, nc + 1)], axis=0)
        cai = jnp.concatenate([pr[s:s + 1] * ci + pi[s:s + 1] * cr for s in range(1, nc + 1)], axis=0)
        kst_ref[gidx] = _dot_nt_acc(etr, cr) - _dot_nt_acc(eti, ci)
        etr_ref[gidx] = etr
        eti_ref[gidx] = eti
        car_ref[gidx] = car
        cai_ref[gidx] = cai
        a16r_ref[gidx] = jnp.broadcast_to(pr[nc:nc + 1], (8, SSM_STATE))
        a16i_ref[gidx] = jnp.broadcast_to(pi[nc:nc + 1], (8, SSM_STATE))


def s5_params(lam_re, lam_im, log_dt, b_re, b_im, c_re, c_im):
    ng, ns = lam_re.shape
    gc = SSM_GROUP
    gb = 8
    rows = SSM_CHUNK * gc

    def spec(*dims):
        return pl.BlockSpec((gb,) + dims, lambda i: (i,) + (0,) * len(dims))

    outs = [((rows, gc), F32)] + [((rows, ns), F32)] * 4 + [((8, ns), F32)] * 2
    return pl.pallas_call(
        functools.partial(_s5_param_body, groups=gb),
        grid=(ng // gb,),
        in_specs=[spec(1, ns), spec(1, ns), spec(1, 1), spec(gc, ns), spec(gc, ns), spec(gc, ns),
                  spec(gc, ns)],
        out_specs=[spec(*s) for s, _ in outs],
        out_shape=[jax.ShapeDtypeStruct((ng,) + s, dt) for s, dt in outs],
        compiler_params=_params(("parallel",), 32),
        name="s5_params",
    )(lam_re.reshape(ng, 1, ns), lam_im.reshape(ng, 1, ns), log_dt.reshape(ng, 1, 1),
      jnp.swapaxes(b_re, 1, 2), jnp.swapaxes(b_im, 1, 2), c_re, c_im)


def _s5_body(x_ref, toep_ref, wsr_ref, wsi_ref, wor_ref, woi_ref, a16r_ref, a16i_ref, d_ref,
             o_ref, sre_ref, sim_ref, xre_ref, xim_ref):
    rows = x_ref.shape[1]
    nchunk = rows // 8
    x0 = x_ref[0]
    x1 = x_ref[1]
    sre_ref[...] = _dot(x0, wsr_ref[0]) + _dot(x1, wsr_ref[1])
    sim_ref[...] = _dot(x0, wsi_ref[0]) + _dot(x1, wsi_ref[1])
    ar = a16r_ref[...]
    ai = a16i_ref[...]

    def step(c, carry):
        xr, xi = carry
        off = pl.multiple_of(c * 8, 8)
        xre_ref[pl.ds(off, 8), :] = xr
        xim_ref[pl.ds(off, 8), :] = xi
        sr = sre_ref[pl.ds(off, 8), :]
        si = sim_ref[pl.ds(off, 8), :]
        return ar * xr - ai * xi + sr, ar * xi + ai * xr + si

    zero = jnp.zeros((8, LANES), F32)
    lax.fori_loop(0, nchunk, step, (zero, zero))
    xr = xre_ref[...].astype(BF16)
    xi = xim_ref[...].astype(BF16)
    for q, xq in enumerate((x0, x1)):
        y = (_dot(xq, toep_ref[q]) + _dot(xr, wor_ref[q]) + _dot(xi, woi_ref[q])
             + xq.astype(F32) * d_ref[q])
        o_ref[q] = _gelu(y).astype(o_ref.dtype)


def s5_apply(xg, toep, wsr, wsi, wor, woi, a16r, a16i, dflat):
    ng, rows, cols = xg.shape

    def spec(*dims):
        return pl.BlockSpec((2,) + dims, lambda i: (i,) + (0,) * len(dims))

    return pl.pallas_call(
        _s5_body,
        grid=(ng // 2,),
        in_specs=[spec(rows, cols), spec(cols, cols), spec(cols, LANES), spec(cols, LANES),
                  spec(LANES, cols), spec(LANES, cols),
                  pl.BlockSpec((None, 8, LANES), lambda i: (i, 0, 0)),
                  pl.BlockSpec((None, 8, LANES), lambda i: (i, 0, 0)),
                  spec(1, cols)],
        out_specs=spec(rows, cols),
        out_shape=jax.ShapeDtypeStruct((ng, rows, cols), BF16),
        scratch_shapes=[pltpu.VMEM((rows, LANES), F32)] * 4,
        compiler_params=_params(("parallel",), 32),
        name="s5_apply",
    )(xg, toep, wsr, wsi, wor, woi, a16r, a16i, dflat)


def s5_mixer(u3, lam_re, lam_im, log_dt, b_re, b_im, c_re, c_im, d_skip):
    bsz, length, width = u3.shape
    ng, ns = lam_re.shape
    gc = SSM_GROUP
    nc = SSM_CHUNK
    kst, etr, eti, car, cai, a16r, a16i = s5_params(lam_re, lam_im, log_dt, b_re, b_im, c_re, c_im)

    k4 = kst.reshape(ng, nc, gc, gc)
    s_idx = jnp.arange(nc)[:, None]
    t_idx = jnp.arange(nc)[None, :]
    lag = t_idx - s_idx
    toep = jnp.where((lag >= 0)[None, :, :, None, None], k4[:, jnp.clip(lag, 0, nc - 1)], 0.0)
    toep = toep.transpose(0, 1, 3, 2, 4).reshape(ng, nc * gc, nc * gc).astype(BF16)
    odd = (jnp.arange(ng) % 2 == 1)[:, None, None]

    def lanes_by_parity(w):
        z = jnp.zeros_like(w)
        return jnp.where(odd, jnp.concatenate([z, w], -1), jnp.concatenate([w, z], -1))

    def flip_lag(e):
        return e.reshape(ng, nc, gc, ns)[:, ::-1].reshape(ng, nc * gc, ns)

    wsr = lanes_by_parity(flip_lag(etr)).astype(BF16)
    wsi = lanes_by_parity(flip_lag(eti)).astype(BF16)
    wor = jnp.swapaxes(lanes_by_parity(car), 1, 2).astype(BF16)
    woi = jnp.swapaxes(lanes_by_parity(-cai), 1, 2).astype(BF16)

    def pair_lanes(a):
        return a.reshape(ng // 2, 2, 8, ns).transpose(0, 2, 1, 3).reshape(ng // 2, 8, 2 * ns)

    dflat = jnp.tile(d_skip.reshape(ng, 1, gc), (1, 1, nc))

    nchunk = length // nc
    x = u3.reshape(bsz, nchunk, nc, ng, gc).transpose(3, 1, 0, 2, 4)
    x = jnp.pad(x, ((0, 0), (0, 0), (0, 8 - bsz), (0, 0), (0, 0))).reshape(ng, nchunk * 8, nc * gc)
    y = s5_apply(x, toep, wsr, wsi, wor, woi, pair_lanes(a16r), pair_lanes(a16i), dflat)
    y = y.reshape(ng, nchunk, 8, nc, gc)[:, :, :bsz].transpose(2, 1, 3, 0, 4)
    return y.reshape(bsz * length, width)


def _topk_rows(s, k):
    tops = []
    cur = s
    rank = jnp.full_like(s, float(PEER_KEYS - 1))
    for i in range(k):
        m = jnp.max(cur, axis=0, keepdims=True)
        tops.append(m)
        hit = cur >= m
        rank = jnp.where(hit, float(i), rank)
        cur = jnp.where(hit, NEG_INF, cur)
    return tops, rank


def _peer_route_body(q_ref, keys_ref, cnt_ref, e1_ref, r2_ref, e2_ref):
    nk = PEER_KEYS
    k = PEER_TOPK
    tm = q_ref.shape[0]
    wide, tall = 4, 3
    assert k == 16 and (wide + 1) * (tall + 1) > k and 2 * 9 > k
    a_idx = lax.broadcasted_iota(jnp.int32, (k, 1), 0)
    for h in range(PEER_HEADS):
        s1_all = _dot_nt_acc(keys_ref[h, 0], q_ref[:, (2 * h) * nk:(2 * h + 1) * nk])
        s2_all = _dot_nt_acc(keys_ref[h, 1], q_ref[:, (2 * h + 1) * nk:(2 * h + 2) * nk])
        for c in range(tm // LANES):
            ts = slice(c * LANES, (c + 1) * LANES)
            s1 = s1_all[:, ts]
            s2 = s2_all[:, ts]
            t1, _ = _topk_rows(s1, k)
            t2, rank2 = _topk_rows(s2, k)
            top1 = jnp.concatenate(t1, axis=0)
            top2 = jnp.concatenate(t2, axis=0)
            row_blocks = [t1[0] + top2] + [t1[a] + top2[0:8] for a in range(1, wide)]
            col_blocks = [jnp.where(a_idx >= wide, top1 + t2[b], NEG_INF) for b in range(tall)]
            cand = jnp.concatenate(row_blocks + col_blocks, axis=0)
            best = t1[0] + t2[0]
            zsum = jnp.zeros_like(best)
            cur = cand
            tau = best
            for _ in range(k):
                tau = jnp.max(cur, axis=0, keepdims=True)
                zsum = zsum + jnp.exp(tau - best)
                cur = jnp.where(cur >= tau, NEG_INF, cur)
            kept_rows = [jnp.sum(jnp.where(blk >= tau, 1.0, 0.0), axis=0, keepdims=True)
                         for blk in row_blocks]
            kept_cols = jnp.where(col_blocks[0] >= tau, 1.0, 0.0)
            for b in range(1, tall):
                kept_cols = kept_cols + jnp.where(col_blocks[b] >= tau, 1.0, 0.0)
            cnt = jnp.zeros_like(s1)
            for a in range(k):
                kept_a = kept_rows[a] if a < wide else kept_cols[a:a + 1]
                cnt = jnp.where(s1 == t1[a], kept_a, cnt)
            cnt_ref[h, :, ts] = cnt
            r2_ref[h, :, ts] = rank2.astype(BF16)
            e1_ref[h, :, ts] = jnp.exp(s1 - t1[0]) / zsum
            e2_ref[h, :, ts] = jnp.exp(s2 - t2[0]).astype(BF16)


def peer_route(q, keys):
    t = q.shape[0]
    tm = _pick_tile(t, (256, 128))
    big = pl.BlockSpec((PEER_HEADS, PEER_KEYS, tm), lambda i: (0, 0, i))
    sds = jax.ShapeDtypeStruct((PEER_HEADS, PEER_KEYS, t), F32)
    sds_bf = jax.ShapeDtypeStruct((PEER_HEADS, PEER_KEYS, t), BF16)
    return pl.pallas_call(
        _peer_route_body,
        grid=(t // tm,),
        in_specs=[pl.BlockSpec((tm, q.shape[1]), lambda i: (i, 0)),
                  pl.BlockSpec(keys.shape, lambda i: (0, 0, 0, 0))],
        out_specs=[big, big, big, big],
        out_shape=[sds, sds, sds_bf, sds_bf],
        compiler_params=_params(("parallel",), 48),
        name="peer_route",
    )(q, keys)


PEER_SLAB = 256


PEER_TILE = 512
SUBLANES = 8


def _peer_dense_body(hnt_ref, u_ref, v_ref, r2_ref, e2_ref, cnt_ref, e1_ref, o_ref, *, te):
    nk = PEER_KEYS
    rows_per_tile = te // nk
    rows_per_slab = PEER_SLAB // nk
    nslab = te // PEER_SLAB
    j = pl.program_id(1)
    base = (j % (SUBLANES // rows_per_tile)) * rows_per_tile

    @pl.when(j == 0)
    def _():
        o_ref[...] = jnp.zeros_like(o_ref)

    def gates(s, hold):
        out = []
        zero = jnp.zeros((), BF16)
        for r in range(rows_per_slab):
            row = base + s * rows_per_slab + r
            gate = None
            for h in range(PEER_HEADS):
                cnt_row = cnt_ref[h, pl.ds(row, 1), :]
                if hold is not None:
                    cnt_row = cnt_row + hold
                keep = r2_ref[h] < cnt_row.astype(BF16)
                term = jnp.where(keep, e2_ref[h] * e1_ref[h, pl.ds(row, 1), :].astype(BF16), zero)
                gate = term if gate is None else gate + term
            out.append(gate.astype(F32))
        return out

    hnt = hnt_ref[...]
    acts = [_dot(u_ref[s * PEER_SLAB:(s + 1) * PEER_SLAB, :], hnt) for s in range(nslab)]
    total = None
    hold = None
    for s in range(nslab):
        gate_blocks = gates(s, hold)
        parts = [gate_blocks[r] * _gelu(acts[s][r * nk:(r + 1) * nk, :])
                 for r in range(rows_per_slab)]
        w = jnp.concatenate(parts, axis=0).T.astype(BF16)
        c = _dot(w, v_ref[s * PEER_SLAB:(s + 1) * PEER_SLAB, :])
        hold = 0.0 * parts[-1][0:1, :]
        total = c if total is None else total + c
    o_ref[...] += total


def peer_dense(hn_t, u_all, v_all, layer, rank2, e2, cnt, e1):
    d, t = hn_t.shape
    ne = u_all.shape[1]
    tm = _pick_tile(t, (512, 256, 128))
    te = min(PEER_TILE, ne)
    rows_per_tile = te // PEER_KEYS
    tiles_per_block = SUBLANES // rows_per_tile
    big = pl.BlockSpec((PEER_HEADS, PEER_KEYS, tm), lambda i, j: (0, 0, i))
    rows = pl.BlockSpec((PEER_HEADS, SUBLANES, tm), lambda i, j: (0, j // tiles_per_block, i))
    table = pl.BlockSpec((None, te, d), lambda i, j: (layer, j, 0))
    return pl.pallas_call(
        functools.partial(_peer_dense_body, te=te),
        grid=(t // tm, ne // te),
        in_specs=[pl.BlockSpec((d, tm), lambda i, j: (0, i)), table, table, big, big, rows, rows],
        out_specs=pl.BlockSpec((tm, d), lambda i, j: (i, 0)),
        out_shape=jax.ShapeDtypeStruct((t, d), F32),
        compiler_params=_params(("parallel", "arbitrary"), 60),
        name="peer_dense",
    )(hn_t, u_all, v_all, rank2, e2, cnt, e1)


def peer_ffn(h, gain, w_q_bf, keys, u_all, v_all, layer):
    hn, hn_t = rms_norm_bf16(h, gain, with_transpose=True)
    q = matmul(hn, w_q_bf)
    cnt, e1, rank2, e2 = peer_route(q, keys)
    return peer_dense(hn_t, u_all, v_all, layer, rank2, e2, cnt, e1)


def _final_norm_body(x_ref, dx_ref, g_ref, o_ref):
    x = x_ref[0] + dx_ref[0]
    inv = lax.rsqrt(jnp.mean(x * x, axis=-1, keepdims=True) + NORM_EPS)
    o_ref[...] = (x * inv) * g_ref[...]


def final_norm(h3, delta3, gain, first, count):
    bsz, _, d = h3.shape
    tm = _pick_tile(count, (256, 128, 64, 32, 16, 8))
    assert first % 8 == 0
    rows = pl.BlockSpec((pl.Element(1), pl.Element(tm), pl.Element(d)),
                        lambda b, i: (b, pl.multiple_of(first + i * tm, 8), 0))
    return pl.pallas_call(
        _final_norm_body,
        grid=(bsz, count // tm),
        in_specs=[rows, rows, pl.BlockSpec((1, d), lambda b, i: (0, 0))],
        out_specs=pl.BlockSpec((None, tm, d), lambda b, i: (b, i, 0)),
        out_shape=jax.ShapeDtypeStruct((bsz, count, d), F32),
        compiler_params=_params(("parallel", "parallel"), 32),
        name="final_norm",
    )(h3, delta3, gain.reshape(1, d))


def _even_layer_weights(w_in, shift_mu, w2, a2, width):
    pool_w = w_in.shape[1] - shift_mu.shape[0]
    c0 = pool_w + 3 * width
    c1 = c0 + RWKV_W_RANK
    c2 = c1 + RWKV_A_RANK
    d = w_in.shape[0]
    zw = jnp.zeros((d, LANES - RWKV_W_RANK), w_in.dtype)
    za = jnp.zeros((d, LANES - RWKV_A_RANK), w_in.dtype)
    w_main = w_in[:, :c0].astype(BF16)
    w_lora = jnp.concatenate([w_in[:, c0:c1], zw, w_in[:, c1:c2], za, w_in[:, c2:]],
                             axis=1).astype(BF16)
    s0 = 3 * width
    s1 = s0 + RWKV_W_RANK
    s2 = s1 + RWKV_A_RANK
    mu_main = jnp.concatenate([jnp.zeros((pool_w,), F32), shift_mu[:s0]]).reshape(1, -1)
    mu_lora = jnp.concatenate([
        shift_mu[s0:s1], jnp.zeros((LANES - RWKV_W_RANK,), F32), shift_mu[s1:s2],
        jnp.zeros((LANES - RWKV_A_RANK,), F32), shift_mu[s2:]]).reshape(1, -1)
    w2p = jnp.pad(w2, ((0, LANES - RWKV_W_RANK), (0, 0)))
    a2p = jnp.pad(a2, ((0, LANES - RWKV_A_RANK), (0, 0)))
    return w_main, w_lora, mu_main, mu_lora, w2p, a2p, pool_w


def kernel(x, meta_tokens, mix_norm_gain, ffn_norm_gain, final_norm_gain, w_in_even, pool_w, pool_scale, shift_mu, rwkv_w0, rwkv_w2, rwkv_a0, rwkv_a2, rwkv_g2, rwkv_k_k, rwkv_k_a, rwkv_r_k, rwkv_gn_gain, rwkv_gn_bias, w_out_even, w_in_odd, ssm_lam_re, ssm_lam_im, ssm_log_dt, ssm_b_re, ssm_b_im, ssm_c_re, ssm_c_im, ssm_d, w_glu, peer_w_q, peer_sub_keys, peer_u, peer_v):
    bsz, seq, d = x.shape
    depth = mix_norm_gain.shape[0]
    real = N_META + seq
    length = -(-real // SEQ_ALIGN) * SEQ_ALIGN
    meta = jnp.broadcast_to(meta_tokens[None].astype(x.dtype), (bsz, N_META, d))
    h = jnp.concatenate([meta, x, jnp.zeros((bsz, length - real, d), x.dtype)], axis=1)
    h = h.reshape(bsz * length, d)
    peer_u_bf = peer_u.astype(BF16)
    peer_v_bf = peer_v.astype(BF16)

    pending = None
    for layer in range(depth):
        if pending is None:
            hn = rms_norm_bf16(h, mix_norm_gain[layer])
        else:
            hn, h = rms_norm_bf16(h, mix_norm_gain[layer], delta=pending)
        if layer % 2 == 0:
            i = layer // 2
            width = rwkv_w0.shape[-1]
            w_main, w_lora, mu_main, mu_lora, w2p, a2p, pool_cols = _even_layer_weights(
                w_in_even[i], shift_mu[i], rwkv_w2[i], rwkv_a2[i], width)
            z3 = matmul(hn, w_main).reshape(bsz, length, -1)
            zl3 = matmul(hn, w_lora).reshape(bsz, length, -1)
            y_pool = pool_mixer(z3, pool_w[i], pool_scale[i])
            ob = pool_cols // 512
            wb = width // 512
            r, k, v, kap, b, lw, g = rwkv_prep(
                z3, zl3, (ob, ob + wb, ob + 2 * wb), mu_main, mu_lora, rwkv_w0[i], w2p,
                rwkv_a0[i], a2p, rwkv_g2[i], rwkv_k_k[i], rwkv_k_a[i])
            y_rwkv = rwkv_scan(r, k, v, kap, b, lw, g, rwkv_r_k[i], rwkv_gn_gain[i],
                               rwkv_gn_bias[i])
            h = proj2_residual(y_pool.reshape(bsz * length, -1), y_rwkv.reshape(bsz * length, -1),
                               w_out_even[i].astype(BF16), h)
        else:
            j = layer // 2
            u = matmul(hn, w_in_odd[j].astype(BF16), out_dtype=BF16)
            y = s5_mixer(u.reshape(bsz, length, -1), ssm_lam_re[j], ssm_lam_im[j], ssm_log_dt[j],
                         ssm_b_re[j], ssm_b_im[j], ssm_c_re[j], ssm_c_im[j], ssm_d[j])
            h = glu_residual(y, w_glu[j].astype(BF16), h)
        pending = peer_ffn(h, ffn_norm_gain[layer], peer_w_q[layer].astype(BF16),
                           peer_sub_keys[layer], peer_u_bf, peer_v_bf, layer)
    return final_norm(h.reshape(bsz, length, d), pending.reshape(bsz, length, d), final_norm_gain,
                      N_META, seq)
```

```python
import functools

import jax
import jax.numpy as jnp
from jax import lax
from jax.experimental import pallas as pl
from jax.experimental.pallas import tpu as pltpu

F32 = jnp.float32
BF16 = jnp.bfloat16

NORM_EPS = 1e-6
N_META = 16
POOL_WINDOWS = (2, 4, 8, 16)
RWKV_HEAD = 64
RWKV_W_RANK = 96
RWKV_A_RANK = 96
RWKV_G_RANK = 256
RWKV_GN_EPS = 64e-5
RWKV_CHUNK = 64
SSM_GROUP = 16
SSM_STATE = 64
SSM_CHUNK = 16
PEER_KEYS = 128
PEER_HEADS = 8
PEER_TOPK = 16
LANES = 128
SEQ_ALIGN = 128

NEG_INF = float("-inf")


def _params(sem, vmem_mb):
    return pltpu.CompilerParams(dimension_semantics=sem, vmem_limit_bytes=vmem_mb << 20)


def _split2(x):
    hi = x.astype(BF16)
    lo = (x - hi.astype(F32)).astype(BF16)
    return hi, lo


def _dot(a, b):
    return jnp.dot(a, b, preferred_element_type=F32)


def _dot_nt(a, b):
    return lax.dot_general(a, b, (((1,), (1,)), ((), ())), preferred_element_type=F32)


def _dot_acc(a, b):
    ah, al = _split2(a)
    bh, bl = _split2(b)
    return _dot(ah, bh) + _dot(al, bh) + _dot(ah, bl)


def _dot_nt_acc(a, b):
    ah, al = _split2(a)
    bh, bl = _split2(b)
    return _dot_nt(ah, bh) + _dot_nt(al, bh) + _dot_nt(ah, bl)


def _gelu(x):
    return 0.5 * x * (1.0 + lax.erf(x * 0.7071067811865476))


def _sigmoid(x):
    return 1.0 / (1.0 + jnp.exp(-x))


def _pick_tile(n, prefs):
    for t in prefs:
        if n % t == 0:
            return t
    return n


def _rms_norm_body(*refs, with_transpose, with_delta):
    refs = list(refs)
    x = refs.pop(0)[...]
    if with_delta:
        x = x + refs.pop(0)[...]
    g_ref, o_ref = refs.pop(0), refs.pop(0)
    inv = lax.rsqrt(jnp.mean(x * x, axis=-1, keepdims=True) + NORM_EPS)
    xn = (x * inv) * g_ref[...]
    o_ref[...] = xn.astype(BF16)
    if with_transpose:
        refs.pop(0)[...] = xn.T.astype(BF16)
    if with_delta:
        refs.pop(0)[...] = x


def rms_norm_bf16(x, gain, *, delta=None, with_transpose=False):
    t, d = x.shape
    tm = _pick_tile(t, (256, 128, 64, 32, 16, 8))
    row = pl.BlockSpec((tm, d), lambda i: (i, 0))
    args = [x] + ([delta] if delta is not None else []) + [gain.reshape(1, d)]
    in_specs = [row] * (len(args) - 1) + [pl.BlockSpec((1, d), lambda i: (0, 0))]
    out_shape = [jax.ShapeDtypeStruct((t, d), BF16)]
    out_specs = [row]
    if with_transpose:
        out_shape.append(jax.ShapeDtypeStruct((d, t), BF16))
        out_specs.append(pl.BlockSpec((d, tm), lambda i: (0, i)))
    if delta is not None:
        out_shape.append(jax.ShapeDtypeStruct((t, d), F32))
        out_specs.append(row)
    res = pl.pallas_call(
        functools.partial(_rms_norm_body, with_transpose=with_transpose,
                          with_delta=delta is not None),
        grid=(t // tm,),
        in_specs=in_specs,
        out_specs=out_specs,
        out_shape=out_shape,
        compiler_params=_params(("parallel",), 48),
        name="rms_norm",
    )(*args)
    return res if len(res) > 1 else res[0]


def _row_tile(t):
    for cand in (1088, 1024, 768, 512, 384, 256, 128, 64, 32, 16):
        if t % cand == 0:
            return cand
    return t


def _matmul_body(a_ref, w_ref, o_ref):
    o_ref[...] = _dot(a_ref[...], w_ref[...].astype(BF16)).astype(o_ref.dtype)


def matmul(a, w_all, layer, *, n=None, out_dtype=F32):
    t, k = a.shape
    n = w_all.shape[2] if n is None else n
    tm = _row_tile(t)
    tn = _pick_tile(n, (512, 256, 128))
    return pl.pallas_call(
        _matmul_body,
        grid=(t // tm, n // tn),
        in_specs=[pl.BlockSpec((tm, k), lambda i, j: (i, 0)),
                  pl.BlockSpec((None, k, tn), lambda i, j: (layer, 0, j))],
        out_specs=pl.BlockSpec((tm, tn), lambda i, j: (i, j)),
        out_shape=jax.ShapeDtypeStruct((t, n), out_dtype),
        compiler_params=_params(("parallel", "arbitrary"), 56),
        name="matmul",
    )(a, w_all)


def _proj2_body(a1_ref, a2_ref, w1_ref, w2_ref, h_ref, o_ref):
    o_ref[...] = (h_ref[...] + _dot(a1_ref[...], w1_ref[...].astype(BF16))
                  + _dot(a2_ref[...], w2_ref[...].astype(BF16)))


def proj2_residual(a1, a2, w_all, layer, h):
    t, k1 = a1.shape
    k2 = a2.shape[1]
    n = w_all.shape[2]
    assert k1 == k2
    tm = _row_tile(t)
    tn = _pick_tile(n, (512, 256, 128))
    return pl.pallas_call(
        _proj2_body,
        grid=(t // tm, n // tn),
        in_specs=[
            pl.BlockSpec((tm, k1), lambda i, j: (i, 0)),
            pl.BlockSpec((tm, k2), lambda i, j: (i, 0)),
            pl.BlockSpec((None, k1, tn), lambda i, j: (layer, 0, j)),
            pl.BlockSpec((None, k2, tn), lambda i, j: (layer, 1, j)),
            pl.BlockSpec((tm, tn), lambda i, j: (i, j)),
        ],
        out_specs=pl.BlockSpec((tm, tn), lambda i, j: (i, j)),
        out_shape=jax.ShapeDtypeStruct((t, n), F32),
        compiler_params=_params(("parallel", "arbitrary"), 56),
        name="proj2_residual",
    )(a1, a2, w_all, w_all, h)


def _glu_body(y_ref, wa_ref, wb_ref, h_ref, o_ref):
    y = y_ref[...]
    ga = _dot(y, wa_ref[...].astype(BF16))
    gb = _dot(y, wb_ref[...].astype(BF16))
    o_ref[...] = h_ref[...] + ga * _sigmoid(gb)


def glu_residual(y, w_all, layer, h):
    t, k = y.shape
    n = w_all.shape[2] // 2
    tm = _row_tile(t)
    tn = _pick_tile(n, (512, 256, 128))
    nb = n // tn
    return pl.pallas_call(
        _glu_body,
        grid=(t // tm, nb),
        in_specs=[
            pl.BlockSpec((tm, k), lambda i, j: (i, 0)),
            pl.BlockSpec((None, k, tn), lambda i, j: (layer, 0, j)),
            pl.BlockSpec((None, k, tn), lambda i, j: (layer, 0, j + nb)),
            pl.BlockSpec((tm, tn), lambda i, j: (i, j)),
        ],
        out_specs=pl.BlockSpec((tm, tn), lambda i, j: (i, j)),
        out_shape=jax.ShapeDtypeStruct((t, n), F32),
        compiler_params=_params(("parallel", "arbitrary"), 56),
        name="glu_residual",
    )(y, w_all, w_all, h)


def _pool_body(z_ref, w_ref, s_ref, o_ref):
    grp = pl.program_id(1)
    z = z_ref[...]
    length = z.shape[0]
    row = lax.broadcasted_iota(jnp.int32, (length, 1), 0)
    posf = (row + 1).astype(F32)

    def shifted(x, k):
        return jnp.where(row >= k, pltpu.roll(x, k, axis=0), 0.0)

    for gi, win in enumerate(POOL_WINDOWS):

        @pl.when(grp == gi)
        def _(win=win):
            acc = z
            span = 1
            while span < win:
                acc = acc + shifted(acc, span)
                span *= 2
            cnt = jnp.minimum(posf, float(win))
            m = acc / cnt - z
            y = _dot(m.astype(BF16), w_ref[...])
            o_ref[...] = (y * s_ref[...]).astype(o_ref.dtype)


def pool_mixer(z3, pool_w, pool_scale):
    bsz, length, _ = z3.shape
    ng, c, _ = pool_w.shape
    return pl.pallas_call(
        _pool_body,
        grid=(bsz, ng),
        in_specs=[
            pl.BlockSpec((None, length, c), lambda b, g: (b, 0, g)),
            pl.BlockSpec((None, c, c), lambda b, g: (g, 0, 0)),
            pl.BlockSpec((1, c), lambda b, g: (0, g)),
        ],
        out_specs=pl.BlockSpec((None, length, c), lambda b, g: (b, 0, g)),
        out_shape=jax.ShapeDtypeStruct((bsz, length, ng * c), BF16),
        compiler_params=_params(("parallel", "arbitrary"), 48),
        name="pool_mixer",
    )(z3, pool_w.astype(BF16), pool_scale.reshape(1, ng * c))


def _head_sum_matrix(width, head):
    shift = head.bit_length() - 1
    r = lax.broadcasted_iota(jnp.int32, (width, width), 0) >> shift
    c = lax.broadcasted_iota(jnp.int32, (width, width), 1) >> shift
    return jnp.where(r == c, 1.0, 0.0).astype(BF16)


def _head_sums(x, ones_bd):
    hi, lo = _split2(x)
    return _dot(hi, ones_bd) + _dot(lo, ones_bd)


def _rwkv_prep_body(zr_ref, zk_ref, zv_ref, zl_ref, pr_ref, pk_ref, pv_ref, pl_ref,
                    mur_ref, muk_ref, muv_ref, mul_ref, w0_ref, w2_ref, a0_ref, a2_ref, g2_ref,
                    kk_ref, ka_ref,
                    r_out, k_out, v_out, kap_out, b_out, lw_out, g_out):
    first = pl.program_id(1) == 0
    tl = zr_ref.shape[0]
    row = lax.broadcasted_iota(jnp.int32, (tl, 1), 0)

    def mix(z_ref, p_ref, mu_ref):
        z = z_ref[...]
        prev_last = jnp.where(first, 0.0, p_ref[7:8, :])
        prev = jnp.where(row == 0, prev_last, pltpu.roll(z, 1, axis=0))
        return z + (prev - z) * mu_ref[...]

    r = mix(zr_ref, pr_ref, mur_ref)
    k = mix(zk_ref, pk_ref, muk_ref)
    v = mix(zv_ref, pv_ref, muv_ref)
    lora = mix(zl_ref, pl_ref, mul_ref)
    w_lr = lora[:, 0:LANES]
    a_lr = lora[:, LANES:2 * LANES]
    g_lr = lora[:, 2 * LANES:]

    x = w0_ref[...] + _dot_acc(jnp.tanh(w_lr), w2_ref[...])
    y = -x
    softplus = jnp.maximum(y, 0.0) + jnp.log(1.0 + jnp.exp(-jnp.abs(y)))
    lw = -jnp.exp(-softplus - 0.5)
    a = _sigmoid(a0_ref[...] + _dot_acc(a_lr, a2_ref[...]))
    g = _dot_acc(_sigmoid(g_lr), g2_ref[...])

    ones_bd = _head_sum_matrix(LANES, RWKV_HEAD)
    kk = k * kk_ref[...]
    kk2 = kk * kk
    width = k.shape[1]
    ss = jnp.concatenate(
        [_head_sums(kk2[:, c:c + LANES], ones_bd) for c in range(0, width, LANES)], axis=1)
    kap = kk / jnp.maximum(jnp.sqrt(ss), 1e-12)

    r_out[...] = r.astype(r_out.dtype)
    k_out[...] = (k * (1.0 + (a - 1.0) * ka_ref[...])).astype(k_out.dtype)
    v_out[...] = v.astype(v_out.dtype)
    kap_out[...] = kap.astype(kap_out.dtype)
    b_out[...] = (kap * a).astype(b_out.dtype)
    lw_out[...] = lw
    g_out[...] = g.astype(g_out.dtype)


def rwkv_prep(z3, zl3, off_blocks, mu, mu_l, w0, w2p, a0, a2p, g2, k_k, k_a):
    bsz, length, _ = z3.shape
    width = w0.shape[-1]
    cw = 512
    ncb = width // cw
    tl = _pick_tile(length, (352, 272, 192, 176, 136, 128, 64))
    ob_r, ob_k, ob_v = off_blocks
    ob_l = 0
    t8 = tl // 8

    def cur(ob, fixed=False):
        if fixed:
            return pl.BlockSpec((None, tl, cw), lambda b, i, c: (b, i, ob))
        return pl.BlockSpec((None, tl, cw), lambda b, i, c: (b, i, ob + c))

    def prev(ob, fixed=False):
        if fixed:
            return pl.BlockSpec((None, 8, cw), lambda b, i, c: (b, jnp.maximum(i * t8 - 1, 0), ob))
        return pl.BlockSpec((None, 8, cw), lambda b, i, c: (b, jnp.maximum(i * t8 - 1, 0), ob + c))

    def vec(ob, fixed=False):
        if fixed:
            return pl.BlockSpec((1, cw), lambda b, i, c: (0, ob))
        return pl.BlockSpec((1, cw), lambda b, i, c: (0, ob + c))

    colvec = pl.BlockSpec((1, cw), lambda b, i, c: (0, c))
    out_spec = pl.BlockSpec((None, tl, cw), lambda b, i, c: (b, i, c))
    out_dtypes = [BF16, BF16, BF16, BF16, BF16, F32, BF16]
    return pl.pallas_call(
        _rwkv_prep_body,
        grid=(bsz, length // tl, ncb),
        in_specs=[
            cur(ob_r), cur(ob_k), cur(ob_v), cur(ob_l, True),
            prev(ob_r), prev(ob_k), prev(ob_v), prev(ob_l, True),
            vec(ob_r), vec(ob_k), vec(ob_v), vec(ob_l, True),
            colvec,
            pl.BlockSpec((LANES, cw), lambda b, i, c: (0, c)),
            colvec,
            pl.BlockSpec((LANES, cw), lambda b, i, c: (0, c)),
            pl.BlockSpec((RWKV_G_RANK, cw), lambda b, i, c: (0, c)),
            colvec, colvec,
        ],
        out_specs=[out_spec] * 7,
        out_shape=[jax.ShapeDtypeStruct((bsz, length, width), dt) for dt in out_dtypes],
        compiler_params=_params(("parallel", "parallel", "arbitrary"), 56),
        name="rwkv_prep",
    )(z3, z3, z3, zl3, z3, z3, z3, zl3, mu, mu, mu, mu_l,
      w0.reshape(1, width), w2p, a0.reshape(1, width), a2p, g2,
      k_k.reshape(1, width), k_a.reshape(1, width))


def _rwkv_scan_body(r_ref, k_ref, v_ref, kap_ref, b_ref, lw_ref, g_ref, rk_ref, gain_ref, bias_ref,
                    o_ref, h_ref, *, pairs):
    cn = RWKV_CHUNK
    hd = RWKV_HEAD

    @pl.when(pl.program_id(2) == 0)
    def _():
        h_ref[...] = jnp.zeros_like(h_ref)

    lane = lax.broadcasted_iota(jnp.int32, (1, LANES), 1)
    m0 = lane < hd
    rowi = lax.broadcasted_iota(jnp.int32, (cn, LANES), 0)
    coli = lax.broadcasted_iota(jnp.int32, (cn, LANES), 1) & (hd - 1)
    strict = rowi > coli
    incl = rowi >= coli
    eye2 = jnp.where(rowi == coli, 1.0, 0.0)
    tri = jnp.where(lax.broadcasted_iota(jnp.int32, (cn, cn), 0)
                    >= lax.broadcasted_iota(jnp.int32, (cn, cn), 1), 1.0, 0.0).astype(BF16)
    br = lax.broadcasted_iota(jnp.int32, (LANES, LANES), 0) >> (hd.bit_length() - 1)
    bc = lax.broadcasted_iota(jnp.int32, (LANES, LANES), 1) >> (hd.bit_length() - 1)
    bd_mask = br == bc
    ones_bd = jnp.where(bd_mask, 1.0, 0.0).astype(BF16)

    def stk(x):
        return jnp.concatenate([jnp.where(m0, x, 0.0), jnp.where(m0, 0.0, x)], axis=0).astype(BF16)

    ps = range(pairs)
    sls = [slice(p * LANES, (p + 1) * LANES) for p in ps]
    r = [r_ref[:, s].astype(F32) for s in sls]
    k = [k_ref[:, s].astype(F32) for s in sls]
    v = [v_ref[:, s].astype(F32) for s in sls]
    kap = [kap_ref[:, s].astype(F32) for s in sls]
    b = [b_ref[:, s].astype(F32) for s in sls]
    lw = [lw_ref[:, s] for s in sls]
    hbd = [h_ref[p] for p in ps]

    lsp = [_split2(x) for x in lw]
    lp = [_dot(tri, a) + _dot(tri, c) for a, c in lsp]
    lpc = [x[cn - 1:cn, :] for x in lp]
    pinv = [jnp.exp(-x) for x in lp]
    pend = [jnp.exp(c - x) for c, x in zip(lpc, lp)]
    kt = [kap[p] * jnp.exp(lp[p] - lw[p]) for p in ps]
    rt = [r[p] * jnp.exp(lp[p]) for p in ps]
    kh = [k[p] * pinv[p] for p in ps]
    bh = [b[p] * pinv[p] for p in ps]
    khc = [k[p] * pend[p] for p in ps]
    bhc = [b[p] * pend[p] for p in ps]

    ktrt = [jnp.concatenate([kt[p], rt[p]], axis=0).astype(BF16) for p in ps]
    ab_all = [_dot_nt(ktrt[p], stk(bh[p])) for p in ps]
    ak_all = [_dot_nt(ktrt[p], stk(kh[p])) for p in ps]
    n_mat = [jnp.where(strict, x[:cn], 0.0) for x in ab_all]
    a_k = [jnp.where(strict, x[:cn], 0.0) for x in ak_all]
    a_rb = [jnp.where(incl, x[cn:], 0.0) for x in ab_all]
    a_rk = [jnp.where(incl, x[cn:], 0.0) for x in ak_all]

    t_mat = [eye2 - x for x in n_mat]
    pw = n_mat
    span = 1
    while 2 * span < cn:
        pw = [_dot(x.astype(BF16), stk(x)) for x in pw]
        t_mat = [t + _dot(t.astype(BF16), stk(x)) for t, x in zip(t_mat, pw)]
        span *= 2
    t_bf = [t.astype(BF16) for t in t_mat]

    kbar = [_dot(t_bf[p], stk(kt[p])) for p in ps]
    akv = [_dot(a_k[p].astype(BF16), stk(v[p])) for p in ps]
    ubar = [_dot(t_bf[p], stk(akv[p])) for p in ps]
    hb = [x.astype(BF16) for x in hbd]
    u = [_dot(kbar[p].astype(BF16), hb[p]) + ubar[p] for p in ps]
    y = [_dot(rt[p].astype(BF16), hb[p]) + _dot(a_rk[p].astype(BF16), stk(v[p]))
         - _dot(a_rb[p].astype(BF16), stk(u[p])) for p in ps]

    kb_t = [jnp.concatenate([khc[p], -bhc[p]], axis=0).T.astype(BF16) for p in ps]
    vu = [jnp.concatenate([v[p], u[p]], axis=0).astype(BF16) for p in ps]
    g_mat = [_dot(kb_t[p], vu[p]) for p in ps]
    pc_col = [jnp.exp(jnp.broadcast_to(x, (LANES, LANES)).T) for x in lpc]
    for p in ps:
        h_ref[p] = jnp.where(bd_mask, pc_col[p] * hbd[p] + g_mat[p], 0.0)

    mean = [_head_sums(x, ones_bd) * (1.0 / hd) for x in y]
    yc = [y[p] - mean[p] for p in ps]
    var = [_head_sums(x * x, ones_bd) * (1.0 / hd) for x in yc]
    bonus = [_head_sums(r[p] * k[p] * rk_ref[:, sls[p]], ones_bd) * v[p] for p in ps]
    for p in ps:
        yn = yc[p] * lax.rsqrt(var[p] + RWKV_GN_EPS) * gain_ref[:, sls[p]] + bias_ref[:, sls[p]]
        o_ref[:, sls[p]] = ((yn + bonus[p]) * g_ref[:, sls[p]].astype(F32)).astype(o_ref.dtype)


def rwkv_scan(r, k, v, kap, b, lw, g, r_k, gn_gain, gn_bias):
    bsz, length, width = r.shape
    cw = _pick_tile(width, (2048, 1024, 512))
    pairs = cw // LANES
    seq = pl.BlockSpec((None, RWKV_CHUNK, cw), lambda bi, c, t: (bi, t, c))
    vec = pl.BlockSpec((1, cw), lambda bi, c, t: (0, c))
    return pl.pallas_call(
        functools.partial(_rwkv_scan_body, pairs=pairs),
        grid=(bsz, width // cw, length // RWKV_CHUNK),
        in_specs=[seq] * 7 + [vec] * 3,
        out_specs=seq,
        out_shape=jax.ShapeDtypeStruct((bsz, length, width), BF16),
        scratch_shapes=[pltpu.VMEM((pairs, LANES, LANES), F32)],
        compiler_params=_params(("parallel", "parallel", "arbitrary"), 48),
        name="rwkv_scan",
    )(r, k, v, kap, b, lw, g, r_k.reshape(1, width), gn_gain.reshape(1, width),
      gn_bias.reshape(1, width))


def _s5_param_body(lr_ref, li_ref, ldt_ref, btr_ref, bti_ref, cr_ref, ci_ref,
                   kst_ref, etr_ref, eti_ref, car_ref, cai_ref, a16r_ref, a16i_ref, *, groups):
    nc = SSM_CHUNK
    npow = 24
    tau = lax.broadcasted_iota(jnp.int32, (npow, SSM_STATE), 0).astype(F32)
    for gidx in range(groups):
        lr = lr_ref[gidx]
        li = li_ref[gidx]
        dt = jnp.exp(ldt_ref[gidx])
        mag = jnp.exp(tau * (lr * dt))
        ang = tau * (li * dt)
        pr = mag * jnp.cos(ang)
        pi = mag * jnp.sin(ang)
        ar = pr[1:2]
        ai = pi[1:2]
        den = lr * lr + li * li
        fr = ((ar - 1.0) * lr + ai * li) / den
        fi = (ai * lr - (ar - 1.0) * li) / den
        btr = btr_ref[gidx]
        bti = bti_ref[gidx]
        bbr = fr * btr - fi * bti
        bbi = fr * bti + fi * btr
        cr = cr_ref[gidx]
        ci = ci_ref[gidx]
        etr = jnp.concatenate([pr[s:s + 1] * bbr - pi[s:s + 1] * bbi for s in range(nc)], axis=0)
        eti = jnp.concatenate([pr[s:s + 1] * bbi + pi[s:s + 1] * bbr for s in range(nc)], axis=0)
        car = jnp.concatenate([pr[s:s + 1] * cr - pi[s:s + 1] * ci for s in range(1, nc + 1)], axis=0)
        cai = jnp.concatenate([pr[s:s + 1] * ci + pi[s:s + 1] * cr for s in range(1, nc + 1)], axis=0)
        kst_ref[gidx] = _dot_nt_acc(etr, cr) - _dot_nt_acc(eti, ci)
        etr_ref[gidx] = etr
        eti_ref[gidx] = eti
        car_ref[gidx] = car
        cai_ref[gidx] = cai
        a16r_ref[gidx] = jnp.broadcast_to(pr[nc:nc + 1], (8, SSM_STATE))
        a16i_ref[gidx] = jnp.broadcast_to(pi[nc:nc + 1], (8, SSM_STATE))


def s5_params(lam_re, lam_im, log_dt, b_re, b_im, c_re, c_im):
    ng, ns = lam_re.shape
    gc = SSM_GROUP
    gb = 8
    rows = SSM_CHUNK * gc

    def spec(*dims):
        return pl.BlockSpec((gb,) + dims, lambda i: (i,) + (0,) * len(dims))

    outs = [((rows, gc), F32)] + [((rows, ns), F32)] * 4 + [((8, ns), F32)] * 2
    return pl.pallas_call(
        functools.partial(_s5_param_body, groups=gb),
        grid=(ng // gb,),
        in_specs=[spec(1, ns), spec(1, ns), spec(1, 1), spec(gc, ns), spec(gc, ns), spec(gc, ns),
                  spec(gc, ns)],
        out_specs=[spec(*s) for s, _ in outs],
        out_shape=[jax.ShapeDtypeStruct((ng,) + s, dt) for s, dt in outs],
        compiler_params=_params(("parallel",), 32),
        name="s5_params",
    )(lam_re.reshape(ng, 1, ns), lam_im.reshape(ng, 1, ns), log_dt.reshape(ng, 1, 1),
      jnp.swapaxes(b_re, 1, 2), jnp.swapaxes(b_im, 1, 2), c_re, c_im)


def _s5_body(x_ref, toep_ref, wsr_ref, wsi_ref, wor_ref, woi_ref, a16r_ref, a16i_ref, d_ref,
             o_ref, sre_ref, sim_ref, xre_ref, xim_ref):
    rows = x_ref.shape[1]
    nchunk = rows // 8
    x0 = x_ref[0]
    x1 = x_ref[1]
    sre_ref[...] = _dot(x0, wsr_ref[0]) + _dot(x1, wsr_ref[1])
    sim_ref[...] = _dot(x0, wsi_ref[0]) + _dot(x1, wsi_ref[1])
    ar = a16r_ref[...]
    ai = a16i_ref[...]

    def step(c, carry):
        xr, xi = carry
        off = pl.multiple_of(c * 8, 8)
        xre_ref[pl.ds(off, 8), :] = xr
        xim_ref[pl.ds(off, 8), :] = xi
        sr = sre_ref[pl.ds(off, 8), :]
        si = sim_ref[pl.ds(off, 8), :]
        return ar * xr - ai * xi + sr, ar * xi + ai * xr + si

    zero = jnp.zeros((8, LANES), F32)
    lax.fori_loop(0, nchunk, step, (zero, zero))
    xr = xre_ref[...].astype(BF16)
    xi = xim_ref[...].astype(BF16)
    for q, xq in enumerate((x0, x1)):
        y = (_dot(xq, toep_ref[q]) + _dot(xr, wor_ref[q]) + _dot(xi, woi_ref[q])
             + xq.astype(F32) * d_ref[q])
        o_ref[q] = _gelu(y).astype(o_ref.dtype)


def s5_apply(xg, toep, wsr, wsi, wor, woi, a16r, a16i, dflat):
    ng, rows, cols = xg.shape

    def spec(*dims):
        return pl.BlockSpec((2,) + dims, lambda i: (i,) + (0,) * len(dims))

    return pl.pallas_call(
        _s5_body,
        grid=(ng // 2,),
        in_specs=[spec(rows, cols), spec(cols, cols), spec(cols, LANES), spec(cols, LANES),
                  spec(LANES, cols), spec(LANES, cols),
                  pl.BlockSpec((None, 8, LANES), lambda i: (i, 0, 0)),
                  pl.BlockSpec((None, 8, LANES), lambda i: (i, 0, 0)),
                  spec(1, cols)],
        out_specs=spec(rows, cols),
        out_shape=jax.ShapeDtypeStruct((ng, rows, cols), BF16),
        scratch_shapes=[pltpu.VMEM((rows, LANES), F32)] * 4,
        compiler_params=_params(("parallel",), 32),
        name="s5_apply",
    )(xg, toep, wsr, wsi, wor, woi, a16r, a16i, dflat)


def s5_mixer(u3, lam_re, lam_im, log_dt, b_re, b_im, c_re, c_im, d_skip):
    bsz, length, width = u3.shape
    ng, ns = lam_re.shape
    gc = SSM_GROUP
    nc = SSM_CHUNK
    kst, etr, eti, car, cai, a16r, a16i = s5_params(lam_re, lam_im, log_dt, b_re, b_im, c_re, c_im)

    k4 = kst.reshape(ng, nc, gc, gc)
    s_idx = jnp.arange(nc)[:, None]
    t_idx = jnp.arange(nc)[None, :]
    lag = t_idx - s_idx
    toep = jnp.where((lag >= 0)[None, :, :, None, None], k4[:, jnp.clip(lag, 0, nc - 1)], 0.0)
    toep = toep.transpose(0, 1, 3, 2, 4).reshape(ng, nc * gc, nc * gc).astype(BF16)
    odd = (jnp.arange(ng) % 2 == 1)[:, None, None]

    def lanes_by_parity(w):
        z = jnp.zeros_like(w)
        return jnp.where(odd, jnp.concatenate([z, w], -1), jnp.concatenate([w, z], -1))

    def flip_lag(e):
        return e.reshape(ng, nc, gc, ns)[:, ::-1].reshape(ng, nc * gc, ns)

    wsr = lanes_by_parity(flip_lag(etr)).astype(BF16)
    wsi = lanes_by_parity(flip_lag(eti)).astype(BF16)
    wor = jnp.swapaxes(lanes_by_parity(car), 1, 2).astype(BF16)
    woi = jnp.swapaxes(lanes_by_parity(-cai), 1, 2).astype(BF16)

    def pair_lanes(a):
        return a.reshape(ng // 2, 2, 8, ns).transpose(0, 2, 1, 3).reshape(ng // 2, 8, 2 * ns)

    dflat = jnp.tile(d_skip.reshape(ng, 1, gc), (1, 1, nc))

    nchunk = length // nc
    x = u3.reshape(bsz, nchunk, nc, ng, gc).transpose(3, 1, 0, 2, 4)
    x = jnp.pad(x, ((0, 0), (0, 0), (0, 8 - bsz), (0, 0), (0, 0))).reshape(ng, nchunk * 8, nc * gc)
    y = s5_apply(x, toep, wsr, wsi, wor, woi, pair_lanes(a16r), pair_lanes(a16i), dflat)
    y = y.reshape(ng, nchunk, 8, nc, gc)[:, :, :bsz].transpose(2, 1, 3, 0, 4)
    return y.reshape(bsz * length, width)


def _topk_rows(s, k):
    tops = []
    cur = s
    rank = jnp.full_like(s, float(PEER_KEYS - 1))
    for i in range(k):
        m = jnp.max(cur, axis=0, keepdims=True)
        tops.append(m)
        hit = cur >= m
        rank = jnp.where(hit, float(i), rank)
        cur = jnp.where(hit, NEG_INF, cur)
    return tops, rank


def _peer_route_body(q_ref, keys_ref, cnt_ref, e1_ref, r2_ref, e2_ref):
    nk = PEER_KEYS
    k = PEER_TOPK
    tm = q_ref.shape[0]
    wide, tall = 4, 3
    assert k == 16 and (wide + 1) * (tall + 1) > k and 2 * 9 > k
    a_idx = lax.broadcasted_iota(jnp.int32, (k, 1), 0)
    for h in range(PEER_HEADS):
        s1_all = _dot_nt_acc(keys_ref[h, 0], q_ref[:, (2 * h) * nk:(2 * h + 1) * nk])
        s2_all = _dot_nt_acc(keys_ref[h, 1], q_ref[:, (2 * h + 1) * nk:(2 * h + 2) * nk])
        for c in range(tm // LANES):
            ts = slice(c * LANES, (c + 1) * LANES)
            s1 = s1_all[:, ts]
            s2 = s2_all[:, ts]
            t1, _ = _topk_rows(s1, k)
            t2, rank2 = _topk_rows(s2, k)
            top1 = jnp.concatenate(t1, axis=0)
            top2 = jnp.concatenate(t2, axis=0)
            row_blocks = [t1[0] + top2] + [t1[a] + top2[0:8] for a in range(1, wide)]
            col_blocks = [jnp.where(a_idx >= wide, top1 + t2[b], NEG_INF) for b in range(tall)]
            cand = jnp.concatenate(row_blocks + col_blocks, axis=0)
            best = t1[0] + t2[0]
            zsum = jnp.zeros_like(best)
            cur = cand
            tau = best
            for _ in range(k):
                tau = jnp.max(cur, axis=0, keepdims=True)
                zsum = zsum + jnp.exp(tau - best)
                cur = jnp.where(cur >= tau, NEG_INF, cur)
            kept_rows = [jnp.sum(jnp.where(blk >= tau, 1.0, 0.0), axis=0, keepdims=True)
                         for blk in row_blocks]
            kept_cols = jnp.where(col_blocks[0] >= tau, 1.0, 0.0)
            for b in range(1, tall):
                kept_cols = kept_cols + jnp.where(col_blocks[b] >= tau, 1.0, 0.0)
            cnt = jnp.zeros_like(s1)
            for a in range(k):
                kept_a = kept_rows[a] if a < wide else kept_cols[a:a + 1]
                cnt = jnp.where(s1 == t1[a], kept_a, cnt)
            cnt_ref[h, :, ts] = cnt
            r2_ref[h, :, ts] = rank2.astype(BF16)
            e1_ref[h, :, ts] = jnp.exp(s1 - t1[0]) / zsum
            e2_ref[h, :, ts] = jnp.exp(s2 - t2[0]).astype(BF16)


def peer_route(q, keys):
    t = q.shape[0]
    tm = _pick_tile(t, (256, 128))
    big = pl.BlockSpec((PEER_HEADS, PEER_KEYS, tm), lambda i: (0, 0, i))
    sds = jax.ShapeDtypeStruct((PEER_HEADS, PEER_KEYS, t), F32)
    sds_bf = jax.ShapeDtypeStruct((PEER_HEADS, PEER_KEYS, t), BF16)
    return pl.pallas_call(
        _peer_route_body,
        grid=(t // tm,),
        in_specs=[pl.BlockSpec((tm, q.shape[1]), lambda i: (i, 0)),
                  pl.BlockSpec(keys.shape, lambda i: (0, 0, 0, 0))],
        out_specs=[big, big, big, big],
        out_shape=[sds, sds, sds_bf, sds_bf],
        compiler_params=_params(("parallel",), 48),
        name="peer_route",
    )(q, keys)


PEER_SLAB = 256


PEER_TILE = 512
SUBLANES = 8


def _peer_dense_body(hnt_ref, u_ref, v_ref, r2_ref, e2_ref, cnt_ref, e1_ref, o_ref, *, te):
    nk = PEER_KEYS
    rows_per_tile = te // nk
    rows_per_slab = PEER_SLAB // nk
    nslab = te // PEER_SLAB
    j = pl.program_id(1)
    base = (j % (SUBLANES // rows_per_tile)) * rows_per_tile

    @pl.when(j == 0)
    def _():
        o_ref[...] = jnp.zeros_like(o_ref)

    def gates(s, hold):
        out = []
        zero = jnp.zeros((), BF16)
        for r in range(rows_per_slab):
            row = base + s * rows_per_slab + r
            gate = None
            for h in range(PEER_HEADS):
                cnt_row = cnt_ref[h, pl.ds(row, 1), :]
                if hold is not None:
                    cnt_row = cnt_row + hold
                keep = r2_ref[h] < cnt_row.astype(BF16)
                term = jnp.where(keep, e2_ref[h] * e1_ref[h, pl.ds(row, 1), :].astype(BF16), zero)
                gate = term if gate is None else gate + term
            out.append(gate.astype(F32))
        return out

    hnt = hnt_ref[...]
    acts = [_dot(u_ref[s * PEER_SLAB:(s + 1) * PEER_SLAB, :], hnt) for s in range(nslab)]
    total = None
    hold = None
    for s in range(nslab):
        gate_blocks = gates(s, hold)
        parts = [gate_blocks[r] * _gelu(acts[s][r * nk:(r + 1) * nk, :])
                 for r in range(rows_per_slab)]
        w = jnp.concatenate(parts, axis=0).T.astype(BF16)
        c = _dot(w, v_ref[s * PEER_SLAB:(s + 1) * PEER_SLAB, :])
        hold = 0.0 * parts[-1][0:1, :]
        total = c if total is None else total + c
    o_ref[...] += total


def peer_dense(hn_t, u_all, v_all, layer, rank2, e2, cnt, e1):
    d, t = hn_t.shape
    ne = u_all.shape[1]
    tm = _pick_tile(t, (512, 256, 128))
    te = min(PEER_TILE, ne)
    rows_per_tile = te // PEER_KEYS
    tiles_per_block = SUBLANES // rows_per_tile
    big = pl.BlockSpec((PEER_HEADS, PEER_KEYS, tm), lambda i, j: (0, 0, i))
    rows = pl.BlockSpec((PEER_HEADS, SUBLANES, tm), lambda i, j: (0, j // tiles_per_block, i))
    table = pl.BlockSpec((None, te, d), lambda i, j: (layer, j, 0))
    return pl.pallas_call(
        functools.partial(_peer_dense_body, te=te),
        grid=(t // tm, ne // te),
        in_specs=[pl.BlockSpec((d, tm), lambda i, j: (0, i)), table, table, big, big, rows, rows],
        out_specs=pl.BlockSpec((tm, d), lambda i, j: (i, 0)),
        out_shape=jax.ShapeDtypeStruct((t, d), F32),
        compiler_params=_params(("parallel", "arbitrary"), 60),
        name="peer_dense",
    )(hn_t, u_all, v_all, rank2, e2, cnt, e1)


def peer_ffn(h, gain, w_q_all, keys, u_all, v_all, layer):
    hn, hn_t = rms_norm_bf16(h, gain, with_transpose=True)
    q = matmul(hn, w_q_all, layer)
    cnt, e1, rank2, e2 = peer_route(q, keys)
    return peer_dense(hn_t, u_all, v_all, layer, rank2, e2, cnt, e1)


def _final_norm_body(x_ref, dx_ref, g_ref, o_ref):
    x = x_ref[0] + dx_ref[0]
    inv = lax.rsqrt(jnp.mean(x * x, axis=-1, keepdims=True) + NORM_EPS)
    o_ref[...] = (x * inv) * g_ref[...]


def final_norm(h3, delta3, gain, first, count):
    bsz, _, d = h3.shape
    tm = _pick_tile(count, (256, 128, 64, 32, 16, 8))
    assert first % 8 == 0
    rows = pl.BlockSpec((pl.Element(1), pl.Element(tm), pl.Element(d)),
                        lambda b, i: (b, pl.multiple_of(first + i * tm, 8), 0))
    return pl.pallas_call(
        _final_norm_body,
        grid=(bsz, count // tm),
        in_specs=[rows, rows, pl.BlockSpec((1, d), lambda b, i: (0, 0))],
        out_specs=pl.BlockSpec((None, tm, d), lambda b, i: (b, i, 0)),
        out_shape=jax.ShapeDtypeStruct((bsz, count, d), F32),
        compiler_params=_params(("parallel", "parallel"), 32),
        name="final_norm",
    )(h3, delta3, gain.reshape(1, d))


def _even_layer_weights(w_in, shift_mu, w2, a2, width):
    pool_w = w_in.shape[1] - shift_mu.shape[0]
    c0 = pool_w + 3 * width
    c1 = c0 + RWKV_W_RANK
    c2 = c1 + RWKV_A_RANK
    d = w_in.shape[0]
    zw = jnp.zeros((d, LANES - RWKV_W_RANK), w_in.dtype)
    za = jnp.zeros((d, LANES - RWKV_A_RANK), w_in.dtype)
    w_lora = jnp.concatenate([w_in[:, c0:c1], zw, w_in[:, c1:c2], za, w_in[:, c2:]], axis=1)[None]
    s0 = 3 * width
    s1 = s0 + RWKV_W_RANK
    s2 = s1 + RWKV_A_RANK
    mu_main = jnp.concatenate([jnp.zeros((pool_w,), F32), shift_mu[:s0]]).reshape(1, -1)
    mu_lora = jnp.concatenate([
        shift_mu[s0:s1], jnp.zeros((LANES - RWKV_W_RANK,), F32), shift_mu[s1:s2],
        jnp.zeros((LANES - RWKV_A_RANK,), F32), shift_mu[s2:]]).reshape(1, -1)
    w2p = jnp.pad(w2, ((0, LANES - RWKV_W_RANK), (0, 0)))
    a2p = jnp.pad(a2, ((0, LANES - RWKV_A_RANK), (0, 0)))
    return c0, w_lora, mu_main, mu_lora, w2p, a2p, pool_w


def kernel(x, meta_tokens, mix_norm_gain, ffn_norm_gain, final_norm_gain, w_in_even, pool_w, pool_scale, shift_mu, rwkv_w0, rwkv_w2, rwkv_a0, rwkv_a2, rwkv_g2, rwkv_k_k, rwkv_k_a, rwkv_r_k, rwkv_gn_gain, rwkv_gn_bias, w_out_even, w_in_odd, ssm_lam_re, ssm_lam_im, ssm_log_dt, ssm_b_re, ssm_b_im, ssm_c_re, ssm_c_im, ssm_d, w_glu, peer_w_q, peer_sub_keys, peer_u, peer_v):
    bsz, seq, d = x.shape
    depth = mix_norm_gain.shape[0]
    real = N_META + seq
    length = -(-real // SEQ_ALIGN) * SEQ_ALIGN
    meta = jnp.broadcast_to(meta_tokens[None].astype(x.dtype), (bsz, N_META, d))
    h = jnp.concatenate([meta, x, jnp.zeros((bsz, length - real, d), x.dtype)], axis=1)
    h = h.reshape(bsz * length, d)
    peer_u_bf = peer_u.astype(BF16)
    peer_v_bf = peer_v.astype(BF16)

    pending = None
    for layer in range(depth):
        if pending is None:
            hn = rms_norm_bf16(h, mix_norm_gain[layer])
        else:
            hn, h = rms_norm_bf16(h, mix_norm_gain[layer], delta=pending)
        if layer % 2 == 0:
            i = layer // 2
            width = rwkv_w0.shape[-1]
            main_cols, w_lora, mu_main, mu_lora, w2p, a2p, pool_cols = _even_layer_weights(
                w_in_even[i], shift_mu[i], rwkv_w2[i], rwkv_a2[i], width)
            z3 = matmul(hn, w_in_even, i, n=main_cols).reshape(bsz, length, -1)
            zl3 = matmul(hn, w_lora, 0).reshape(bsz, length, -1)
            y_pool = pool_mixer(z3, pool_w[i], pool_scale[i])
            ob = pool_cols // 512
            wb = width // 512
            r, k, v, kap, b, lw, g = rwkv_prep(
                z3, zl3, (ob, ob + wb, ob + 2 * wb), mu_main, mu_lora, rwkv_w0[i], w2p,
                rwkv_a0[i], a2p, rwkv_g2[i], rwkv_k_k[i], rwkv_k_a[i])
            y_rwkv = rwkv_scan(r, k, v, kap, b, lw, g, rwkv_r_k[i], rwkv_gn_gain[i],
                               rwkv_gn_bias[i])
            h = proj2_residual(y_pool.reshape(bsz * length, -1), y_rwkv.reshape(bsz * length, -1),
                               w_out_even, i, h)
        else:
            j = layer // 2
            u = matmul(hn, w_in_odd, j, out_dtype=BF16)
            y = s5_mixer(u.reshape(bsz, length, -1), ssm_lam_re[j], ssm_lam_im[j], ssm_log_dt[j],
                         ssm_b_re[j], ssm_b_im[j], ssm_c_re[j], ssm_c_im[j], ssm_d[j])
            h = glu_residual(y, w_glu, j, h)
        pending = peer_ffn(h, ffn_norm_gain[layer], peer_w_q, peer_sub_keys[layer], peer_u_bf,
                           peer_v_bf, layer)
    return final_norm(h.reshape(bsz, length, d), pending.reshape(bsz, length, d), final_norm_gain,
                      N_META, seq)
```

```python
import functools

import jax
import jax.numpy as jnp
from jax import lax
from jax.experimental import pallas as pl
from jax.experimental.pallas import tpu as pltpu

F32 = jnp.float32
BF16 = jnp.bfloat16

NORM_EPS = 1e-6
N_META = 16
POOL_WINDOWS = (2, 4, 8, 16)
RWKV_HEAD = 64
RWKV_W_RANK = 96
RWKV_A_RANK = 96
RWKV_G_RANK = 256
RWKV_GN_EPS = 64e-5
RWKV_CHUNK = 64
SSM_GROUP = 16
SSM_STATE = 64
SSM_CHUNK = 16
PEER_KEYS = 128
PEER_HEADS = 8
PEER_TOPK = 16
LANES = 128
SEQ_ALIGN = 128

NEG_INF = float("-inf")


def _params(sem, vmem_mb):
    return pltpu.CompilerParams(dimension_semantics=sem, vmem_limit_bytes=vmem_mb << 20)


def _split2(x):
    hi = x.astype(BF16)
    lo = (x - hi.astype(F32)).astype(BF16)
    return hi, lo


def _dot(a, b):
    return jnp.dot(a, b, preferred_element_type=F32)


def _dot_nt(a, b):
    return lax.dot_general(a, b, (((1,), (1,)), ((), ())), preferred_element_type=F32)


def _dot_acc(a, b):
    ah, al = _split2(a)
    bh, bl = _split2(b)
    return _dot(ah, bh) + _dot(al, bh) + _dot(ah, bl)


def _dot_nt_acc(a, b):
    ah, al = _split2(a)
    bh, bl = _split2(b)
    return _dot_nt(ah, bh) + _dot_nt(al, bh) + _dot_nt(ah, bl)


def _gelu(x):
    return 0.5 * x * (1.0 + lax.erf(x * 0.7071067811865476))


def _sigmoid(x):
    return 1.0 / (1.0 + jnp.exp(-x))


def _pick_tile(n, prefs):
    for t in prefs:
        if n % t == 0:
            return t
    return n


def _rms_norm_body(*refs, with_transpose, with_delta, windowed=False):
    refs = list(refs)
    x = refs.pop(0)[0] if windowed else refs.pop(0)[...]
    if with_delta:
        x = x + refs.pop(0)[...]
    g_ref, o_ref = refs.pop(0), refs.pop(0)
    inv = lax.rsqrt(jnp.mean(x * x, axis=-1, keepdims=True) + NORM_EPS)
    xn = (x * inv) * g_ref[...]
    o_ref[...] = xn.astype(BF16)
    if with_transpose:
        refs.pop(0)[...] = xn.T.astype(BF16)
    if with_delta:
        refs.pop(0)[...] = x


def rms_norm_bf16(x, gain, *, delta=None, with_transpose=False):
    t, d = x.shape
    tm = _pick_tile(t, (256, 128, 64, 32, 16, 8))
    row = pl.BlockSpec((tm, d), lambda i: (i, 0))
    args = [x] + ([delta] if delta is not None else []) + [gain.reshape(1, d)]
    in_specs = [row] * (len(args) - 1) + [pl.BlockSpec((1, d), lambda i: (0, 0))]
    out_shape = [jax.ShapeDtypeStruct((t, d), BF16)]
    out_specs = [row]
    if with_transpose:
        out_shape.append(jax.ShapeDtypeStruct((d, t), BF16))
        out_specs.append(pl.BlockSpec((d, tm), lambda i: (0, i)))
    if delta is not None:
        out_shape.append(jax.ShapeDtypeStruct((t, d), F32))
        out_specs.append(row)
    res = pl.pallas_call(
        functools.partial(_rms_norm_body, with_transpose=with_transpose,
                          with_delta=delta is not None),
        grid=(t // tm,),
        in_specs=in_specs,
        out_specs=out_specs,
        out_shape=out_shape,
        compiler_params=_params(("parallel",), 48),
        name="rms_norm",
    )(*args)
    return res if len(res) > 1 else res[0]


def rms_norm_rows(x3, gain, first, count):
    bsz, _, d = x3.shape
    tm = _pick_tile(count, (256, 128, 64, 32, 16, 8))
    nt = count // tm
    assert first % 8 == 0
    window = pl.BlockSpec((pl.Element(1), pl.Element(tm), pl.Element(d)),
                          lambda b, i: (b, pl.multiple_of(first + i * tm, 8), 0))
    return pl.pallas_call(
        functools.partial(_rms_norm_body, with_transpose=True, with_delta=False, windowed=True),
        grid=(bsz, nt),
        in_specs=[window, pl.BlockSpec((1, d), lambda b, i: (0, 0))],
        out_specs=[pl.BlockSpec((tm, d), lambda b, i: (b * nt + i, 0)),
                   pl.BlockSpec((d, tm), lambda b, i: (0, b * nt + i))],
        out_shape=[jax.ShapeDtypeStruct((bsz * count, d), BF16),
                   jax.ShapeDtypeStruct((d, bsz * count), BF16)],
        compiler_params=_params(("parallel", "parallel"), 48),
        name="rms_norm_rows",
    )(x3, gain.reshape(1, d))


def _row_tile(t):
    for cand in (1088, 1024, 768, 512, 384, 256, 128, 64, 32, 16):
        if t % cand == 0:
            return cand
    return t


def _matmul_body(a_ref, w_ref, o_ref):
    o_ref[...] = _dot(a_ref[...], w_ref[...].astype(BF16)).astype(o_ref.dtype)


def matmul(a, w_all, layer, *, n=None, out_dtype=F32):
    t, k = a.shape
    n = w_all.shape[2] if n is None else n
    tm = _row_tile(t)
    tn = _pick_tile(n, (512, 256, 128))
    return pl.pallas_call(
        _matmul_body,
        grid=(t // tm, n // tn),
        in_specs=[pl.BlockSpec((tm, k), lambda i, j: (i, 0)),
                  pl.BlockSpec((None, k, tn), lambda i, j: (layer, 0, j))],
        out_specs=pl.BlockSpec((tm, tn), lambda i, j: (i, j)),
        out_shape=jax.ShapeDtypeStruct((t, n), out_dtype),
        compiler_params=_params(("parallel", "arbitrary"), 56),
        name="matmul",
    )(a, w_all)


def _proj2_body(a1_ref, a2_ref, w1_ref, w2_ref, h_ref, o_ref):
    o_ref[...] = (h_ref[...] + _dot(a1_ref[...], w1_ref[...].astype(BF16))
                  + _dot(a2_ref[...], w2_ref[...].astype(BF16)))


def proj2_residual(a1, a2, w_all, layer, h):
    t, k1 = a1.shape
    k2 = a2.shape[1]
    n = w_all.shape[2]
    assert k1 == k2
    tm = _row_tile(t)
    tn = _pick_tile(n, (512, 256, 128))
    return pl.pallas_call(
        _proj2_body,
        grid=(t // tm, n // tn),
        in_specs=[
            pl.BlockSpec((tm, k1), lambda i, j: (i, 0)),
            pl.BlockSpec((tm, k2), lambda i, j: (i, 0)),
            pl.BlockSpec((None, k1, tn), lambda i, j: (layer, 0, j)),
            pl.BlockSpec((None, k2, tn), lambda i, j: (layer, 1, j)),
            pl.BlockSpec((tm, tn), lambda i, j: (i, j)),
        ],
        out_specs=pl.BlockSpec((tm, tn), lambda i, j: (i, j)),
        out_shape=jax.ShapeDtypeStruct((t, n), F32),
        compiler_params=_params(("parallel", "arbitrary"), 56),
        name="proj2_residual",
    )(a1, a2, w_all, w_all, h)


def _glu_body(y_ref, wa_ref, wb_ref, h_ref, o_ref):
    y = y_ref[...]
    ga = _dot(y, wa_ref[...].astype(BF16))
    gb = _dot(y, wb_ref[...].astype(BF16))
    o_ref[...] = h_ref[...] + ga * _sigmoid(gb)


def glu_residual(y, w_all, layer, h):
    t, k = y.shape
    n = w_all.shape[2] // 2
    tm = _row_tile(t)
    tn = _pick_tile(n, (512, 256, 128))
    nb = n // tn
    return pl.pallas_call(
        _glu_body,
        grid=(t // tm, nb),
        in_specs=[
            pl.BlockSpec((tm, k), lambda i, j: (i, 0)),
            pl.BlockSpec((None, k, tn), lambda i, j: (layer, 0, j)),
            pl.BlockSpec((None, k, tn), lambda i, j: (layer, 0, j + nb)),
            pl.BlockSpec((tm, tn), lambda i, j: (i, j)),
        ],
        out_specs=pl.BlockSpec((tm, tn), lambda i, j: (i, j)),
        out_shape=jax.ShapeDtypeStruct((t, n), F32),
        compiler_params=_params(("parallel", "arbitrary"), 56),
        name="glu_residual",
    )(y, w_all, w_all, h)


def _pool_body(z_ref, w_ref, s_ref, o_ref):
    grp = pl.program_id(1)
    z = z_ref[...]
    length = z.shape[0]
    row = lax.broadcasted_iota(jnp.int32, (length, 1), 0)
    posf = (row + 1).astype(F32)

    def shifted(x, k):
        return jnp.where(row >= k, pltpu.roll(x, k, axis=0), 0.0)

    for gi, win in enumerate(POOL_WINDOWS):

        @pl.when(grp == gi)
        def _(win=win):
            acc = z
            span = 1
            while span < win:
                acc = acc + shifted(acc, span)
                span *= 2
            cnt = jnp.minimum(posf, float(win))
            m = acc / cnt - z
            y = _dot(m.astype(BF16), w_ref[...])
            o_ref[...] = (y * s_ref[...]).astype(o_ref.dtype)


def pool_mixer(z3, pool_w, pool_scale):
    bsz, length, _ = z3.shape
    ng, c, _ = pool_w.shape
    return pl.pallas_call(
        _pool_body,
        grid=(bsz, ng),
        in_specs=[
            pl.BlockSpec((None, length, c), lambda b, g: (b, 0, g)),
            pl.BlockSpec((None, c, c), lambda b, g: (g, 0, 0)),
            pl.BlockSpec((1, c), lambda b, g: (0, g)),
        ],
        out_specs=pl.BlockSpec((None, length, c), lambda b, g: (b, 0, g)),
        out_shape=jax.ShapeDtypeStruct((bsz, length, ng * c), BF16),
        compiler_params=_params(("parallel", "arbitrary"), 48),
        name="pool_mixer",
    )(z3, pool_w.astype(BF16), pool_scale.reshape(1, ng * c))


def _head_sum_matrix(width, head):
    shift = head.bit_length() - 1
    r = lax.broadcasted_iota(jnp.int32, (width, width), 0) >> shift
    c = lax.broadcasted_iota(jnp.int32, (width, width), 1) >> shift
    return jnp.where(r == c, 1.0, 0.0).astype(BF16)


def _head_sums(x, ones_bd):
    hi, lo = _split2(x)
    return _dot(hi, ones_bd) + _dot(lo, ones_bd)


def _rwkv_prep_body(zr_ref, zk_ref, zv_ref, zl_ref, pr_ref, pk_ref, pv_ref, pl_ref,
                    mur_ref, muk_ref, muv_ref, mul_ref, w0_ref, w2_ref, a0_ref, a2_ref, g2_ref,
                    kk_ref, ka_ref,
                    r_out, k_out, v_out, kap_out, b_out, lw_out, g_out):
    first = pl.program_id(1) == 0
    tl = zr_ref.shape[0]
    row = lax.broadcasted_iota(jnp.int32, (tl, 1), 0)

    def mix(z_ref, p_ref, mu_ref):
        z = z_ref[...]
        prev_last = jnp.where(first, 0.0, p_ref[7:8, :])
        prev = jnp.where(row == 0, prev_last, pltpu.roll(z, 1, axis=0))
        return z + (prev - z) * mu_ref[...]

    r = mix(zr_ref, pr_ref, mur_ref)
    k = mix(zk_ref, pk_ref, muk_ref)
    v = mix(zv_ref, pv_ref, muv_ref)
    lora = mix(zl_ref, pl_ref, mul_ref)
    w_lr = lora[:, 0:LANES]
    a_lr = lora[:, LANES:2 * LANES]
    g_lr = lora[:, 2 * LANES:]

    x = w0_ref[...] + _dot_acc(jnp.tanh(w_lr), w2_ref[...])
    y = -x
    softplus = jnp.maximum(y, 0.0) + jnp.log(1.0 + jnp.exp(-jnp.abs(y)))
    lw = -jnp.exp(-softplus - 0.5)
    a = _sigmoid(a0_ref[...] + _dot_acc(a_lr, a2_ref[...]))
    g = _dot_acc(_sigmoid(g_lr), g2_ref[...])

    ones_bd = _head_sum_matrix(LANES, RWKV_HEAD)
    kk = k * kk_ref[...]
    kk2 = kk * kk
    width = k.shape[1]
    ss = jnp.concatenate(
        [_head_sums(kk2[:, c:c + LANES], ones_bd) for c in range(0, width, LANES)], axis=1)
    kap = kk / jnp.maximum(jnp.sqrt(ss), 1e-12)

    r_out[...] = r.astype(r_out.dtype)
    k_out[...] = (k * (1.0 + (a - 1.0) * ka_ref[...])).astype(k_out.dtype)
    v_out[...] = v.astype(v_out.dtype)
    kap_out[...] = kap.astype(kap_out.dtype)
    b_out[...] = (kap * a).astype(b_out.dtype)
    lw_out[...] = lw
    g_out[...] = g.astype(g_out.dtype)


def rwkv_prep(z3, zl3, off_blocks, mu, mu_l, w0, w2p, a0, a2p, g2, k_k, k_a):
    bsz, length, _ = z3.shape
    width = w0.shape[-1]
    cw = 512
    ncb = width // cw
    tl = _pick_tile(length, (352, 272, 192, 176, 136, 128, 64))
    ob_r, ob_k, ob_v = off_blocks
    ob_l = 0
    t8 = tl // 8

    def cur(ob, fixed=False):
        if fixed:
            return pl.BlockSpec((None, tl, cw), lambda b, i, c: (b, i, ob))
        return pl.BlockSpec((None, tl, cw), lambda b, i, c: (b, i, ob + c))

    def prev(ob, fixed=False):
        if fixed:
            return pl.BlockSpec((None, 8, cw), lambda b, i, c: (b, jnp.maximum(i * t8 - 1, 0), ob))
        return pl.BlockSpec((None, 8, cw), lambda b, i, c: (b, jnp.maximum(i * t8 - 1, 0), ob + c))

    def vec(ob, fixed=False):
        if fixed:
            return pl.BlockSpec((1, cw), lambda b, i, c: (0, ob))
        return pl.BlockSpec((1, cw), lambda b, i, c: (0, ob + c))

    colvec = pl.BlockSpec((1, cw), lambda b, i, c: (0, c))
    out_spec = pl.BlockSpec((None, tl, cw), lambda b, i, c: (b, i, c))
    out_dtypes = [BF16, BF16, BF16, BF16, BF16, F32, BF16]
    return pl.pallas_call(
        _rwkv_prep_body,
        grid=(bsz, length // tl, ncb),
        in_specs=[
            cur(ob_r), cur(ob_k), cur(ob_v), cur(ob_l, True),
            prev(ob_r), prev(ob_k), prev(ob_v), prev(ob_l, True),
            vec(ob_r), vec(ob_k), vec(ob_v), vec(ob_l, True),
            colvec,
            pl.BlockSpec((LANES, cw), lambda b, i, c: (0, c)),
            colvec,
            pl.BlockSpec((LANES, cw), lambda b, i, c: (0, c)),
            pl.BlockSpec((RWKV_G_RANK, cw), lambda b, i, c: (0, c)),
            colvec, colvec,
        ],
        out_specs=[out_spec] * 7,
        out_shape=[jax.ShapeDtypeStruct((bsz, length, width), dt) for dt in out_dtypes],
        compiler_params=_params(("parallel", "parallel", "arbitrary"), 56),
        name="rwkv_prep",
    )(z3, z3, z3, zl3, z3, z3, z3, zl3, mu, mu, mu, mu_l,
      w0.reshape(1, width), w2p, a0.reshape(1, width), a2p, g2,
      k_k.reshape(1, width), k_a.reshape(1, width))


def _rwkv_scan_body(r_ref, k_ref, v_ref, kap_ref, b_ref, lw_ref, g_ref, rk_ref, gain_ref, bias_ref,
                    o_ref, h_ref, *, pairs):
    cn = RWKV_CHUNK
    hd = RWKV_HEAD

    @pl.when(pl.program_id(2) == 0)
    def _():
        h_ref[...] = jnp.zeros_like(h_ref)

    lane = lax.broadcasted_iota(jnp.int32, (1, LANES), 1)
    m0 = lane < hd
    rowi = lax.broadcasted_iota(jnp.int32, (cn, LANES), 0)
    coli = lax.broadcasted_iota(jnp.int32, (cn, LANES), 1) & (hd - 1)
    strict = rowi > coli
    incl = rowi >= coli
    eye2 = jnp.where(rowi == coli, 1.0, 0.0)
    tri = jnp.where(lax.broadcasted_iota(jnp.int32, (cn, cn), 0)
                    >= lax.broadcasted_iota(jnp.int32, (cn, cn), 1), 1.0, 0.0).astype(BF16)
    br = lax.broadcasted_iota(jnp.int32, (LANES, LANES), 0) >> (hd.bit_length() - 1)
    bc = lax.broadcasted_iota(jnp.int32, (LANES, LANES), 1) >> (hd.bit_length() - 1)
    bd_mask = br == bc
    ones_bd = jnp.where(bd_mask, 1.0, 0.0).astype(BF16)

    def stk(x):
        return jnp.concatenate([jnp.where(m0, x, 0.0), jnp.where(m0, 0.0, x)], axis=0).astype(BF16)

    ps = range(pairs)
    sls = [slice(p * LANES, (p + 1) * LANES) for p in ps]
    r = [r_ref[:, s].astype(F32) for s in sls]
    k = [k_ref[:, s].astype(F32) for s in sls]
    v = [v_ref[:, s].astype(F32) for s in sls]
    kap = [kap_ref[:, s].astype(F32) for s in sls]
    b = [b_ref[:, s].astype(F32) for s in sls]
    lw = [lw_ref[:, s] for s in sls]
    hbd = [h_ref[p] for p in ps]

    lsp = [_split2(x) for x in lw]
    lp = [_dot(tri, a) + _dot(tri, c) for a, c in lsp]
    lpc = [x[cn - 1:cn, :] for x in lp]
    pinv = [jnp.exp(-x) for x in lp]
    pend = [jnp.exp(c - x) for c, x in zip(lpc, lp)]
    kt = [kap[p] * jnp.exp(lp[p] - lw[p]) for p in ps]
    rt = [r[p] * jnp.exp(lp[p]) for p in ps]
    kh = [k[p] * pinv[p] for p in ps]
    bh = [b[p] * pinv[p] for p in ps]
    khc = [k[p] * pend[p] for p in ps]
    bhc = [b[p] * pend[p] for p in ps]

    ktrt = [jnp.concatenate([kt[p], rt[p]], axis=0).astype(BF16) for p in ps]
    ab_all = [_dot_nt(ktrt[p], stk(bh[p])) for p in ps]
    ak_all = [_dot_nt(ktrt[p], stk(kh[p])) for p in ps]
    n_mat = [jnp.where(strict, x[:cn], 0.0) for x in ab_all]
    a_k = [jnp.where(strict, x[:cn], 0.0) for x in ak_all]
    a_rb = [jnp.where(incl, x[cn:], 0.0) for x in ab_all]
    a_rk = [jnp.where(incl, x[cn:], 0.0) for x in ak_all]

    t_mat = [eye2 - x for x in n_mat]
    pw = n_mat
    span = 1
    while 2 * span < cn:
        pw = [_dot(x.astype(BF16), stk(x)) for x in pw]
        t_mat = [t + _dot(t.astype(BF16), stk(x)) for t, x in zip(t_mat, pw)]
        span *= 2
    t_bf = [t.astype(BF16) for t in t_mat]

    kbar = [_dot(t_bf[p], stk(kt[p])) for p in ps]
    akv = [_dot(a_k[p].astype(BF16), stk(v[p])) for p in ps]
    ubar = [_dot(t_bf[p], stk(akv[p])) for p in ps]
    hb = [x.astype(BF16) for x in hbd]
    u = [_dot(kbar[p].astype(BF16), hb[p]) + ubar[p] for p in ps]
    y = [_dot(rt[p].astype(BF16), hb[p]) + _dot(a_rk[p].astype(BF16), stk(v[p]))
         - _dot(a_rb[p].astype(BF16), stk(u[p])) for p in ps]

    kb_t = [jnp.concatenate([khc[p], -bhc[p]], axis=0).T.astype(BF16) for p in ps]
    vu = [jnp.concatenate([v[p], u[p]], axis=0).astype(BF16) for p in ps]
    g_mat = [_dot(kb_t[p], vu[p]) for p in ps]
    pc_col = [jnp.exp(jnp.broadcast_to(x, (LANES, LANES)).T) for x in lpc]
    for p in ps:
        h_ref[p] = jnp.where(bd_mask, pc_col[p] * hbd[p] + g_mat[p], 0.0)

    mean = [_head_sums(x, ones_bd) * (1.0 / hd) for x in y]
    yc = [y[p] - mean[p] for p in ps]
    var = [_head_sums(x * x, ones_bd) * (1.0 / hd) for x in yc]
    bonus = [_head_sums(r[p] * k[p] * rk_ref[:, sls[p]], ones_bd) * v[p] for p in ps]
    for p in ps:
        yn = yc[p] * lax.rsqrt(var[p] + RWKV_GN_EPS) * gain_ref[:, sls[p]] + bias_ref[:, sls[p]]
        o_ref[:, sls[p]] = ((yn + bonus[p]) * g_ref[:, sls[p]].astype(F32)).astype(o_ref.dtype)


def rwkv_scan(r, k, v, kap, b, lw, g, r_k, gn_gain, gn_bias):
    bsz, length, width = r.shape
    cw = _pick_tile(width, (2048, 1024, 512))
    pairs = cw // LANES
    seq = pl.BlockSpec((None, RWKV_CHUNK, cw), lambda bi, c, t: (bi, t, c))
    vec = pl.BlockSpec((1, cw), lambda bi, c, t: (0, c))
    return pl.pallas_call(
        functools.partial(_rwkv_scan_body, pairs=pairs),
        grid=(bsz, width // cw, length // RWKV_CHUNK),
        in_specs=[seq] * 7 + [vec] * 3,
        out_specs=seq,
        out_shape=jax.ShapeDtypeStruct((bsz, length, width), BF16),
        scratch_shapes=[pltpu.VMEM((pairs, LANES, LANES), F32)],
        compiler_params=_params(("parallel", "parallel", "arbitrary"), 48),
        name="rwkv_scan",
    )(r, k, v, kap, b, lw, g, r_k.reshape(1, width), gn_gain.reshape(1, width),
      gn_bias.reshape(1, width))


def _s5_param_body(lr_ref, li_ref, ldt_ref, btr_ref, bti_ref, cr_ref, ci_ref,
                   kst_ref, etr_ref, eti_ref, car_ref, cai_ref, a16r_ref, a16i_ref, *, groups):
    nc = SSM_CHUNK
    npow = 24
    tau = lax.broadcasted_iota(jnp.int32, (npow, SSM_STATE), 0).astype(F32)
    for gidx in range(groups):
        lr = lr_ref[gidx]
        li = li_ref[gidx]
        dt = jnp.exp(ldt_ref[gidx])
        mag = jnp.exp(tau * (lr * dt))
        ang = tau * (li * dt)
        pr = mag * jnp.cos(ang)
        pi = mag * jnp.sin(ang)
        ar = pr[1:2]
        ai = pi[1:2]
        den = lr * lr + li * li
        fr = ((ar - 1.0) * lr + ai * li) / den
        fi = (ai * lr - (ar - 1.0) * li) / den
        btr = btr_ref[gidx]
        bti = bti_ref[gidx]
        bbr = fr * btr - fi * bti
        bbi = fr * bti + fi * btr
        cr = cr_ref[gidx]
        ci = ci_ref[gidx]
        etr = jnp.concatenate([pr[s:s + 1] * bbr - pi[s:s + 1] * bbi for s in range(nc)], axis=0)
        eti = jnp.concatenate([pr[s:s + 1] * bbi + pi[s:s + 1] * bbr for s in range(nc)], axis=0)
        car = jnp.concatenate([pr[s:s + 1] * cr - pi[s:s + 1] * ci for s in range(1, nc + 1)], axis=0)
        cai = jnp.concatenate([pr[s:s + 1] * ci + pi[s:s + 1] * cr for s in range(1, nc + 1)], axis=0)
        kst_ref[gidx] = _dot_nt_acc(etr, cr) - _dot_nt_acc(eti, ci)
        etr_ref[gidx] = etr
        eti_ref[gidx] = eti
        car_ref[gidx] = car
        cai_ref[gidx] = cai
        a16r_ref[gidx] = jnp.broadcast_to(pr[nc:nc + 1], (8, SSM_STATE))
        a16i_ref[gidx] = jnp.broadcast_to(pi[nc:nc + 1], (8, SSM_STATE))


def s5_params(lam_re, lam_im, log_dt, b_re, b_im, c_re, c_im):
    ng, ns = lam_re.shape
    gc = SSM_GROUP
    gb = 8
    rows = SSM_CHUNK * gc

    def spec(*dims):
        return pl.BlockSpec((gb,) + dims, lambda i: (i,) + (0,) * len(dims))

    outs = [((rows, gc), F32)] + [((rows, ns), F32)] * 4 + [((8, ns), F32)] * 2
    return pl.pallas_call(
        functools.partial(_s5_param_body, groups=gb),
        grid=(ng // gb,),
        in_specs=[spec(1, ns), spec(1, ns), spec(1, 1), spec(gc, ns), spec(gc, ns), spec(gc, ns),
                  spec(gc, ns)],
        out_specs=[spec(*s) for s, _ in outs],
        out_shape=[jax.ShapeDtypeStruct((ng,) + s, dt) for s, dt in outs],
        compiler_params=_params(("parallel",), 32),
        name="s5_params",
    )(lam_re.reshape(ng, 1, ns), lam_im.reshape(ng, 1, ns), log_dt.reshape(ng, 1, 1),
      jnp.swapaxes(b_re, 1, 2), jnp.swapaxes(b_im, 1, 2), c_re, c_im)


def _s5_body(x_ref, toep_ref, wsr_ref, wsi_ref, wor_ref, woi_ref, a16r_ref, a16i_ref, d_ref,
             o_ref, sre_ref, sim_ref, xre_ref, xim_ref):
    rows = x_ref.shape[1]
    nchunk = rows // 8
    x0 = x_ref[0]
    x1 = x_ref[1]
    sre_ref[...] = _dot(x0, wsr_ref[0]) + _dot(x1, wsr_ref[1])
    sim_ref[...] = _dot(x0, wsi_ref[0]) + _dot(x1, wsi_ref[1])
    ar = a16r_ref[...]
    ai = a16i_ref[...]

    def step(c, carry):
        xr, xi = carry
        off = pl.multiple_of(c * 8, 8)
        xre_ref[pl.ds(off, 8), :] = xr
        xim_ref[pl.ds(off, 8), :] = xi
        sr = sre_ref[pl.ds(off, 8), :]
        si = sim_ref[pl.ds(off, 8), :]
        return ar * xr - ai * xi + sr, ar * xi + ai * xr + si

    zero = jnp.zeros((8, LANES), F32)
    lax.fori_loop(0, nchunk, step, (zero, zero))
    xr = xre_ref[...].astype(BF16)
    xi = xim_ref[...].astype(BF16)
    for q, xq in enumerate((x0, x1)):
        y = (_dot(xq, toep_ref[q]) + _dot(xr, wor_ref[q]) + _dot(xi, woi_ref[q])
             + xq.astype(F32) * d_ref[q])
        o_ref[q] = _gelu(y).astype(o_ref.dtype)


def s5_apply(xg, toep, wsr, wsi, wor, woi, a16r, a16i, dflat):
    ng, rows, cols = xg.shape

    def spec(*dims):
        return pl.BlockSpec((2,) + dims, lambda i: (i,) + (0,) * len(dims))

    return pl.pallas_call(
        _s5_body,
        grid=(ng // 2,),
        in_specs=[spec(rows, cols), spec(cols, cols), spec(cols, LANES), spec(cols, LANES),
                  spec(LANES, cols), spec(LANES, cols),
                  pl.BlockSpec((None, 8, LANES), lambda i: (i, 0, 0)),
                  pl.BlockSpec((None, 8, LANES), lambda i: (i, 0, 0)),
                  spec(1, cols)],
        out_specs=spec(rows, cols),
        out_shape=jax.ShapeDtypeStruct((ng, rows, cols), BF16),
        scratch_shapes=[pltpu.VMEM((rows, LANES), F32)] * 4,
        compiler_params=_params(("parallel",), 32),
        name="s5_apply",
    )(xg, toep, wsr, wsi, wor, woi, a16r, a16i, dflat)


def s5_mixer(u3, lam_re, lam_im, log_dt, b_re, b_im, c_re, c_im, d_skip):
    bsz, length, width = u3.shape
    ng, ns = lam_re.shape
    gc = SSM_GROUP
    nc = SSM_CHUNK
    kst, etr, eti, car, cai, a16r, a16i = s5_params(lam_re, lam_im, log_dt, b_re, b_im, c_re, c_im)

    k4 = kst.reshape(ng, nc, gc, gc)
    s_idx = jnp.arange(nc)[:, None]
    t_idx = jnp.arange(nc)[None, :]
    lag = t_idx - s_idx
    toep = jnp.where((lag >= 0)[None, :, :, None, None], k4[:, jnp.clip(lag, 0, nc - 1)], 0.0)
    toep = toep.transpose(0, 1, 3, 2, 4).reshape(ng, nc * gc, nc * gc).astype(BF16)
    odd = (jnp.arange(ng) % 2 == 1)[:, None, None]

    def lanes_by_parity(w):
        z = jnp.zeros_like(w)
        return jnp.where(odd, jnp.concatenate([z, w], -1), jnp.concatenate([w, z], -1))

    def flip_lag(e):
        return e.reshape(ng, nc, gc, ns)[:, ::-1].reshape(ng, nc * gc, ns)

    wsr = lanes_by_parity(flip_lag(etr)).astype(BF16)
    wsi = lanes_by_parity(flip_lag(eti)).astype(BF16)
    wor = jnp.swapaxes(lanes_by_parity(car), 1, 2).astype(BF16)
    woi = jnp.swapaxes(lanes_by_parity(-cai), 1, 2).astype(BF16)

    def pair_lanes(a):
        return a.reshape(ng // 2, 2, 8, ns).transpose(0, 2, 1, 3).reshape(ng // 2, 8, 2 * ns)

    dflat = jnp.tile(d_skip.reshape(ng, 1, gc), (1, 1, nc))

    nchunk = length // nc
    x = u3.reshape(bsz, nchunk, nc, ng, gc).transpose(3, 1, 0, 2, 4)
    x = jnp.pad(x, ((0, 0), (0, 0), (0, 8 - bsz), (0, 0), (0, 0))).reshape(ng, nchunk * 8, nc * gc)
    y = s5_apply(x, toep, wsr, wsi, wor, woi, pair_lanes(a16r), pair_lanes(a16i), dflat)
    y = y.reshape(ng, nchunk, 8, nc, gc)[:, :, :bsz].transpose(2, 1, 3, 0, 4)
    return y.reshape(bsz * length, width)


def _topk_rows(s, k):
    tops = []
    cur = s
    rank = jnp.full_like(s, float(PEER_KEYS - 1))
    for i in range(k):
        m = jnp.max(cur, axis=0, keepdims=True)
        tops.append(m)
        hit = cur >= m
        rank = jnp.where(hit, float(i), rank)
        cur = jnp.where(hit, NEG_INF, cur)
    return tops, rank


def _peer_route_body(q_ref, keys_ref, cnt_ref, e1_ref, r2_ref, e2_ref):
    nk = PEER_KEYS
    k = PEER_TOPK
    tm = q_ref.shape[0]
    wide, tall = 4, 3
    assert k == 16 and (wide + 1) * (tall + 1) > k and 2 * 9 > k
    a_idx = lax.broadcasted_iota(jnp.int32, (k, 1), 0)
    for h in range(PEER_HEADS):
        s1_all = _dot_nt_acc(keys_ref[h, 0], q_ref[:, (2 * h) * nk:(2 * h + 1) * nk])
        s2_all = _dot_nt_acc(keys_ref[h, 1], q_ref[:, (2 * h + 1) * nk:(2 * h + 2) * nk])
        for c in range(tm // LANES):
            ts = slice(c * LANES, (c + 1) * LANES)
            s1 = s1_all[:, ts]
            s2 = s2_all[:, ts]
            t1, _ = _topk_rows(s1, k)
            t2, rank2 = _topk_rows(s2, k)
            top1 = jnp.concatenate(t1, axis=0)
            top2 = jnp.concatenate(t2, axis=0)
            row_blocks = [t1[0] + top2] + [t1[a] + top2[0:8] for a in range(1, wide)]
            col_blocks = [jnp.where(a_idx >= wide, top1 + t2[b], NEG_INF) for b in range(tall)]
            cand = jnp.concatenate(row_blocks + col_blocks, axis=0)
            best = t1[0] + t2[0]
            zsum = jnp.zeros_like(best)
            cur = cand
            tau = best
            for _ in range(k):
                tau = jnp.max(cur, axis=0, keepdims=True)
                zsum = zsum + jnp.exp(tau - best)
                cur = jnp.where(cur >= tau, NEG_INF, cur)
            kept_rows = [jnp.sum(jnp.where(blk >= tau, 1.0, 0.0), axis=0, keepdims=True)
                         for blk in row_blocks]
            kept_cols = jnp.where(col_blocks[0] >= tau, 1.0, 0.0)
            for b in range(1, tall):
                kept_cols = kept_cols + jnp.where(col_blocks[b] >= tau, 1.0, 0.0)
            cnt = jnp.zeros_like(s1)
            for a in range(k):
                kept_a = kept_rows[a] if a < wide else kept_cols[a:a + 1]
                cnt = jnp.where(s1 == t1[a], kept_a, cnt)
            cnt_ref[h, :, ts] = cnt
            r2_ref[h, :, ts] = rank2.astype(BF16)
            e1_ref[h, :, ts] = jnp.exp(s1 - t1[0]) / zsum
            e2_ref[h, :, ts] = jnp.exp(s2 - t2[0]).astype(BF16)


def peer_route(q, keys):
    t = q.shape[0]
    tm = _pick_tile(t, (256, 128))
    assert tm % LANES == 0, "the route kernel walks its token tile in 128-lane chunks"
    big = pl.BlockSpec((PEER_HEADS, PEER_KEYS, tm), lambda i: (0, 0, i))
    sds = jax.ShapeDtypeStruct((PEER_HEADS, PEER_KEYS, t), F32)
    sds_bf = jax.ShapeDtypeStruct((PEER_HEADS, PEER_KEYS, t), BF16)
    return pl.pallas_call(
        _peer_route_body,
        grid=(t // tm,),
        in_specs=[pl.BlockSpec((tm, q.shape[1]), lambda i: (i, 0)),
                  pl.BlockSpec(keys.shape, lambda i: (0, 0, 0, 0))],
        out_specs=[big, big, big, big],
        out_shape=[sds, sds, sds_bf, sds_bf],
        compiler_params=_params(("parallel",), 48),
        name="peer_route",
    )(q, keys)


PEER_SLAB = 256


PEER_TILE = 512
SUBLANES = 8


def _peer_dense_body(hnt_ref, u_ref, v_ref, r2_ref, e2_ref, cnt_ref, e1_ref, o_ref, *, te):
    nk = PEER_KEYS
    rows_per_tile = te // nk
    rows_per_slab = PEER_SLAB // nk
    nslab = te // PEER_SLAB
    j = pl.program_id(1)
    base = (j % (SUBLANES // rows_per_tile)) * rows_per_tile

    @pl.when(j == 0)
    def _():
        o_ref[...] = jnp.zeros_like(o_ref)

    def gates(s, hold):
        out = []
        zero = jnp.zeros((), BF16)
        for r in range(rows_per_slab):
            row = base + s * rows_per_slab + r
            gate = None
            for h in range(PEER_HEADS):
                cnt_row = cnt_ref[h, pl.ds(row, 1), :]
                if hold is not None:
                    cnt_row = cnt_row + hold
                keep = r2_ref[h] < cnt_row.astype(BF16)
                term = jnp.where(keep, e2_ref[h] * e1_ref[h, pl.ds(row, 1), :].astype(BF16), zero)
                gate = term if gate is None else gate + term
            out.append(gate.astype(F32))
        return out

    hnt = hnt_ref[...]
    acts = [_dot(u_ref[s * PEER_SLAB:(s + 1) * PEER_SLAB, :], hnt) for s in range(nslab)]
    total = None
    hold = None
    for s in range(nslab):
        gate_blocks = gates(s, hold)
        parts = [gate_blocks[r] * _gelu(acts[s][r * nk:(r + 1) * nk, :])
                 for r in range(rows_per_slab)]
        w = jnp.concatenate(parts, axis=0).T.astype(BF16)
        c = _dot(w, v_ref[s * PEER_SLAB:(s + 1) * PEER_SLAB, :])
        hold = 0.0 * parts[-1][0:1, :]
        total = c if total is None else total + c
    o_ref[...] += total


def peer_dense(hn_t, u_all, v_all, layer, rank2, e2, cnt, e1):
    d, t = hn_t.shape
    ne = u_all.shape[1]
    tm = _pick_tile(t, (512, 256, 128))
    te = min(PEER_TILE, ne)
    rows_per_tile = te // PEER_KEYS
    tiles_per_block = SUBLANES // rows_per_tile
    big = pl.BlockSpec((PEER_HEADS, PEER_KEYS, tm), lambda i, j: (0, 0, i))
    rows = pl.BlockSpec((PEER_HEADS, SUBLANES, tm), lambda i, j: (0, j // tiles_per_block, i))
    table = pl.BlockSpec((None, te, d), lambda i, j: (layer, j, 0))
    return pl.pallas_call(
        functools.partial(_peer_dense_body, te=te),
        grid=(t // tm, ne // te),
        in_specs=[pl.BlockSpec((d, tm), lambda i, j: (0, i)), table, table, big, big, rows, rows],
        out_specs=pl.BlockSpec((tm, d), lambda i, j: (i, 0)),
        out_shape=jax.ShapeDtypeStruct((t, d), F32),
        compiler_params=_params(("parallel", "arbitrary"), 60),
        name="peer_dense",
    )(hn_t, u_all, v_all, rank2, e2, cnt, e1)


def peer_ffn(h, gain, w_q_all, keys, u_all, v_all, layer, window=None):
    if window is None:
        hn, hn_t = rms_norm_bf16(h, gain, with_transpose=True)
    else:
        hn, hn_t = rms_norm_rows(h, gain, *window)
    q = matmul(hn, w_q_all, layer)
    cnt, e1, rank2, e2 = peer_route(q, keys)
    return peer_dense(hn_t, u_all, v_all, layer, rank2, e2, cnt, e1)


def _final_norm_body(x_ref, dx_ref, g_ref, o_ref):
    x = x_ref[0] + dx_ref[0]
    inv = lax.rsqrt(jnp.mean(x * x, axis=-1, keepdims=True) + NORM_EPS)
    o_ref[...] = (x * inv) * g_ref[...]


def final_norm(h3, delta3, gain, first):
    bsz, _, d = h3.shape
    count = delta3.shape[1]
    tm = _pick_tile(count, (256, 128, 64, 32, 16, 8))
    assert first % 8 == 0
    rows = pl.BlockSpec((pl.Element(1), pl.Element(tm), pl.Element(d)),
                        lambda b, i: (b, pl.multiple_of(first + i * tm, 8), 0))
    return pl.pallas_call(
        _final_norm_body,
        grid=(bsz, count // tm),
        in_specs=[rows, pl.BlockSpec((1, tm, d), lambda b, i: (b, i, 0)),
                  pl.BlockSpec((1, d), lambda b, i: (0, 0))],
        out_specs=pl.BlockSpec((None, tm, d), lambda b, i: (b, i, 0)),
        out_shape=jax.ShapeDtypeStruct((bsz, count, d), F32),
        compiler_params=_params(("parallel", "parallel"), 32),
        name="final_norm",
    )(h3, delta3, gain.reshape(1, d))


def _even_layer_weights(w_in, shift_mu, w2, a2, width):
    pool_w = w_in.shape[1] - shift_mu.shape[0]
    c0 = pool_w + 3 * width
    c1 = c0 + RWKV_W_RANK
    c2 = c1 + RWKV_A_RANK
    d = w_in.shape[0]
    zw = jnp.zeros((d, LANES - RWKV_W_RANK), w_in.dtype)
    za = jnp.zeros((d, LANES - RWKV_A_RANK), w_in.dtype)
    w_lora = jnp.concatenate([w_in[:, c0:c1], zw, w_in[:, c1:c2], za, w_in[:, c2:]], axis=1)[None]
    s0 = 3 * width
    s1 = s0 + RWKV_W_RANK
    s2 = s1 + RWKV_A_RANK
    mu_main = jnp.concatenate([jnp.zeros((pool_w,), F32), shift_mu[:s0]]).reshape(1, -1)
    mu_lora = jnp.concatenate([
        shift_mu[s0:s1], jnp.zeros((LANES - RWKV_W_RANK,), F32), shift_mu[s1:s2],
        jnp.zeros((LANES - RWKV_A_RANK,), F32), shift_mu[s2:]]).reshape(1, -1)
    w2p = jnp.pad(w2, ((0, LANES - RWKV_W_RANK), (0, 0)))
    a2p = jnp.pad(a2, ((0, LANES - RWKV_A_RANK), (0, 0)))
    return c0, w_lora, mu_main, mu_lora, w2p, a2p, pool_w


def kernel(x, meta_tokens, mix_norm_gain, ffn_norm_gain, final_norm_gain, w_in_even, pool_w, pool_scale, shift_mu, rwkv_w0, rwkv_w2, rwkv_a0, rwkv_a2, rwkv_g2, rwkv_k_k, rwkv_k_a, rwkv_r_k, rwkv_gn_gain, rwkv_gn_bias, w_out_even, w_in_odd, ssm_lam_re, ssm_lam_im, ssm_log_dt, ssm_b_re, ssm_b_im, ssm_c_re, ssm_c_im, ssm_d, w_glu, peer_w_q, peer_sub_keys, peer_u, peer_v):
    bsz, seq, d = x.shape
    depth = mix_norm_gain.shape[0]
    real = N_META + seq
    length = -(-real // SEQ_ALIGN) * SEQ_ALIGN
    meta = jnp.broadcast_to(meta_tokens[None].astype(x.dtype), (bsz, N_META, d))
    h = jnp.concatenate([meta, x, jnp.zeros((bsz, length - real, d), x.dtype)], axis=1)
    h = h.reshape(bsz * length, d)
    peer_u_bf = peer_u.astype(BF16)
    peer_v_bf = peer_v.astype(BF16)

    pending = None
    for layer in range(depth):
        if pending is None:
            hn = rms_norm_bf16(h, mix_norm_gain[layer])
        else:
            hn, h = rms_norm_bf16(h, mix_norm_gain[layer], delta=pending)
        if layer % 2 == 0:
            i = layer // 2
            width = rwkv_w0.shape[-1]
            main_cols, w_lora, mu_main, mu_lora, w2p, a2p, pool_cols = _even_layer_weights(
                w_in_even[i], shift_mu[i], rwkv_w2[i], rwkv_a2[i], width)
            z3 = matmul(hn, w_in_even, i, n=main_cols).reshape(bsz, length, -1)
            zl3 = matmul(hn, w_lora, 0).reshape(bsz, length, -1)
            y_pool = pool_mixer(z3, pool_w[i], pool_scale[i])
            ob = pool_cols // 512
            wb = width // 512
            r, k, v, kap, b, lw, g = rwkv_prep(
                z3, zl3, (ob, ob + wb, ob + 2 * wb), mu_main, mu_lora, rwkv_w0[i], w2p,
                rwkv_a0[i], a2p, rwkv_g2[i], rwkv_k_k[i], rwkv_k_a[i])
            y_rwkv = rwkv_scan(r, k, v, kap, b, lw, g, rwkv_r_k[i], rwkv_gn_gain[i],
                               rwkv_gn_bias[i])
            h = proj2_residual(y_pool.reshape(bsz * length, -1), y_rwkv.reshape(bsz * length, -1),
                               w_out_even, i, h)
        else:
            j = layer // 2
            u = matmul(hn, w_in_odd, j, out_dtype=BF16)
            y = s5_mixer(u.reshape(bsz, length, -1), ssm_lam_re[j], ssm_lam_im[j], ssm_log_dt[j],
                         ssm_b_re[j], ssm_b_im[j], ssm_c_re[j], ssm_c_im[j], ssm_d[j])
            h = glu_residual(y, w_glu, j, h)
        if layer + 1 < depth:
            pending = peer_ffn(h, ffn_norm_gain[layer], peer_w_q, peer_sub_keys[layer], peer_u_bf,
                               peer_v_bf, layer)
    h3 = h.reshape(bsz, length, d)
    last = depth - 1
    pending = peer_ffn(h3, ffn_norm_gain[last], peer_w_q, peer_sub_keys[last], peer_u_bf, peer_v_bf,
                       last, window=(N_META, seq))
    return final_norm(h3, pending.reshape(bsz, seq, d), final_norm_gain, N_META)
```

```python
import functools

import jax
import jax.numpy as jnp
from jax import lax
from jax.experimental import pallas as pl
from jax.experimental.pallas import tpu as pltpu

F32 = jnp.float32
BF16 = jnp.bfloat16

NORM_EPS = 1e-6
N_META = 16
POOL_WINDOWS = (2, 4, 8, 16)
RWKV_HEAD = 64
RWKV_W_RANK = 96
RWKV_A_RANK = 96
RWKV_G_RANK = 256
RWKV_GN_EPS = 64e-5
RWKV_CHUNK = 64
SSM_GROUP = 16
SSM_STATE = 64
SSM_CHUNK = 16
PEER_KEYS = 128
PEER_HEADS = 8
PEER_TOPK = 16
LANES = 128
SEQ_ALIGN = 128

NEG_INF = float("-inf")


def _params(sem, vmem_mb):
    return pltpu.CompilerParams(dimension_semantics=sem, vmem_limit_bytes=vmem_mb << 20)


def _split2(x):
    hi = x.astype(BF16)
    lo = (x - hi.astype(F32)).astype(BF16)
    return hi, lo


def _dot(a, b):
    return jnp.dot(a, b, preferred_element_type=F32)


def _dot_nt(a, b):
    return lax.dot_general(a, b, (((1,), (1,)), ((), ())), preferred_element_type=F32)


def _dot_acc(a, b):
    ah, al = _split2(a)
    bh, bl = _split2(b)
    return _dot(ah, bh) + _dot(al, bh) + _dot(ah, bl)


def _dot_nt_acc(a, b):
    ah, al = _split2(a)
    bh, bl = _split2(b)
    return _dot_nt(ah, bh) + _dot_nt(al, bh) + _dot_nt(ah, bl)


def _gelu(x):
    return 0.5 * x * (1.0 + lax.erf(x * 0.7071067811865476))


def _sigmoid(x):
    return 1.0 / (1.0 + jnp.exp(-x))


def _pick_tile(n, prefs):
    for t in prefs:
        if n % t == 0:
            return t
    return n


def _rms_norm_body(*refs, with_transpose, with_delta, windowed=False):
    refs = list(refs)
    x = refs.pop(0)[0] if windowed else refs.pop(0)[...]
    if with_delta:
        x = x + refs.pop(0)[...]
    g_ref, o_ref = refs.pop(0), refs.pop(0)
    inv = lax.rsqrt(jnp.mean(x * x, axis=-1, keepdims=True) + NORM_EPS)
    xn = (x * inv) * g_ref[...]
    o_ref[...] = xn.astype(BF16)
    if with_transpose:
        refs.pop(0)[...] = xn.T.astype(BF16)
    if with_delta:
        refs.pop(0)[...] = x


def rms_norm_bf16(x, gain, *, delta=None, with_transpose=False):
    t, d = x.shape
    tm = _pick_tile(t, (256, 128, 64, 32, 16, 8))
    row = pl.BlockSpec((tm, d), lambda i: (i, 0))
    args = [x] + ([delta] if delta is not None else []) + [gain.reshape(1, d)]
    in_specs = [row] * (len(args) - 1) + [pl.BlockSpec((1, d), lambda i: (0, 0))]
    out_shape = [jax.ShapeDtypeStruct((t, d), BF16)]
    out_specs = [row]
    if with_transpose:
        out_shape.append(jax.ShapeDtypeStruct((d, t), BF16))
        out_specs.append(pl.BlockSpec((d, tm), lambda i: (0, i)))
    if delta is not None:
        out_shape.append(jax.ShapeDtypeStruct((t, d), F32))
        out_specs.append(row)
    res = pl.pallas_call(
        functools.partial(_rms_norm_body, with_transpose=with_transpose,
                          with_delta=delta is not None),
        grid=(t // tm,),
        in_specs=in_specs,
        out_specs=out_specs,
        out_shape=out_shape,
        compiler_params=_params(("parallel",), 48),
        name="rms_norm",
    )(*args)
    return res if len(res) > 1 else res[0]


def rms_norm_rows(x3, gain, first, count):
    bsz, _, d = x3.shape
    tm = _pick_tile(count, (256, 128, 64, 32, 16, 8))
    nt = count // tm
    assert first % 8 == 0
    window = pl.BlockSpec((pl.Element(1), pl.Element(tm), pl.Element(d)),
                          lambda b, i: (b, pl.multiple_of(first + i * tm, 8), 0))
    return pl.pallas_call(
        functools.partial(_rms_norm_body, with_transpose=True, with_delta=False, windowed=True),
        grid=(bsz, nt),
        in_specs=[window, pl.BlockSpec((1, d), lambda b, i: (0, 0))],
        out_specs=[pl.BlockSpec((tm, d), lambda b, i: (b * nt + i, 0)),
                   pl.BlockSpec((d, tm), lambda b, i: (0, b * nt + i))],
        out_shape=[jax.ShapeDtypeStruct((bsz * count, d), BF16),
                   jax.ShapeDtypeStruct((d, bsz * count), BF16)],
        compiler_params=_params(("parallel", "parallel"), 48),
        name="rms_norm_rows",
    )(x3, gain.reshape(1, d))


def _row_tile(t):
    for cand in (1088, 1024, 768, 512, 384, 256, 128, 64, 32, 16):
        if t % cand == 0:
            return cand
    return t


def _matmul_body(a_ref, w_ref, o_ref):
    o_ref[...] = _dot(a_ref[...], w_ref[...].astype(BF16)).astype(o_ref.dtype)


def matmul(a, w_all, layer, *, n=None, out_dtype=F32):
    t, k = a.shape
    n = w_all.shape[2] if n is None else n
    tm = _row_tile(t)
    tn = _pick_tile(n, (512, 256, 128))
    return pl.pallas_call(
        _matmul_body,
        grid=(t // tm, n // tn),
        in_specs=[pl.BlockSpec((tm, k), lambda i, j: (i, 0)),
                  pl.BlockSpec((None, k, tn), lambda i, j: (layer, 0, j))],
        out_specs=pl.BlockSpec((tm, tn), lambda i, j: (i, j)),
        out_shape=jax.ShapeDtypeStruct((t, n), out_dtype),
        compiler_params=_params(("parallel", "arbitrary"), 56),
        name="matmul",
    )(a, w_all)


def _proj2_body(a1_ref, a2_ref, w1_ref, w2_ref, h_ref, o_ref):
    o_ref[...] = (h_ref[...] + _dot(a1_ref[...], w1_ref[...].astype(BF16))
                  + _dot(a2_ref[...], w2_ref[...].astype(BF16)))


def proj2_residual(a1, a2, w_all, layer, h):
    t, k1 = a1.shape
    k2 = a2.shape[1]
    n = w_all.shape[2]
    assert k1 == k2
    tm = _row_tile(t)
    tn = _pick_tile(n, (512, 256, 128))
    return pl.pallas_call(
        _proj2_body,
        grid=(t // tm, n // tn),
        in_specs=[
            pl.BlockSpec((tm, k1), lambda i, j: (i, 0)),
            pl.BlockSpec((tm, k2), lambda i, j: (i, 0)),
            pl.BlockSpec((None, k1, tn), lambda i, j: (layer, 0, j)),
            pl.BlockSpec((None, k2, tn), lambda i, j: (layer, 1, j)),
            pl.BlockSpec((tm, tn), lambda i, j: (i, j)),
        ],
        out_specs=pl.BlockSpec((tm, tn), lambda i, j: (i, j)),
        out_shape=jax.ShapeDtypeStruct((t, n), F32),
        compiler_params=_params(("parallel", "arbitrary"), 56),
        name="proj2_residual",
    )(a1, a2, w_all, w_all, h)


def _glu_body(y_ref, wa_ref, wb_ref, h_ref, o_ref):
    y = y_ref[...]
    ga = _dot(y, wa_ref[...].astype(BF16))
    gb = _dot(y, wb_ref[...].astype(BF16))
    o_ref[...] = h_ref[...] + ga * _sigmoid(gb)


def glu_residual(y, w_all, layer, h):
    t, k = y.shape
    n = w_all.shape[2] // 2
    tm = _row_tile(t)
    tn = _pick_tile(n, (512, 256, 128))
    nb = n // tn
    return pl.pallas_call(
        _glu_body,
        grid=(t // tm, nb),
        in_specs=[
            pl.BlockSpec((tm, k), lambda i, j: (i, 0)),
            pl.BlockSpec((None, k, tn), lambda i, j: (layer, 0, j)),
            pl.BlockSpec((None, k, tn), lambda i, j: (layer, 0, j + nb)),
            pl.BlockSpec((tm, tn), lambda i, j: (i, j)),
        ],
        out_specs=pl.BlockSpec((tm, tn), lambda i, j: (i, j)),
        out_shape=jax.ShapeDtypeStruct((t, n), F32),
        compiler_params=_params(("parallel", "arbitrary"), 56),
        name="glu_residual",
    )(y, w_all, w_all, h)


def _pool_body(z_ref, w_ref, s_ref, o_ref):
    grp = pl.program_id(1)
    z = z_ref[...]
    length = z.shape[0]
    row = lax.broadcasted_iota(jnp.int32, (length, 1), 0)
    posf = (row + 1).astype(F32)

    def shifted(x, k):
        return jnp.where(row >= k, pltpu.roll(x, k, axis=0), 0.0)

    for gi, win in enumerate(POOL_WINDOWS):

        @pl.when(grp == gi)
        def _(win=win):
            acc = z
            span = 1
            while span < win:
                acc = acc + shifted(acc, span)
                span *= 2
            cnt = jnp.minimum(posf, float(win))
            m = acc / cnt - z
            y = _dot(m.astype(BF16), w_ref[...])
            o_ref[...] = (y * s_ref[...]).astype(o_ref.dtype)


def pool_mixer(z3, pool_w, pool_scale):
    bsz, length, _ = z3.shape
    ng, c, _ = pool_w.shape
    return pl.pallas_call(
        _pool_body,
        grid=(bsz, ng),
        in_specs=[
            pl.BlockSpec((None, length, c), lambda b, g: (b, 0, g)),
            pl.BlockSpec((None, c, c), lambda b, g: (g, 0, 0)),
            pl.BlockSpec((1, c), lambda b, g: (0, g)),
        ],
        out_specs=pl.BlockSpec((None, length, c), lambda b, g: (b, 0, g)),
        out_shape=jax.ShapeDtypeStruct((bsz, length, ng * c), BF16),
        compiler_params=_params(("parallel", "arbitrary"), 48),
        name="pool_mixer",
    )(z3, pool_w.astype(BF16), pool_scale.reshape(1, ng * c))


def _head_sum_matrix(width, head):
    shift = head.bit_length() - 1
    r = lax.broadcasted_iota(jnp.int32, (width, width), 0) >> shift
    c = lax.broadcasted_iota(jnp.int32, (width, width), 1) >> shift
    return jnp.where(r == c, 1.0, 0.0).astype(BF16)


def _head_sums(x, ones_bd):
    hi, lo = _split2(x)
    return _dot(hi, ones_bd) + _dot(lo, ones_bd)


def _rwkv_prep_body(zr_ref, zk_ref, zv_ref, zl_ref, pr_ref, pk_ref, pv_ref, pl_ref,
                    mur_ref, muk_ref, muv_ref, mul_ref, w0_ref, w2_ref, a0_ref, a2_ref, g2_ref,
                    kk_ref, ka_ref,
                    r_out, k_out, v_out, kap_out, b_out, lw_out, g_out):
    first = pl.program_id(1) == 0
    tl = zr_ref.shape[0]
    row = lax.broadcasted_iota(jnp.int32, (tl, 1), 0)

    def mix(z_ref, p_ref, mu_ref):
        z = z_ref[...]
        prev_last = jnp.where(first, 0.0, p_ref[7:8, :])
        prev = jnp.where(row == 0, prev_last, pltpu.roll(z, 1, axis=0))
        return z + (prev - z) * mu_ref[...]

    r = mix(zr_ref, pr_ref, mur_ref)
    k = mix(zk_ref, pk_ref, muk_ref)
    v = mix(zv_ref, pv_ref, muv_ref)
    lora = mix(zl_ref, pl_ref, mul_ref)
    w_lr = lora[:, 0:LANES]
    a_lr = lora[:, LANES:2 * LANES]
    g_lr = lora[:, 2 * LANES:]

    x = w0_ref[...] + _dot_acc(jnp.tanh(w_lr), w2_ref[...])
    y = -x
    softplus = jnp.maximum(y, 0.0) + jnp.log(1.0 + jnp.exp(-jnp.abs(y)))
    lw = -jnp.exp(-softplus - 0.5)
    a = _sigmoid(a0_ref[...] + _dot_acc(a_lr, a2_ref[...]))
    g = _dot_acc(_sigmoid(g_lr), g2_ref[...])

    ones_bd = _head_sum_matrix(LANES, RWKV_HEAD)
    kk = k * kk_ref[...]
    kk2 = kk * kk
    width = k.shape[1]
    ss = jnp.concatenate(
        [_head_sums(kk2[:, c:c + LANES], ones_bd) for c in range(0, width, LANES)], axis=1)
    kap = kk / jnp.maximum(jnp.sqrt(ss), 1e-12)

    r_out[...] = r.astype(r_out.dtype)
    k_out[...] = (k * (1.0 + (a - 1.0) * ka_ref[...])).astype(k_out.dtype)
    v_out[...] = v.astype(v_out.dtype)
    kap_out[...] = kap.astype(kap_out.dtype)
    b_out[...] = (kap * a).astype(b_out.dtype)
    lw_out[...] = lw
    g_out[...] = g.astype(g_out.dtype)


def rwkv_prep(z3, zl3, off_blocks, mu, mu_l, w0, w2p, a0, a2p, g2, k_k, k_a):
    bsz, length, _ = z3.shape
    width = w0.shape[-1]
    cw = 512
    ncb = width // cw
    tl = _pick_tile(length, (352, 272, 192, 176, 136, 128, 64))
    ob_r, ob_k, ob_v = off_blocks
    ob_l = 0
    t8 = tl // 8

    def cur(ob, fixed=False):
        if fixed:
            return pl.BlockSpec((None, tl, cw), lambda b, i, c: (b, i, ob))
        return pl.BlockSpec((None, tl, cw), lambda b, i, c: (b, i, ob + c))

    def prev(ob, fixed=False):
        if fixed:
            return pl.BlockSpec((None, 8, cw), lambda b, i, c: (b, jnp.maximum(i * t8 - 1, 0), ob))
        return pl.BlockSpec((None, 8, cw), lambda b, i, c: (b, jnp.maximum(i * t8 - 1, 0), ob + c))

    def vec(ob, fixed=False):
        if fixed:
            return pl.BlockSpec((1, cw), lambda b, i, c: (0, ob))
        return pl.BlockSpec((1, cw), lambda b, i, c: (0, ob + c))

    colvec = pl.BlockSpec((1, cw), lambda b, i, c: (0, c))
    out_spec = pl.BlockSpec((None, tl, cw), lambda b, i, c: (b, i, c))
    out_dtypes = [BF16, BF16, BF16, BF16, BF16, F32, BF16]
    return pl.pallas_call(
        _rwkv_prep_body,
        grid=(bsz, length // tl, ncb),
        in_specs=[
            cur(ob_r), cur(ob_k), cur(ob_v), cur(ob_l, True),
            prev(ob_r), prev(ob_k), prev(ob_v), prev(ob_l, True),
            vec(ob_r), vec(ob_k), vec(ob_v), vec(ob_l, True),
            colvec,
            pl.BlockSpec((LANES, cw), lambda b, i, c: (0, c)),
            colvec,
            pl.BlockSpec((LANES, cw), lambda b, i, c: (0, c)),
            pl.BlockSpec((RWKV_G_RANK, cw), lambda b, i, c: (0, c)),
            colvec, colvec,
        ],
        out_specs=[out_spec] * 7,
        out_shape=[jax.ShapeDtypeStruct((bsz, length, width), dt) for dt in out_dtypes],
        compiler_params=_params(("parallel", "parallel", "arbitrary"), 56),
        name="rwkv_prep",
    )(z3, z3, z3, zl3, z3, z3, z3, zl3, mu, mu, mu, mu_l,
      w0.reshape(1, width), w2p, a0.reshape(1, width), a2p, g2,
      k_k.reshape(1, width), k_a.reshape(1, width))


def _rwkv_scan_body(r_ref, k_ref, v_ref, kap_ref, b_ref, lw_ref, g_ref, rk_ref, gain_ref, bias_ref,
                    o_ref, h_ref, *, pairs):
    cn = RWKV_CHUNK
    hd = RWKV_HEAD

    @pl.when(pl.program_id(2) == 0)
    def _():
        h_ref[...] = jnp.zeros_like(h_ref)

    lane = lax.broadcasted_iota(jnp.int32, (1, LANES), 1)
    m0 = lane < hd
    rowi = lax.broadcasted_iota(jnp.int32, (cn, LANES), 0)
    coli = lax.broadcasted_iota(jnp.int32, (cn, LANES), 1) & (hd - 1)
    strict = rowi > coli
    incl = rowi >= coli
    eye2 = jnp.where(rowi == coli, 1.0, 0.0)
    tri = jnp.where(lax.broadcasted_iota(jnp.int32, (cn, cn), 0)
                    >= lax.broadcasted_iota(jnp.int32, (cn, cn), 1), 1.0, 0.0).astype(BF16)
    br = lax.broadcasted_iota(jnp.int32, (LANES, LANES), 0) >> (hd.bit_length() - 1)
    bc = lax.broadcasted_iota(jnp.int32, (LANES, LANES), 1) >> (hd.bit_length() - 1)
    bd_mask = br == bc
    ones_bd = jnp.where(bd_mask, 1.0, 0.0).astype(BF16)

    def stk(x):
        return jnp.concatenate([jnp.where(m0, x, 0.0), jnp.where(m0, 0.0, x)], axis=0).astype(BF16)

    ps = range(pairs)
    sls = [slice(p * LANES, (p + 1) * LANES) for p in ps]
    r = [r_ref[:, s].astype(F32) for s in sls]
    k = [k_ref[:, s].astype(F32) for s in sls]
    v = [v_ref[:, s].astype(F32) for s in sls]
    kap = [kap_ref[:, s].astype(F32) for s in sls]
    b = [b_ref[:, s].astype(F32) for s in sls]
    lw = [lw_ref[:, s] for s in sls]
    hbd = [h_ref[p] for p in ps]

    lsp = [_split2(x) for x in lw]
    lp = [_dot(tri, a) + _dot(tri, c) for a, c in lsp]
    lpc = [x[cn - 1:cn, :] for x in lp]
    pinv = [jnp.exp(-x) for x in lp]
    pend = [jnp.exp(c - x) for c, x in zip(lpc, lp)]
    kt = [kap[p] * jnp.exp(lp[p] - lw[p]) for p in ps]
    rt = [r[p] * jnp.exp(lp[p]) for p in ps]
    kh = [k[p] * pinv[p] for p in ps]
    bh = [b[p] * pinv[p] for p in ps]
    khc = [k[p] * pend[p] for p in ps]
    bhc = [b[p] * pend[p] for p in ps]

    ktrt = [jnp.concatenate([kt[p], rt[p]], axis=0).astype(BF16) for p in ps]
    ab_all = [_dot_nt(ktrt[p], stk(bh[p])) for p in ps]
    ak_all = [_dot_nt(ktrt[p], stk(kh[p])) for p in ps]
    n_mat = [jnp.where(strict, x[:cn], 0.0) for x in ab_all]
    a_k = [jnp.where(strict, x[:cn], 0.0) for x in ak_all]
    a_rb = [jnp.where(incl, x[cn:], 0.0) for x in ab_all]
    a_rk = [jnp.where(incl, x[cn:], 0.0) for x in ak_all]

    t_mat = [eye2 - x for x in n_mat]
    pw = n_mat
    span = 1
    while 2 * span < cn:
        pw = [_dot(x.astype(BF16), stk(x)) for x in pw]
        t_mat = [t + _dot(t.astype(BF16), stk(x)) for t, x in zip(t_mat, pw)]
        span *= 2
    t_bf = [t.astype(BF16) for t in t_mat]

    kbar = [_dot(t_bf[p], stk(kt[p])) for p in ps]
    akv = [_dot(a_k[p].astype(BF16), stk(v[p])) for p in ps]
    ubar = [_dot(t_bf[p], stk(akv[p])) for p in ps]
    hb = [x.astype(BF16) for x in hbd]
    u = [_dot(kbar[p].astype(BF16), hb[p]) + ubar[p] for p in ps]
    y = [_dot(rt[p].astype(BF16), hb[p]) + _dot(a_rk[p].astype(BF16), stk(v[p]))
         - _dot(a_rb[p].astype(BF16), stk(u[p])) for p in ps]

    kb_t = [jnp.concatenate([khc[p], -bhc[p]], axis=0).T.astype(BF16) for p in ps]
    vu = [jnp.concatenate([v[p], u[p]], axis=0).astype(BF16) for p in ps]
    g_mat = [_dot(kb_t[p], vu[p]) for p in ps]
    pc_col = [jnp.exp(jnp.broadcast_to(x, (LANES, LANES)).T) for x in lpc]
    for p in ps:
        h_ref[p] = jnp.where(bd_mask, pc_col[p] * hbd[p] + g_mat[p], 0.0)

    mean = [_head_sums(x, ones_bd) * (1.0 / hd) for x in y]
    yc = [y[p] - mean[p] for p in ps]
    var = [_head_sums(x * x, ones_bd) * (1.0 / hd) for x in yc]
    bonus = [_head_sums(r[p] * k[p] * rk_ref[:, sls[p]], ones_bd) * v[p] for p in ps]
    for p in ps:
        yn = yc[p] * lax.rsqrt(var[p] + RWKV_GN_EPS) * gain_ref[:, sls[p]] + bias_ref[:, sls[p]]
        o_ref[:, sls[p]] = ((yn + bonus[p]) * g_ref[:, sls[p]].astype(F32)).astype(o_ref.dtype)


def rwkv_scan(r, k, v, kap, b, lw, g, r_k, gn_gain, gn_bias):
    bsz, length, width = r.shape
    cw = _pick_tile(width, (2048, 1024, 512))
    pairs = cw // LANES
    seq = pl.BlockSpec((None, RWKV_CHUNK, cw), lambda bi, c, t: (bi, t, c))
    vec = pl.BlockSpec((1, cw), lambda bi, c, t: (0, c))
    return pl.pallas_call(
        functools.partial(_rwkv_scan_body, pairs=pairs),
        grid=(bsz, width // cw, length // RWKV_CHUNK),
        in_specs=[seq] * 7 + [vec] * 3,
        out_specs=seq,
        out_shape=jax.ShapeDtypeStruct((bsz, length, width), BF16),
        scratch_shapes=[pltpu.VMEM((pairs, LANES, LANES), F32)],
        compiler_params=_params(("parallel", "parallel", "arbitrary"), 48),
        name="rwkv_scan",
    )(r, k, v, kap, b, lw, g, r_k.reshape(1, width), gn_gain.reshape(1, width),
      gn_bias.reshape(1, width))


def _s5_param_body(lr_ref, li_ref, ldt_ref, btr_ref, bti_ref, cr_ref, ci_ref,
                   kst_ref, etr_ref, eti_ref, car_ref, cai_ref, a16r_ref, a16i_ref, *, groups):
    nc = SSM_CHUNK
    npow = 24
    tau = lax.broadcasted_iota(jnp.int32, (npow, SSM_STATE), 0).astype(F32)
    for gidx in range(groups):
        lr = lr_ref[gidx]
        li = li_ref[gidx]
        dt = jnp.exp(ldt_ref[gidx])
        mag = jnp.exp(tau * (lr * dt))
        ang = tau * (li * dt)
        pr = mag * jnp.cos(ang)
        pi = mag * jnp.sin(ang)
        ar = pr[1:2]
        ai = pi[1:2]
        den = lr * lr + li * li
        fr = ((ar - 1.0) * lr + ai * li) / den
        fi = (ai * lr - (ar - 1.0) * li) / den
        btr = btr_ref[gidx]
        bti = bti_ref[gidx]
        bbr = fr * btr - fi * bti
        bbi = fr * bti + fi * btr
        cr = cr_ref[gidx]
        ci = ci_ref[gidx]
        etr = jnp.concatenate([pr[s:s + 1] * bbr - pi[s:s + 1] * bbi for s in range(nc)], axis=0)
        eti = jnp.concatenate([pr[s:s + 1] * bbi + pi[s:s + 1] * bbr for s in range(nc)], axis=0)
        car = jnp.concatenate([pr[s:s + 1] * cr - pi[s:s + 1] * ci for s in range(1, nc + 1)], axis=0)
        cai = jnp.concatenate([pr[s:s + 1] * ci + pi[s:s + 1] * cr for s in range(1, nc + 1)], axis=0)
        kst_ref[gidx] = _dot_nt_acc(etr, cr) - _dot_nt_acc(eti, ci)
        etr_ref[gidx] = etr
        eti_ref[gidx] = eti
        car_ref[gidx] = car
        cai_ref[gidx] = cai
        a16r_ref[gidx] = jnp.broadcast_to(pr[nc:nc + 1], (8, SSM_STATE))
        a16i_ref[gidx] = jnp.broadcast_to(pi[nc:nc + 1], (8, SSM_STATE))


def s5_params(lam_re, lam_im, log_dt, b_re, b_im, c_re, c_im):
    ng, ns = lam_re.shape
    gc = SSM_GROUP
    gb = 8
    rows = SSM_CHUNK * gc

    def spec(*dims):
        return pl.BlockSpec((gb,) + dims, lambda i: (i,) + (0,) * len(dims))

    outs = [((rows, gc), F32)] + [((rows, ns), F32)] * 4 + [((8, ns), F32)] * 2
    return pl.pallas_call(
        functools.partial(_s5_param_body, groups=gb),
        grid=(ng // gb,),
        in_specs=[spec(1, ns), spec(1, ns), spec(1, 1), spec(gc, ns), spec(gc, ns), spec(gc, ns),
                  spec(gc, ns)],
        out_specs=[spec(*s) for s, _ in outs],
        out_shape=[jax.ShapeDtypeStruct((ng,) + s, dt) for s, dt in outs],
        compiler_params=_params(("parallel",), 32),
        name="s5_params",
    )(lam_re.reshape(ng, 1, ns), lam_im.reshape(ng, 1, ns), log_dt.reshape(ng, 1, 1),
      jnp.swapaxes(b_re, 1, 2), jnp.swapaxes(b_im, 1, 2), c_re, c_im)


def _s5_body(x_ref, toep_ref, wsr_ref, wsi_ref, wor_ref, woi_ref, a16r_ref, a16i_ref, d_ref,
             o_ref, sre_ref, sim_ref, xre_ref, xim_ref):
    rows = x_ref.shape[1]
    nchunk = rows // 8
    x0 = x_ref[0]
    x1 = x_ref[1]
    sre_ref[...] = _dot(x0, wsr_ref[0]) + _dot(x1, wsr_ref[1])
    sim_ref[...] = _dot(x0, wsi_ref[0]) + _dot(x1, wsi_ref[1])
    ar = a16r_ref[...]
    ai = a16i_ref[...]

    def step(c, carry):
        xr, xi = carry
        off = pl.multiple_of(c * 8, 8)
        xre_ref[pl.ds(off, 8), :] = xr
        xim_ref[pl.ds(off, 8), :] = xi
        sr = sre_ref[pl.ds(off, 8), :]
        si = sim_ref[pl.ds(off, 8), :]
        return ar * xr - ai * xi + sr, ar * xi + ai * xr + si

    zero = jnp.zeros((8, LANES), F32)
    lax.fori_loop(0, nchunk, step, (zero, zero))
    xr = xre_ref[...].astype(BF16)
    xi = xim_ref[...].astype(BF16)
    for q, xq in enumerate((x0, x1)):
        y = (_dot(xq, toep_ref[q]) + _dot(xr, wor_ref[q]) + _dot(xi, woi_ref[q])
             + xq.astype(F32) * d_ref[q])
        o_ref[q] = _gelu(y).astype(o_ref.dtype)


def s5_apply(xg, toep, wsr, wsi, wor, woi, a16r, a16i, dflat):
    ng, rows, cols = xg.shape

    def spec(*dims):
        return pl.BlockSpec((2,) + dims, lambda i: (i,) + (0,) * len(dims))

    return pl.pallas_call(
        _s5_body,
        grid=(ng // 2,),
        in_specs=[spec(rows, cols), spec(cols, cols), spec(cols, LANES), spec(cols, LANES),
                  spec(LANES, cols), spec(LANES, cols),
                  pl.BlockSpec((None, 8, LANES), lambda i: (i, 0, 0)),
                  pl.BlockSpec((None, 8, LANES), lambda i: (i, 0, 0)),
                  spec(1, cols)],
        out_specs=spec(rows, cols),
        out_shape=jax.ShapeDtypeStruct((ng, rows, cols), BF16),
        scratch_shapes=[pltpu.VMEM((rows, LANES), F32)] * 4,
        compiler_params=_params(("parallel",), 32),
        name="s5_apply",
    )(xg, toep, wsr, wsi, wor, woi, a16r, a16i, dflat)


def s5_mixer(u3, lam_re, lam_im, log_dt, b_re, b_im, c_re, c_im, d_skip):
    bsz, length, width = u3.shape
    ng, ns = lam_re.shape
    gc = SSM_GROUP
    nc = SSM_CHUNK
    kst, etr, eti, car, cai, a16r, a16i = s5_params(lam_re, lam_im, log_dt, b_re, b_im, c_re, c_im)

    k4 = kst.reshape(ng, nc, gc, gc)
    s_idx = jnp.arange(nc)[:, None]
    t_idx = jnp.arange(nc)[None, :]
    lag = t_idx - s_idx
    toep = jnp.where((lag >= 0)[None, :, :, None, None], k4[:, jnp.clip(lag, 0, nc - 1)], 0.0)
    toep = toep.transpose(0, 1, 3, 2, 4).reshape(ng, nc * gc, nc * gc).astype(BF16)
    odd = (jnp.arange(ng) % 2 == 1)[:, None, None]

    def lanes_by_parity(w):
        z = jnp.zeros_like(w)
        return jnp.where(odd, jnp.concatenate([z, w], -1), jnp.concatenate([w, z], -1))

    def flip_lag(e):
        return e.reshape(ng, nc, gc, ns)[:, ::-1].reshape(ng, nc * gc, ns)

    wsr = lanes_by_parity(flip_lag(etr)).astype(BF16)
    wsi = lanes_by_parity(flip_lag(eti)).astype(BF16)
    wor = jnp.swapaxes(lanes_by_parity(car), 1, 2).astype(BF16)
    woi = jnp.swapaxes(lanes_by_parity(-cai), 1, 2).astype(BF16)

    def pair_lanes(a):
        return a.reshape(ng // 2, 2, 8, ns).transpose(0, 2, 1, 3).reshape(ng // 2, 8, 2 * ns)

    dflat = jnp.tile(d_skip.reshape(ng, 1, gc), (1, 1, nc))

    nchunk = length // nc
    x = u3.reshape(bsz, nchunk, nc, ng, gc).transpose(3, 1, 0, 2, 4)
    x = jnp.pad(x, ((0, 0), (0, 0), (0, 8 - bsz), (0, 0), (0, 0))).reshape(ng, nchunk * 8, nc * gc)
    y = s5_apply(x, toep, wsr, wsi, wor, woi, pair_lanes(a16r), pair_lanes(a16i), dflat)
    y = y.reshape(ng, nchunk, 8, nc, gc)[:, :, :bsz].transpose(2, 1, 3, 0, 4)
    return y.reshape(bsz * length, width)


def _topk_rows(s, k):
    rows = lax.broadcasted_iota(jnp.int32, s.shape, 0).astype(F32)
    tops, idxs = [], []
    cur = s
    rank = jnp.full_like(s, float(PEER_KEYS - 1))
    for i in range(k):
        m = jnp.max(cur, axis=0, keepdims=True)
        first = jnp.min(jnp.where(cur >= m, rows, float(s.shape[0])), axis=0, keepdims=True)
        hit = rows == first
        tops.append(m)
        idxs.append(first)
        rank = jnp.where(hit, float(i), rank)
        cur = jnp.where(hit, NEG_INF, cur)
    return tops, idxs, rank


def _peer_route_body(q_ref, keys_ref, cnt_ref, e1_ref, r2_ref, e2_ref):
    nk = PEER_KEYS
    k = PEER_TOPK
    tm = q_ref.shape[0]
    wide, tall = 4, 3
    assert k == 16 and (wide + 1) * (tall + 1) > k and 2 * 9 > k
    a_idx = lax.broadcasted_iota(jnp.int32, (k, 1), 0)
    rho = lax.broadcasted_iota(jnp.int32, (k + 8 * (wide - 1) + k * tall, 1), 0)
    mid = rho - k
    col = rho - (k + 8 * (wide - 1))
    flat = jnp.where(rho < k, rho,
                     jnp.where(col < 0, (1 + (mid >> 3)) * k + (mid & 7),
                               (col & (k - 1)) * k + (col >> 4))).astype(F32)
    key_rows = lax.broadcasted_iota(jnp.int32, (nk, LANES), 0).astype(F32)
    for h in range(PEER_HEADS):
        s1_all = _dot_nt_acc(keys_ref[h, 0], q_ref[:, (2 * h) * nk:(2 * h + 1) * nk])
        s2_all = _dot_nt_acc(keys_ref[h, 1], q_ref[:, (2 * h + 1) * nk:(2 * h + 2) * nk])
        for c in range(tm // LANES):
            ts = slice(c * LANES, (c + 1) * LANES)
            s1 = s1_all[:, ts]
            s2 = s2_all[:, ts]
            t1, idx1, _ = _topk_rows(s1, k)
            t2, _, rank2 = _topk_rows(s2, k)
            top1 = jnp.concatenate(t1, axis=0)
            top2 = jnp.concatenate(t2, axis=0)
            row_blocks = [t1[0] + top2] + [t1[a] + top2[0:8] for a in range(1, wide)]
            col_blocks = [jnp.where(a_idx >= wide, top1 + t2[b], NEG_INF) for b in range(tall)]
            cand = jnp.concatenate(row_blocks + col_blocks, axis=0)
            best = t1[0] + t2[0]
            zsum = jnp.zeros_like(best)
            cur = cand
            taken = jnp.zeros_like(cand)
            for _ in range(k):
                m = jnp.max(cur, axis=0, keepdims=True)
                first = jnp.min(jnp.where(cur >= m, flat, float(k * k)), axis=0, keepdims=True)
                hit = flat == first
                zsum = zsum + jnp.exp(m - best)
                taken = jnp.where(hit, 1.0, taken)
                cur = jnp.where(hit, NEG_INF, cur)
            bounds = [0, k] + [k + 8 * a for a in range(1, wide)]
            kept_rows = [jnp.sum(taken[bounds[a]:bounds[a + 1]], axis=0, keepdims=True)
                         for a in range(wide)]
            base = bounds[-1]
            kept_cols = taken[base:base + k]
            for b in range(1, tall):
                kept_cols = kept_cols + taken[base + b * k:base + (b + 1) * k]
            cnt = jnp.zeros_like(s1)
            for a in range(k):
                kept_a = kept_rows[a] if a < wide else kept_cols[a:a + 1]
                cnt = jnp.where(key_rows == idx1[a], kept_a, cnt)
            cnt_ref[h, :, ts] = cnt
            r2_ref[h, :, ts] = rank2.astype(BF16)
            e1_ref[h, :, ts] = jnp.exp(s1 - t1[0]) / zsum
            e2_ref[h, :, ts] = jnp.exp(s2 - t2[0]).astype(BF16)


def peer_route(q, keys):
    t = q.shape[0]
    tm = _pick_tile(t, (256, 128))
    assert tm % LANES == 0, "the route kernel walks its token tile in 128-lane chunks"
    big = pl.BlockSpec((PEER_HEADS, PEER_KEYS, tm), lambda i: (0, 0, i))
    sds = jax.ShapeDtypeStruct((PEER_HEADS, PEER_KEYS, t), F32)
    sds_bf = jax.ShapeDtypeStruct((PEER_HEADS, PEER_KEYS, t), BF16)
    return pl.pallas_call(
        _peer_route_body,
        grid=(t // tm,),
        in_specs=[pl.BlockSpec((tm, q.shape[1]), lambda i: (i, 0)),
                  pl.BlockSpec(keys.shape, lambda i: (0, 0, 0, 0))],
        out_specs=[big, big, big, big],
        out_shape=[sds, sds, sds_bf, sds_bf],
        compiler_params=_params(("parallel",), 48),
        name="peer_route",
    )(q, keys)


PEER_SLAB = 256


PEER_TILE = 512
SUBLANES = 8


def _peer_dense_body(hnt_ref, u_ref, v_ref, r2_ref, e2_ref, cnt_ref, e1_ref, o_ref, *, te):
    nk = PEER_KEYS
    rows_per_tile = te // nk
    rows_per_slab = PEER_SLAB // nk
    nslab = te // PEER_SLAB
    j = pl.program_id(1)
    base = (j % (SUBLANES // rows_per_tile)) * rows_per_tile

    @pl.when(j == 0)
    def _():
        o_ref[...] = jnp.zeros_like(o_ref)

    def gates(s, hold):
        out = []
        zero = jnp.zeros((), BF16)
        for r in range(rows_per_slab):
            row = base + s * rows_per_slab + r
            gate = None
            for h in range(PEER_HEADS):
                cnt_row = cnt_ref[h, pl.ds(row, 1), :]
                if hold is not None:
                    cnt_row = cnt_row + hold
                keep = r2_ref[h] < cnt_row.astype(BF16)
                term = jnp.where(keep, e2_ref[h] * e1_ref[h, pl.ds(row, 1), :].astype(BF16), zero)
                gate = term if gate is None else gate + term
            out.append(gate.astype(F32))
        return out

    hnt = hnt_ref[...]
    acts = [_dot(u_ref[s * PEER_SLAB:(s + 1) * PEER_SLAB, :], hnt) for s in range(nslab)]
    total = None
    hold = None
    for s in range(nslab):
        gate_blocks = gates(s, hold)
        parts = [gate_blocks[r] * _gelu(acts[s][r * nk:(r + 1) * nk, :])
                 for r in range(rows_per_slab)]
        w = jnp.concatenate(parts, axis=0).T.astype(BF16)
        c = _dot(w, v_ref[s * PEER_SLAB:(s + 1) * PEER_SLAB, :])
        hold = 0.0 * parts[-1][0:1, :]
        total = c if total is None else total + c
    o_ref[...] += total


def peer_dense(hn_t, u_all, v_all, layer, rank2, e2, cnt, e1):
    d, t = hn_t.shape
    ne = u_all.shape[1]
    tm = _pick_tile(t, (512, 256, 128))
    te = min(PEER_TILE, ne)
    rows_per_tile = te // PEER_KEYS
    tiles_per_block = SUBLANES // rows_per_tile
    big = pl.BlockSpec((PEER_HEADS, PEER_KEYS, tm), lambda i, j: (0, 0, i))
    rows = pl.BlockSpec((PEER_HEADS, SUBLANES, tm), lambda i, j: (0, j // tiles_per_block, i))
    table = pl.BlockSpec((None, te, d), lambda i, j: (layer, j, 0))
    return pl.pallas_call(
        functools.partial(_peer_dense_body, te=te),
        grid=(t // tm, ne // te),
        in_specs=[pl.BlockSpec((d, tm), lambda i, j: (0, i)), table, table, big, big, rows, rows],
        out_specs=pl.BlockSpec((tm, d), lambda i, j: (i, 0)),
        out_shape=jax.ShapeDtypeStruct((t, d), F32),
        compiler_params=_params(("parallel", "arbitrary"), 60),
        name="peer_dense",
    )(hn_t, u_all, v_all, rank2, e2, cnt, e1)


def peer_ffn(h, gain, w_q_all, keys, u_all, v_all, layer, window=None):
    if window is None:
        hn, hn_t = rms_norm_bf16(h, gain, with_transpose=True)
    else:
        hn, hn_t = rms_norm_rows(h, gain, *window)
    q = matmul(hn, w_q_all, layer)
    cnt, e1, rank2, e2 = peer_route(q, keys)
    return peer_dense(hn_t, u_all, v_all, layer, rank2, e2, cnt, e1)


def _final_norm_body(x_ref, dx_ref, g_ref, o_ref):
    x = x_ref[0] + dx_ref[0]
    inv = lax.rsqrt(jnp.mean(x * x, axis=-1, keepdims=True) + NORM_EPS)
    o_ref[...] = (x * inv) * g_ref[...]


def final_norm(h3, delta3, gain, first):
    bsz, _, d = h3.shape
    count = delta3.shape[1]
    tm = _pick_tile(count, (256, 128, 64, 32, 16, 8))
    assert first % 8 == 0
    rows = pl.BlockSpec((pl.Element(1), pl.Element(tm), pl.Element(d)),
                        lambda b, i: (b, pl.multiple_of(first + i * tm, 8), 0))
    return pl.pallas_call(
        _final_norm_body,
        grid=(bsz, count // tm),
        in_specs=[rows, pl.BlockSpec((1, tm, d), lambda b, i: (b, i, 0)),
                  pl.BlockSpec((1, d), lambda b, i: (0, 0))],
        out_specs=pl.BlockSpec((None, tm, d), lambda b, i: (b, i, 0)),
        out_shape=jax.ShapeDtypeStruct((bsz, count, d), F32),
        compiler_params=_params(("parallel", "parallel"), 32),
        name="final_norm",
    )(h3, delta3, gain.reshape(1, d))


def _even_layer_weights(w_in, shift_mu, w2, a2, width):
    pool_w = w_in.shape[1] - shift_mu.shape[0]
    c0 = pool_w + 3 * width
    c1 = c0 + RWKV_W_RANK
    c2 = c1 + RWKV_A_RANK
    d = w_in.shape[0]
    zw = jnp.zeros((d, LANES - RWKV_W_RANK), w_in.dtype)
    za = jnp.zeros((d, LANES - RWKV_A_RANK), w_in.dtype)
    w_lora = jnp.concatenate([w_in[:, c0:c1], zw, w_in[:, c1:c2], za, w_in[:, c2:]], axis=1)[None]
    s0 = 3 * width
    s1 = s0 + RWKV_W_RANK
    s2 = s1 + RWKV_A_RANK
    mu_main = jnp.concatenate([jnp.zeros((pool_w,), F32), shift_mu[:s0]]).reshape(1, -1)
    mu_lora = jnp.concatenate([
        shift_mu[s0:s1], jnp.zeros((LANES - RWKV_W_RANK,), F32), shift_mu[s1:s2],
        jnp.zeros((LANES - RWKV_A_RANK,), F32), shift_mu[s2:]]).reshape(1, -1)
    w2p = jnp.pad(w2, ((0, LANES - RWKV_W_RANK), (0, 0)))
    a2p = jnp.pad(a2, ((0, LANES - RWKV_A_RANK), (0, 0)))
    return c0, w_lora, mu_main, mu_lora, w2p, a2p, pool_w


def kernel(x, meta_tokens, mix_norm_gain, ffn_norm_gain, final_norm_gain, w_in_even, pool_w, pool_scale, shift_mu, rwkv_w0, rwkv_w2, rwkv_a0, rwkv_a2, rwkv_g2, rwkv_k_k, rwkv_k_a, rwkv_r_k, rwkv_gn_gain, rwkv_gn_bias, w_out_even, w_in_odd, ssm_lam_re, ssm_lam_im, ssm_log_dt, ssm_b_re, ssm_b_im, ssm_c_re, ssm_c_im, ssm_d, w_glu, peer_w_q, peer_sub_keys, peer_u, peer_v):
    bsz, seq, d = x.shape
    depth = mix_norm_gain.shape[0]
    real = N_META + seq
    length = -(-real // SEQ_ALIGN) * SEQ_ALIGN
    meta = jnp.broadcast_to(meta_tokens[None].astype(x.dtype), (bsz, N_META, d))
    h = jnp.concatenate([meta, x, jnp.zeros((bsz, length - real, d), x.dtype)], axis=1)
    h = h.reshape(bsz * length, d)
    peer_u_bf = peer_u.astype(BF16)
    peer_v_bf = peer_v.astype(BF16)

    pending = None
    for layer in range(depth):
        if pending is None:
            hn = rms_norm_bf16(h, mix_norm_gain[layer])
        else:
            hn, h = rms_norm_bf16(h, mix_norm_gain[layer], delta=pending)
        if layer % 2 == 0:
            i = layer // 2
            width = rwkv_w0.shape[-1]
            main_cols, w_lora, mu_main, mu_lora, w2p, a2p, pool_cols = _even_layer_weights(
                w_in_even[i], shift_mu[i], rwkv_w2[i], rwkv_a2[i], width)
            z3 = matmul(hn, w_in_even, i, n=main_cols).reshape(bsz, length, -1)
            zl3 = matmul(hn, w_lora, 0).reshape(bsz, length, -1)
            y_pool = pool_mixer(z3, pool_w[i], pool_scale[i])
            ob = pool_cols // 512
            wb = width // 512
            r, k, v, kap, b, lw, g = rwkv_prep(
                z3, zl3, (ob, ob + wb, ob + 2 * wb), mu_main, mu_lora, rwkv_w0[i], w2p,
                rwkv_a0[i], a2p, rwkv_g2[i], rwkv_k_k[i], rwkv_k_a[i])
            y_rwkv = rwkv_scan(r, k, v, kap, b, lw, g, rwkv_r_k[i], rwkv_gn_gain[i],
                               rwkv_gn_bias[i])
            h = proj2_residual(y_pool.reshape(bsz * length, -1), y_rwkv.reshape(bsz * length, -1),
                               w_out_even, i, h)
        else:
            j = layer // 2
            u = matmul(hn, w_in_odd, j, out_dtype=BF16)
            y = s5_mixer(u.reshape(bsz, length, -1), ssm_lam_re[j], ssm_lam_im[j], ssm_log_dt[j],
                         ssm_b_re[j], ssm_b_im[j], ssm_c_re[j], ssm_c_im[j], ssm_d[j])
            h = glu_residual(y, w_glu, j, h)
        if layer + 1 < depth:
            pending = peer_ffn(h, ffn_norm_gain[layer], peer_w_q, peer_sub_keys[layer], peer_u_bf,
                               peer_v_bf, layer)
    h3 = h.reshape(bsz, length, d)
    last = depth - 1
    pending = peer_ffn(h3, ffn_norm_gain[last], peer_w_q, peer_sub_keys[last], peer_u_bf, peer_v_bf,
                       last, window=(N_META, seq))
    return final_norm(h3, pending.reshape(bsz, seq, d), final_norm_gain, N_META)
```

```python
import functools

import jax
import jax.numpy as jnp
from jax import lax
from jax.experimental import pallas as pl
from jax.experimental.pallas import tpu as pltpu

F32 = jnp.float32
BF16 = jnp.bfloat16

NORM_EPS = 1e-6
N_META = 16
POOL_WINDOWS = (2, 4, 8, 16)
RWKV_HEAD = 64
RWKV_W_RANK = 96
RWKV_A_RANK = 96
RWKV_G_RANK = 256
RWKV_GN_EPS = 64e-5
RWKV_CHUNK = 64
SSM_GROUP = 16
SSM_STATE = 64
SSM_CHUNK = 16
PEER_KEYS = 128
PEER_HEADS = 8
PEER_TOPK = 16
LANES = 128
SEQ_ALIGN = 128

NEG_INF = float("-inf")


def _params(sem, vmem_mb):
    return pltpu.CompilerParams(dimension_semantics=sem, vmem_limit_bytes=vmem_mb << 20)


def _split2(x):
    hi = x.astype(BF16)
    lo = (x - hi.astype(F32)).astype(BF16)
    return hi, lo


def _dot(a, b):
    return jnp.dot(a, b, preferred_element_type=F32)


def _dot_nt(a, b):
    return lax.dot_general(a, b, (((1,), (1,)), ((), ())), preferred_element_type=F32)


def _dot_acc(a, b):
    ah, al = _split2(a)
    bh, bl = _split2(b)
    return _dot(ah, bh) + _dot(al, bh) + _dot(ah, bl)


def _dot_nt_acc(a, b):
    ah, al = _split2(a)
    bh, bl = _split2(b)
    return _dot_nt(ah, bh) + _dot_nt(al, bh) + _dot_nt(ah, bl)


def _gelu(x):
    return 0.5 * x * (1.0 + lax.erf(x * 0.7071067811865476))


def _sigmoid(x):
    return 1.0 / (1.0 + jnp.exp(-x))


def _pick_tile(n, prefs):
    for t in prefs:
        if n % t == 0:
            return t
    return n


def _rms_norm_body(*refs, with_transpose, with_delta, windowed=False):
    refs = list(refs)
    x = refs.pop(0)[0] if windowed else refs.pop(0)[...]
    if with_delta:
        x = x + refs.pop(0)[...]
    g_ref, o_ref = refs.pop(0), refs.pop(0)
    inv = lax.rsqrt(jnp.mean(x * x, axis=-1, keepdims=True) + NORM_EPS)
    xn = (x * inv) * g_ref[...]
    o_ref[...] = xn.astype(BF16)
    if with_transpose:
        refs.pop(0)[...] = xn.T.astype(BF16)
    if with_delta:
        refs.pop(0)[...] = x


def rms_norm_bf16(x, gain, *, delta=None, with_transpose=False):
    t, d = x.shape
    tm = _pick_tile(t, (256, 128, 64, 32, 16, 8))
    row = pl.BlockSpec((tm, d), lambda i: (i, 0))
    args = [x] + ([delta] if delta is not None else []) + [gain.reshape(1, d)]
    in_specs = [row] * (len(args) - 1) + [pl.BlockSpec((1, d), lambda i: (0, 0))]
    out_shape = [jax.ShapeDtypeStruct((t, d), BF16)]
    out_specs = [row]
    if with_transpose:
        out_shape.append(jax.ShapeDtypeStruct((d, t), BF16))
        out_specs.append(pl.BlockSpec((d, tm), lambda i: (0, i)))
    if delta is not None:
        out_shape.append(jax.ShapeDtypeStruct((t, d), F32))
        out_specs.append(row)
    res = pl.pallas_call(
        functools.partial(_rms_norm_body, with_transpose=with_transpose,
                          with_delta=delta is not None),
        grid=(t // tm,),
        in_specs=in_specs,
        out_specs=out_specs,
        out_shape=out_shape,
        compiler_params=_params(("parallel",), 48),
        name="rms_norm",
    )(*args)
    return res if len(res) > 1 else res[0]


def rms_norm_rows(x3, gain, first, count):
    bsz, _, d = x3.shape
    tm = _pick_tile(count, (256, 128, 64, 32, 16, 8))
    nt = count // tm
    assert first % 8 == 0
    window = pl.BlockSpec((pl.Element(1), pl.Element(tm), pl.Element(d)),
                          lambda b, i: (b, pl.multiple_of(first + i * tm, 8), 0))
    return pl.pallas_call(
        functools.partial(_rms_norm_body, with_transpose=True, with_delta=False, windowed=True),
        grid=(bsz, nt),
        in_specs=[window, pl.BlockSpec((1, d), lambda b, i: (0, 0))],
        out_specs=[pl.BlockSpec((tm, d), lambda b, i: (b * nt + i, 0)),
                   pl.BlockSpec((d, tm), lambda b, i: (0, b * nt + i))],
        out_shape=[jax.ShapeDtypeStruct((bsz * count, d), BF16),
                   jax.ShapeDtypeStruct((d, bsz * count), BF16)],
        compiler_params=_params(("parallel", "parallel"), 48),
        name="rms_norm_rows",
    )(x3, gain.reshape(1, d))


def _row_tile(t):
    for cand in (1088, 1024, 768, 512, 384, 256, 128, 64, 32, 16):
        if t % cand == 0:
            return cand
    return t


def _matmul_body(a_ref, w_ref, o_ref):
    o_ref[...] = _dot(a_ref[...], w_ref[...].astype(BF16)).astype(o_ref.dtype)


def matmul(a, w_all, layer, *, n=None, out_dtype=F32):
    t, k = a.shape
    n = w_all.shape[2] if n is None else n
    tm = _row_tile(t)
    tn = _pick_tile(n, (512, 256, 128))
    return pl.pallas_call(
        _matmul_body,
        grid=(t // tm, n // tn),
        in_specs=[pl.BlockSpec((tm, k), lambda i, j: (i, 0)),
                  pl.BlockSpec((None, k, tn), lambda i, j: (layer, 0, j))],
        out_specs=pl.BlockSpec((tm, tn), lambda i, j: (i, j)),
        out_shape=jax.ShapeDtypeStruct((t, n), out_dtype),
        compiler_params=_params(("parallel", "arbitrary"), 56),
        name="matmul",
    )(a, w_all)


def _proj2_body(a1_ref, a2_ref, w1_ref, w2_ref, h_ref, o_ref):
    o_ref[...] = (h_ref[...] + _dot(a1_ref[...], w1_ref[...].astype(BF16))
                  + _dot(a2_ref[...], w2_ref[...].astype(BF16)))


def proj2_residual(a1, a2, w_all, layer, h):
    t, k1 = a1.shape
    k2 = a2.shape[1]
    n = w_all.shape[2]
    assert k1 == k2
    tm = _row_tile(t)
    tn = _pick_tile(n, (512, 256, 128))
    return pl.pallas_call(
        _proj2_body,
        grid=(t // tm, n // tn),
        in_specs=[
            pl.BlockSpec((tm, k1), lambda i, j: (i, 0)),
            pl.BlockSpec((tm, k2), lambda i, j: (i, 0)),
            pl.BlockSpec((None, k1, tn), lambda i, j: (layer, 0, j)),
            pl.BlockSpec((None, k2, tn), lambda i, j: (layer, 1, j)),
            pl.BlockSpec((tm, tn), lambda i, j: (i, j)),
        ],
        out_specs=pl.BlockSpec((tm, tn), lambda i, j: (i, j)),
        out_shape=jax.ShapeDtypeStruct((t, n), F32),
        compiler_params=_params(("parallel", "arbitrary"), 56),
        name="proj2_residual",
    )(a1, a2, w_all, w_all, h)


def _glu_body(y_ref, wa_ref, wb_ref, h_ref, o_ref):
    y = y_ref[...]
    ga = _dot(y, wa_ref[...].astype(BF16))
    gb = _dot(y, wb_ref[...].astype(BF16))
    o_ref[...] = h_ref[...] + ga * _sigmoid(gb)


def glu_residual(y, w_all, layer, h):
    t, k = y.shape
    n = w_all.shape[2] // 2
    tm = _row_tile(t)
    tn = _pick_tile(n, (512, 256, 128))
    nb = n // tn
    return pl.pallas_call(
        _glu_body,
        grid=(t // tm, nb),
        in_specs=[
            pl.BlockSpec((tm, k), lambda i, j: (i, 0)),
            pl.BlockSpec((None, k, tn), lambda i, j: (layer, 0, j)),
            pl.BlockSpec((None, k, tn), lambda i, j: (layer, 0, j + nb)),
            pl.BlockSpec((tm, tn), lambda i, j: (i, j)),
        ],
        out_specs=pl.BlockSpec((tm, tn), lambda i, j: (i, j)),
        out_shape=jax.ShapeDtypeStruct((t, n), F32),
        compiler_params=_params(("parallel", "arbitrary"), 56),
        name="glu_residual",
    )(y, w_all, w_all, h)


def _pool_body(z_ref, w_ref, s_ref, o_ref):
    grp = pl.program_id(1)
    z = z_ref[...]
    length = z.shape[0]
    row = lax.broadcasted_iota(jnp.int32, (length, 1), 0)
    posf = (row + 1).astype(F32)

    def shifted(x, k):
        return jnp.where(row >= k, pltpu.roll(x, k, axis=0), 0.0)

    for gi, win in enumerate(POOL_WINDOWS):

        @pl.when(grp == gi)
        def _(win=win):
            acc = z
            span = 1
            while span < win:
                acc = acc + shifted(acc, span)
                span *= 2
            cnt = jnp.minimum(posf, float(win))
            m = acc / cnt - z
            y = _dot(m.astype(BF16), w_ref[...])
            o_ref[...] = (y * s_ref[...]).astype(o_ref.dtype)


def pool_mixer(z3, pool_w, pool_scale):
    bsz, length, _ = z3.shape
    ng, c, _ = pool_w.shape
    return pl.pallas_call(
        _pool_body,
        grid=(bsz, ng),
        in_specs=[
            pl.BlockSpec((None, length, c), lambda b, g: (b, 0, g)),
            pl.BlockSpec((None, c, c), lambda b, g: (g, 0, 0)),
            pl.BlockSpec((1, c), lambda b, g: (0, g)),
        ],
        out_specs=pl.BlockSpec((None, length, c), lambda b, g: (b, 0, g)),
        out_shape=jax.ShapeDtypeStruct((bsz, length, ng * c), BF16),
        compiler_params=_params(("parallel", "arbitrary"), 48),
        name="pool_mixer",
    )(z3, pool_w.astype(BF16), pool_scale.reshape(1, ng * c))


def _head_sum_matrix(width, head):
    shift = head.bit_length() - 1
    r = lax.broadcasted_iota(jnp.int32, (width, width), 0) >> shift
    c = lax.broadcasted_iota(jnp.int32, (width, width), 1) >> shift
    return jnp.where(r == c, 1.0, 0.0).astype(BF16)


def _head_sums(x, ones_bd):
    hi, lo = _split2(x)
    return _dot(hi, ones_bd) + _dot(lo, ones_bd)


def _rwkv_prep_body(zr_ref, zk_ref, zv_ref, zl_ref, pr_ref, pk_ref, pv_ref, pl_ref,
                    mur_ref, muk_ref, muv_ref, mul_ref, w0_ref, w2_ref, a0_ref, a2_ref, g2_ref,
                    kk_ref, ka_ref,
                    r_out, k_out, v_out, kap_out, b_out, lw_out, g_out):
    first = pl.program_id(1) == 0
    tl = zr_ref.shape[0]
    row = lax.broadcasted_iota(jnp.int32, (tl, 1), 0)

    def mix(z_ref, p_ref, mu_ref):
        z = z_ref[...]
        prev_last = jnp.where(first, 0.0, p_ref[7:8, :])
        prev = jnp.where(row == 0, prev_last, pltpu.roll(z, 1, axis=0))
        return z + (prev - z) * mu_ref[...]

    r = mix(zr_ref, pr_ref, mur_ref)
    k = mix(zk_ref, pk_ref, muk_ref)
    v = mix(zv_ref, pv_ref, muv_ref)
    lora = mix(zl_ref, pl_ref, mul_ref)
    w_lr = lora[:, 0:LANES]
    a_lr = lora[:, LANES:2 * LANES]
    g_lr = lora[:, 2 * LANES:]

    x = w0_ref[...] + _dot_acc(jnp.tanh(w_lr), w2_ref[...])
    y = -x
    softplus = jnp.maximum(y, 0.0) + jnp.log(1.0 + jnp.exp(-jnp.abs(y)))
    lw = -jnp.exp(-softplus - 0.5)
    a = _sigmoid(a0_ref[...] + _dot_acc(a_lr, a2_ref[...]))
    g = _dot_acc(_sigmoid(g_lr), g2_ref[...])

    ones_bd = _head_sum_matrix(LANES, RWKV_HEAD)
    kk = k * kk_ref[...]
    kk2 = kk * kk
    width = k.shape[1]
    ss = jnp.concatenate(
        [_head_sums(kk2[:, c:c + LANES], ones_bd) for c in range(0, width, LANES)], axis=1)
    kap = kk / jnp.maximum(jnp.sqrt(ss), 1e-12)

    r_out[...] = r.astype(r_out.dtype)
    k_out[...] = (k * (1.0 + (a - 1.0) * ka_ref[...])).astype(k_out.dtype)
    v_out[...] = v.astype(v_out.dtype)
    kap_out[...] = kap.astype(kap_out.dtype)
    b_out[...] = (kap * a).astype(b_out.dtype)
    lw_out[...] = lw
    g_out[...] = g.astype(g_out.dtype)


def rwkv_prep(z3, zl3, off_blocks, mu, mu_l, w0, w2p, a0, a2p, g2, k_k, k_a):
    bsz, length, _ = z3.shape
    width = w0.shape[-1]
    cw = 512
    ncb = width // cw
    tl = _pick_tile(length, (352, 272, 192, 176, 136, 128, 64))
    ob_r, ob_k, ob_v = off_blocks
    ob_l = 0
    t8 = tl // 8

    def cur(ob, fixed=False):
        if fixed:
            return pl.BlockSpec((None, tl, cw), lambda b, i, c: (b, i, ob))
        return pl.BlockSpec((None, tl, cw), lambda b, i, c: (b, i, ob + c))

    def prev(ob, fixed=False):
        if fixed:
            return pl.BlockSpec((None, 8, cw), lambda b, i, c: (b, jnp.maximum(i * t8 - 1, 0), ob))
        return pl.BlockSpec((None, 8, cw), lambda b, i, c: (b, jnp.maximum(i * t8 - 1, 0), ob + c))

    def vec(ob, fixed=False):
        if fixed:
            return pl.BlockSpec((1, cw), lambda b, i, c: (0, ob))
        return pl.BlockSpec((1, cw), lambda b, i, c: (0, ob + c))

    colvec = pl.BlockSpec((1, cw), lambda b, i, c: (0, c))
    out_spec = pl.BlockSpec((None, tl, cw), lambda b, i, c: (b, i, c))
    out_dtypes = [BF16, BF16, BF16, BF16, BF16, F32, BF16]
    return pl.pallas_call(
        _rwkv_prep_body,
        grid=(bsz, length // tl, ncb),
        in_specs=[
            cur(ob_r), cur(ob_k), cur(ob_v), cur(ob_l, True),
            prev(ob_r), prev(ob_k), prev(ob_v), prev(ob_l, True),
            vec(ob_r), vec(ob_k), vec(ob_v), vec(ob_l, True),
            colvec,
            pl.BlockSpec((LANES, cw), lambda b, i, c: (0, c)),
            colvec,
            pl.BlockSpec((LANES, cw), lambda b, i, c: (0, c)),
            pl.BlockSpec((RWKV_G_RANK, cw), lambda b, i, c: (0, c)),
            colvec, colvec,
        ],
        out_specs=[out_spec] * 7,
        out_shape=[jax.ShapeDtypeStruct((bsz, length, width), dt) for dt in out_dtypes],
        compiler_params=_params(("parallel", "parallel", "arbitrary"), 56),
        name="rwkv_prep",
    )(z3, z3, z3, zl3, z3, z3, z3, zl3, mu, mu, mu, mu_l,
      w0.reshape(1, width), w2p, a0.reshape(1, width), a2p, g2,
      k_k.reshape(1, width), k_a.reshape(1, width))


def _rwkv_scan_body(r_ref, k_ref, v_ref, kap_ref, b_ref, lw_ref, g_ref, rk_ref, gain_ref, bias_ref,
                    o_ref, h_ref, *, pairs):
    cn = RWKV_CHUNK
    hd = RWKV_HEAD

    @pl.when(pl.program_id(2) == 0)
    def _():
        h_ref[...] = jnp.zeros_like(h_ref)

    lane = lax.broadcasted_iota(jnp.int32, (1, LANES), 1)
    m0 = lane < hd
    rowi = lax.broadcasted_iota(jnp.int32, (cn, LANES), 0)
    coli = lax.broadcasted_iota(jnp.int32, (cn, LANES), 1) & (hd - 1)
    strict = rowi > coli
    incl = rowi >= coli
    eye2 = jnp.where(rowi == coli, 1.0, 0.0)
    tri = jnp.where(lax.broadcasted_iota(jnp.int32, (cn, cn), 0)
                    >= lax.broadcasted_iota(jnp.int32, (cn, cn), 1), 1.0, 0.0).astype(BF16)
    br = lax.broadcasted_iota(jnp.int32, (LANES, LANES), 0) >> (hd.bit_length() - 1)
    bc = lax.broadcasted_iota(jnp.int32, (LANES, LANES), 1) >> (hd.bit_length() - 1)
    bd_mask = br == bc
    ones_bd = jnp.where(bd_mask, 1.0, 0.0).astype(BF16)

    def stk(x):
        return jnp.concatenate([jnp.where(m0, x, 0.0), jnp.where(m0, 0.0, x)], axis=0).astype(BF16)

    ps = range(pairs)
    sls = [slice(p * LANES, (p + 1) * LANES) for p in ps]
    r = [r_ref[:, s].astype(F32) for s in sls]
    k = [k_ref[:, s].astype(F32) for s in sls]
    v = [v_ref[:, s].astype(F32) for s in sls]
    kap = [kap_ref[:, s].astype(F32) for s in sls]
    b = [b_ref[:, s].astype(F32) for s in sls]
    lw = [lw_ref[:, s] for s in sls]
    hbd = [h_ref[p] for p in ps]

    lsp = [_split2(x) for x in lw]
    lp = [_dot(tri, a) + _dot(tri, c) for a, c in lsp]
    lpc = [x[cn - 1:cn, :] for x in lp]
    pinv = [jnp.exp(-x) for x in lp]
    pend = [jnp.exp(c - x) for c, x in zip(lpc, lp)]
    kt = [kap[p] * jnp.exp(lp[p] - lw[p]) for p in ps]
    rt = [r[p] * jnp.exp(lp[p]) for p in ps]
    kh = [k[p] * pinv[p] for p in ps]
    bh = [b[p] * pinv[p] for p in ps]
    khc = [k[p] * pend[p] for p in ps]
    bhc = [b[p] * pend[p] for p in ps]

    ktrt = [jnp.concatenate([kt[p], rt[p]], axis=0).astype(BF16) for p in ps]
    ab_all = [_dot_nt(ktrt[p], stk(bh[p])) for p in ps]
    ak_all = [_dot_nt(ktrt[p], stk(kh[p])) for p in ps]
    n_mat = [jnp.where(strict, x[:cn], 0.0) for x in ab_all]
    a_k = [jnp.where(strict, x[:cn], 0.0) for x in ak_all]
    a_rb = [jnp.where(incl, x[cn:], 0.0) for x in ab_all]
    a_rk = [jnp.where(incl, x[cn:], 0.0) for x in ak_all]

    t_mat = [eye2 - x for x in n_mat]
    pw = n_mat
    span = 1
    while 2 * span < cn:
        pw = [_dot(x.astype(BF16), stk(x)) for x in pw]
        t_mat = [t + _dot(t.astype(BF16), stk(x)) for t, x in zip(t_mat, pw)]
        span *= 2
    t_bf = [t.astype(BF16) for t in t_mat]

    kbar = [_dot(t_bf[p], stk(kt[p])) for p in ps]
    akv = [_dot(a_k[p].astype(BF16), stk(v[p])) for p in ps]
    ubar = [_dot(t_bf[p], stk(akv[p])) for p in ps]
    hb = [x.astype(BF16) for x in hbd]
    u = [_dot(kbar[p].astype(BF16), hb[p]) + ubar[p] for p in ps]
    y = [_dot(rt[p].astype(BF16), hb[p]) + _dot(a_rk[p].astype(BF16), stk(v[p]))
         - _dot(a_rb[p].astype(BF16), stk(u[p])) for p in ps]

    kb_t = [jnp.concatenate([khc[p], -bhc[p]], axis=0).T.astype(BF16) for p in ps]
    vu = [jnp.concatenate([v[p], u[p]], axis=0).astype(BF16) for p in ps]
    g_mat = [_dot(kb_t[p], vu[p]) for p in ps]
    pc_col = [jnp.exp(jnp.broadcast_to(x, (LANES, LANES)).T) for x in lpc]
    for p in ps:
        h_ref[p] = jnp.where(bd_mask, pc_col[p] * hbd[p] + g_mat[p], 0.0)

    mean = [_head_sums(x, ones_bd) * (1.0 / hd) for x in y]
    yc = [y[p] - mean[p] for p in ps]
    var = [_head_sums(x * x, ones_bd) * (1.0 / hd) for x in yc]
    bonus = [_head_sums(r[p] * k[p] * rk_ref[:, sls[p]], ones_bd) * v[p] for p in ps]
    for p in ps:
        yn = yc[p] * lax.rsqrt(var[p] + RWKV_GN_EPS) * gain_ref[:, sls[p]] + bias_ref[:, sls[p]]
        o_ref[:, sls[p]] = ((yn + bonus[p]) * g_ref[:, sls[p]].astype(F32)).astype(o_ref.dtype)


def rwkv_scan(r, k, v, kap, b, lw, g, r_k, gn_gain, gn_bias):
    bsz, length, width = r.shape
    cw = _pick_tile(width, (2048, 1024, 512))
    pairs = cw // LANES
    seq = pl.BlockSpec((None, RWKV_CHUNK, cw), lambda bi, c, t: (bi, t, c))
    vec = pl.BlockSpec((1, cw), lambda bi, c, t: (0, c))
    return pl.pallas_call(
        functools.partial(_rwkv_scan_body, pairs=pairs),
        grid=(bsz, width // cw, length // RWKV_CHUNK),
        in_specs=[seq] * 7 + [vec] * 3,
        out_specs=seq,
        out_shape=jax.ShapeDtypeStruct((bsz, length, width), BF16),
        scratch_shapes=[pltpu.VMEM((pairs, LANES, LANES), F32)],
        compiler_params=_params(("parallel", "parallel", "arbitrary"), 48),
        name="rwkv_scan",
    )(r, k, v, kap, b, lw, g, r_k.reshape(1, width), gn_gain.reshape(1, width),
      gn_bias.reshape(1, width))


def _s5_param_body(lr_ref, li_ref, ldt_ref, btr_ref, bti_ref, cr_ref, ci_ref,
                   kst_ref, etr_ref, eti_ref, car_ref, cai_ref, a16r_ref, a16i_ref, *, groups):
    nc = SSM_CHUNK
    npow = 24
    tau = lax.broadcasted_iota(jnp.int32, (npow, SSM_STATE), 0).astype(F32)
    for gidx in range(groups):
        lr = lr_ref[gidx]
        li = li_ref[gidx]
        dt = jnp.exp(ldt_ref[gidx])
        mag = jnp.exp(tau * (lr * dt))
        ang = tau * (li * dt)
        pr = mag * jnp.cos(ang)
        pi = mag * jnp.sin(ang)
        ar = pr[1:2]
        ai = pi[1:2]
        den = lr * lr + li * li
        fr = ((ar - 1.0) * lr + ai * li) / den
        fi = (ai * lr - (ar - 1.0) * li) / den
        btr = btr_ref[gidx]
        bti = bti_ref[gidx]
        bbr = fr * btr - fi * bti
        bbi = fr * bti + fi * btr
        cr = cr_ref[gidx]
        ci = ci_ref[gidx]
        etr = jnp.concatenate([pr[s:s + 1] * bbr - pi[s:s + 1] * bbi for s in range(nc)], axis=0)
        eti = jnp.concatenate([pr[s:s + 1] * bbi + pi[s:s + 1] * bbr for s in range(nc)], axis=0)
        car = jnp.concatenate([pr[s:s + 1] * cr - pi[s:s + 1] * ci for s in range(1, nc + 1)], axis=0)
        cai = jnp.concatenate([pr[s:s + 1] * ci + pi[s:s + 1] * cr for s in range(1, nc + 1)], axis=0)
        kst_ref[gidx] = _dot_nt_acc(etr, cr) - _dot_nt_acc(eti, ci)
        etr_ref[gidx] = etr
        eti_ref[gidx] = eti
        car_ref[gidx] = car
        cai_ref[gidx] = cai
        a16r_ref[gidx] = jnp.broadcast_to(pr[nc:nc + 1], (8, SSM_STATE))
        a16i_ref[gidx] = jnp.broadcast_to(pi[nc:nc + 1], (8, SSM_STATE))


def s5_params(lam_re, lam_im, log_dt, b_re, b_im, c_re, c_im):
    ng, ns = lam_re.shape
    gc = SSM_GROUP
    gb = 8
    rows = SSM_CHUNK * gc

    def spec(*dims):
        return pl.BlockSpec((gb,) + dims, lambda i: (i,) + (0,) * len(dims))

    outs = [((rows, gc), F32)] + [((rows, ns), F32)] * 4 + [((8, ns), F32)] * 2
    return pl.pallas_call(
        functools.partial(_s5_param_body, groups=gb),
        grid=(ng // gb,),
        in_specs=[spec(1, ns), spec(1, ns), spec(1, 1), spec(gc, ns), spec(gc, ns), spec(gc, ns),
                  spec(gc, ns)],
        out_specs=[spec(*s) for s, _ in outs],
        out_shape=[jax.ShapeDtypeStruct((ng,) + s, dt) for s, dt in outs],
        compiler_params=_params(("parallel",), 32),
        name="s5_params",
    )(lam_re.reshape(ng, 1, ns), lam_im.reshape(ng, 1, ns), log_dt.reshape(ng, 1, 1),
      jnp.swapaxes(b_re, 1, 2), jnp.swapaxes(b_im, 1, 2), c_re, c_im)


def _s5_body(x_ref, toep_ref, wsr_ref, wsi_ref, wor_ref, woi_ref, a16r_ref, a16i_ref, d_ref,
             o_ref, sre_ref, sim_ref, xre_ref, xim_ref):
    rows = x_ref.shape[1]
    nchunk = rows // 8
    x0 = x_ref[0]
    x1 = x_ref[1]
    sre_ref[...] = _dot(x0, wsr_ref[0]) + _dot(x1, wsr_ref[1])
    sim_ref[...] = _dot(x0, wsi_ref[0]) + _dot(x1, wsi_ref[1])
    ar = a16r_ref[...]
    ai = a16i_ref[...]

    def step(c, carry):
        xr, xi = carry
        off = pl.multiple_of(c * 8, 8)
        xre_ref[pl.ds(off, 8), :] = xr
        xim_ref[pl.ds(off, 8), :] = xi
        sr = sre_ref[pl.ds(off, 8), :]
        si = sim_ref[pl.ds(off, 8), :]
        return ar * xr - ai * xi + sr, ar * xi + ai * xr + si

    zero = jnp.zeros((8, LANES), F32)
    lax.fori_loop(0, nchunk, step, (zero, zero))
    xr = xre_ref[...].astype(BF16)
    xi = xim_ref[...].astype(BF16)
    for q, xq in enumerate((x0, x1)):
        y = (_dot(xq, toep_ref[q]) + _dot(xr, wor_ref[q]) + _dot(xi, woi_ref[q])
             + xq.astype(F32) * d_ref[q])
        o_ref[q] = _gelu(y).astype(o_ref.dtype)


def s5_apply(xg, toep, wsr, wsi, wor, woi, a16r, a16i, dflat):
    ng, rows, cols = xg.shape

    def spec(*dims):
        return pl.BlockSpec((2,) + dims, lambda i: (i,) + (0,) * len(dims))

    return pl.pallas_call(
        _s5_body,
        grid=(ng // 2,),
        in_specs=[spec(rows, cols), spec(cols, cols), spec(cols, LANES), spec(cols, LANES),
                  spec(LANES, cols), spec(LANES, cols),
                  pl.BlockSpec((None, 8, LANES), lambda i: (i, 0, 0)),
                  pl.BlockSpec((None, 8, LANES), lambda i: (i, 0, 0)),
                  spec(1, cols)],
        out_specs=spec(rows, cols),
        out_shape=jax.ShapeDtypeStruct((ng, rows, cols), BF16),
        scratch_shapes=[pltpu.VMEM((rows, LANES), F32)] * 4,
        compiler_params=_params(("parallel",), 32),
        name="s5_apply",
    )(xg, toep, wsr, wsi, wor, woi, a16r, a16i, dflat)


def s5_mixer(u3, lam_re, lam_im, log_dt, b_re, b_im, c_re, c_im, d_skip):
    bsz, length, width = u3.shape
    ng, ns = lam_re.shape
    gc = SSM_GROUP
    nc = SSM_CHUNK
    kst, etr, eti, car, cai, a16r, a16i = s5_params(lam_re, lam_im, log_dt, b_re, b_im, c_re, c_im)

    k4 = kst.reshape(ng, nc, gc, gc)
    s_idx = jnp.arange(nc)[:, None]
    t_idx = jnp.arange(nc)[None, :]
    lag = t_idx - s_idx
    toep = jnp.where((lag >= 0)[None, :, :, None, None], k4[:, jnp.clip(lag, 0, nc - 1)], 0.0)
    toep = toep.transpose(0, 1, 3, 2, 4).reshape(ng, nc * gc, nc * gc).astype(BF16)
    odd = (jnp.arange(ng) % 2 == 1)[:, None, None]

    def lanes_by_parity(w):
        z = jnp.zeros_like(w)
        return jnp.where(odd, jnp.concatenate([z, w], -1), jnp.concatenate([w, z], -1))

    def flip_lag(e):
        return e.reshape(ng, nc, gc, ns)[:, ::-1].reshape(ng, nc * gc, ns)

    wsr = lanes_by_parity(flip_lag(etr)).astype(BF16)
    wsi = lanes_by_parity(flip_lag(eti)).astype(BF16)
    wor = jnp.swapaxes(lanes_by_parity(car), 1, 2).astype(BF16)
    woi = jnp.swapaxes(lanes_by_parity(-cai), 1, 2).astype(BF16)

    def pair_lanes(a):
        return a.reshape(ng // 2, 2, 8, ns).transpose(0, 2, 1, 3).reshape(ng // 2, 8, 2 * ns)

    dflat = jnp.tile(d_skip.reshape(ng, 1, gc), (1, 1, nc))

    nchunk = length // nc
    x = u3.reshape(bsz, nchunk, nc, ng, gc).transpose(3, 1, 0, 2, 4)
    x = jnp.pad(x, ((0, 0), (0, 0), (0, 8 - bsz), (0, 0), (0, 0))).reshape(ng, nchunk * 8, nc * gc)
    y = s5_apply(x, toep, wsr, wsi, wor, woi, pair_lanes(a16r), pair_lanes(a16i), dflat)
    y = y.reshape(ng, nchunk, 8, nc, gc)[:, :, :bsz].transpose(2, 1, 3, 0, 4)
    return y.reshape(bsz * length, width)


def _topk_rows(s, k, stable):
    rows = lax.broadcasted_iota(jnp.int32, s.shape, 0).astype(F32)
    tops, idxs = [], []
    cur = s
    rank = jnp.full_like(s, float(PEER_KEYS - 1))
    for i in range(k):
        m = jnp.max(cur, axis=0, keepdims=True)
        hit = cur >= m
        if stable:
            first = jnp.min(jnp.where(hit, rows, float(s.shape[0])), axis=0, keepdims=True)
            hit = rows == first
            idxs.append(first)
        tops.append(m)
        rank = jnp.where(hit, float(i), rank)
        cur = jnp.where(hit, NEG_INF, cur)
    removed = jnp.sum(jnp.where(cur == NEG_INF, 1.0, 0.0), axis=0, keepdims=True)
    return tops, idxs, rank, removed


def _route_unit(s1, s2, stable):
    nk = PEER_KEYS
    k = PEER_TOPK
    wide, tall = 4, 3
    assert k == 16 and (wide + 1) * (tall + 1) > k and 2 * 9 > k
    a_idx = lax.broadcasted_iota(jnp.int32, (k, 1), 0)
    rho = lax.broadcasted_iota(jnp.int32, (k + 8 * (wide - 1) + k * tall, 1), 0)
    mid = rho - k
    col = rho - (k + 8 * (wide - 1))
    flat = jnp.where(rho < k, rho,
                     jnp.where(col < 0, (1 + (mid >> 3)) * k + (mid & 7),
                               (col & (k - 1)) * k + (col >> 4))).astype(F32)
    key_rows = lax.broadcasted_iota(jnp.int32, (nk, LANES), 0).astype(F32)
    t1, idx1, _, rem1 = _topk_rows(s1, k, stable)
    t2, _, rank2, rem2 = _topk_rows(s2, k, stable)
    top1 = jnp.concatenate(t1, axis=0)
    top2 = jnp.concatenate(t2, axis=0)
    row_blocks = [t1[0] + top2] + [t1[a] + top2[0:8] for a in range(1, wide)]
    col_blocks = [jnp.where(a_idx >= wide, top1 + t2[b], NEG_INF) for b in range(tall)]
    cand = jnp.concatenate(row_blocks + col_blocks, axis=0)
    best = t1[0] + t2[0]
    zsum = jnp.zeros_like(best)
    cur = cand
    taken = jnp.zeros_like(cand)
    for _ in range(k):
        m = jnp.max(cur, axis=0, keepdims=True)
        hit = cur >= m
        if stable:
            first = jnp.min(jnp.where(hit, flat, float(k * k)), axis=0, keepdims=True)
            hit = flat == first
        zsum = zsum + jnp.exp(m - best)
        taken = jnp.where(hit, 1.0, taken)
        cur = jnp.where(hit, NEG_INF, cur)
    ntaken = jnp.sum(taken, axis=0, keepdims=True)
    tie = jnp.abs(rem1 - k) + jnp.abs(rem2 - k) + jnp.abs(ntaken - k)
    bounds = [0, k] + [k + 8 * a for a in range(1, wide)]
    kept_rows = [jnp.sum(taken[bounds[a]:bounds[a + 1]], axis=0, keepdims=True) for a in range(wide)]
    base = bounds[-1]
    kept_cols = taken[base:base + k]
    for b in range(1, tall):
        kept_cols = kept_cols + taken[base + b * k:base + (b + 1) * k]
    cnt = jnp.zeros_like(s1)
    for a in range(k):
        kept_a = kept_rows[a] if a < wide else kept_cols[a:a + 1]
        same = (key_rows == idx1[a]) if stable else (s1 == t1[a])
        cnt = jnp.where(same, kept_a, cnt)
    e1 = jnp.exp(s1 - t1[0]) / zsum
    e2 = jnp.exp(s2 - t2[0])
    return cnt, rank2, e1, e2, tie


def _peer_route_body(q_ref, keys_ref, cnt_ref, e1_ref, r2_ref, e2_ref):
    nk = PEER_KEYS
    tm = q_ref.shape[0]
    for h in range(PEER_HEADS):
        s1_all = _dot_nt_acc(keys_ref[h, 0], q_ref[:, (2 * h) * nk:(2 * h + 1) * nk])
        s2_all = _dot_nt_acc(keys_ref[h, 1], q_ref[:, (2 * h + 1) * nk:(2 * h + 2) * nk])
        for c in range(tm // LANES):
            ts = slice(c * LANES, (c + 1) * LANES)
            s1 = s1_all[:, ts]
            s2 = s2_all[:, ts]

            def store(vals, h=h, ts=ts):
                cnt, rank2, e1, e2 = vals
                cnt_ref[h, :, ts] = cnt
                r2_ref[h, :, ts] = rank2.astype(BF16)
                e1_ref[h, :, ts] = e1
                e2_ref[h, :, ts] = e2.astype(BF16)

            fast = _route_unit(s1, s2, stable=False)
            store(fast[:4])
            has_tie = jnp.max(fast[4]) > 0.0

            @pl.when(has_tie)
            def _(s1=s1, s2=s2, store=store):
                store(_route_unit(s1, s2, stable=True)[:4])


def peer_route(q, keys):
    t = q.shape[0]
    tm = _pick_tile(t, (256, 128))
    assert tm % LANES == 0, "the route kernel walks its token tile in 128-lane chunks"
    big = pl.BlockSpec((PEER_HEADS, PEER_KEYS, tm), lambda i: (0, 0, i))
    sds = jax.ShapeDtypeStruct((PEER_HEADS, PEER_KEYS, t), F32)
    sds_bf = jax.ShapeDtypeStruct((PEER_HEADS, PEER_KEYS, t), BF16)
    return pl.pallas_call(
        _peer_route_body,
        grid=(t // tm,),
        in_specs=[pl.BlockSpec((tm, q.shape[1]), lambda i: (i, 0)),
                  pl.BlockSpec(keys.shape, lambda i: (0, 0, 0, 0))],
        out_specs=[big, big, big, big],
        out_shape=[sds, sds, sds_bf, sds_bf],
        compiler_params=_params(("parallel",), 48),
        name="peer_route",
    )(q, keys)


PEER_SLAB = 256


PEER_TILE = 512
SUBLANES = 8


def _peer_dense_body(hnt_ref, u_ref, v_ref, r2_ref, e2_ref, cnt_ref, e1_ref, o_ref, *, te):
    nk = PEER_KEYS
    rows_per_tile = te // nk
    rows_per_slab = PEER_SLAB // nk
    nslab = te // PEER_SLAB
    j = pl.program_id(1)
    base = (j % (SUBLANES // rows_per_tile)) * rows_per_tile

    @pl.when(j == 0)
    def _():
        o_ref[...] = jnp.zeros_like(o_ref)

    def gates(s, hold):
        out = []
        zero = jnp.zeros((), BF16)
        for r in range(rows_per_slab):
            row = base + s * rows_per_slab + r
            gate = None
            for h in range(PEER_HEADS):
                cnt_row = cnt_ref[h, pl.ds(row, 1), :]
                if hold is not None:
                    cnt_row = cnt_row + hold
                keep = r2_ref[h] < cnt_row.astype(BF16)
                term = jnp.where(keep, e2_ref[h] * e1_ref[h, pl.ds(row, 1), :].astype(BF16), zero)
                gate = term if gate is None else gate + term
            out.append(gate.astype(F32))
        return out

    hnt = hnt_ref[...]
    acts = [_dot(u_ref[s * PEER_SLAB:(s + 1) * PEER_SLAB, :], hnt) for s in range(nslab)]
    total = None
    hold = None
    for s in range(nslab):
        gate_blocks = gates(s, hold)
        parts = [gate_blocks[r] * _gelu(acts[s][r * nk:(r + 1) * nk, :])
                 for r in range(rows_per_slab)]
        w = jnp.concatenate(parts, axis=0).T.astype(BF16)
        c = _dot(w, v_ref[s * PEER_SLAB:(s + 1) * PEER_SLAB, :])
        hold = 0.0 * parts[-1][0:1, :]
        total = c if total is None else total + c
    o_ref[...] += total


def peer_dense(hn_t, u_all, v_all, layer, rank2, e2, cnt, e1):
    d, t = hn_t.shape
    ne = u_all.shape[1]
    tm = _pick_tile(t, (512, 256, 128))
    te = min(PEER_TILE, ne)
    rows_per_tile = te // PEER_KEYS
    tiles_per_block = SUBLANES // rows_per_tile
    big = pl.BlockSpec((PEER_HEADS, PEER_KEYS, tm), lambda i, j: (0, 0, i))
    rows = pl.BlockSpec((PEER_HEADS, SUBLANES, tm), lambda i, j: (0, j // tiles_per_block, i))
    table = pl.BlockSpec((None, te, d), lambda i, j: (layer, j, 0))
    return pl.pallas_call(
        functools.partial(_peer_dense_body, te=te),
        grid=(t // tm, ne // te),
        in_specs=[pl.BlockSpec((d, tm), lambda i, j: (0, i)), table, table, big, big, rows, rows],
        out_specs=pl.BlockSpec((tm, d), lambda i, j: (i, 0)),
        out_shape=jax.ShapeDtypeStruct((t, d), F32),
        compiler_params=_params(("parallel", "arbitrary"), 60),
        name="peer_dense",
    )(hn_t, u_all, v_all, rank2, e2, cnt, e1)


def peer_ffn(h, gain, w_q_all, keys, u_all, v_all, layer, window=None):
    if window is None:
        hn, hn_t = rms_norm_bf16(h, gain, with_transpose=True)
    else:
        hn, hn_t = rms_norm_rows(h, gain, *window)
    q = matmul(hn, w_q_all, layer)
    cnt, e1, rank2, e2 = peer_route(q, keys)
    return peer_dense(hn_t, u_all, v_all, layer, rank2, e2, cnt, e1)


def _final_norm_body(x_ref, dx_ref, g_ref, o_ref):
    x = x_ref[0] + dx_ref[0]
    inv = lax.rsqrt(jnp.mean(x * x, axis=-1, keepdims=True) + NORM_EPS)
    o_ref[...] = (x * inv) * g_ref[...]


def final_norm(h3, delta3, gain, first):
    bsz, _, d = h3.shape
    count = delta3.shape[1]
    tm = _pick_tile(count, (256, 128, 64, 32, 16, 8))
    assert first % 8 == 0
    rows = pl.BlockSpec((pl.Element(1), pl.Element(tm), pl.Element(d)),
                        lambda b, i: (b, pl.multiple_of(first + i * tm, 8), 0))
    return pl.pallas_call(
        _final_norm_body,
        grid=(bsz, count // tm),
        in_specs=[rows, pl.BlockSpec((1, tm, d), lambda b, i: (b, i, 0)),
                  pl.BlockSpec((1, d), lambda b, i: (0, 0))],
        out_specs=pl.BlockSpec((None, tm, d), lambda b, i: (b, i, 0)),
        out_shape=jax.ShapeDtypeStruct((bsz, count, d), F32),
        compiler_params=_params(("parallel", "parallel"), 32),
        name="final_norm",
    )(h3, delta3, gain.reshape(1, d))


def _even_layer_weights(w_in, shift_mu, w2, a2, width):
    pool_w = w_in.shape[1] - shift_mu.shape[0]
    c0 = pool_w + 3 * width
    c1 = c0 + RWKV_W_RANK
    c2 = c1 + RWKV_A_RANK
    d = w_in.shape[0]
    zw = jnp.zeros((d, LANES - RWKV_W_RANK), w_in.dtype)
    za = jnp.zeros((d, LANES - RWKV_A_RANK), w_in.dtype)
    w_lora = jnp.concatenate([w_in[:, c0:c1], zw, w_in[:, c1:c2], za, w_in[:, c2:]], axis=1)[None]
    s0 = 3 * width
    s1 = s0 + RWKV_W_RANK
    s2 = s1 + RWKV_A_RANK
    mu_main = jnp.concatenate([jnp.zeros((pool_w,), F32), shift_mu[:s0]]).reshape(1, -1)
    mu_lora = jnp.concatenate([
        shift_mu[s0:s1], jnp.zeros((LANES - RWKV_W_RANK,), F32), shift_mu[s1:s2],
        jnp.zeros((LANES - RWKV_A_RANK,), F32), shift_mu[s2:]]).reshape(1, -1)
    w2p = jnp.pad(w2, ((0, LANES - RWKV_W_RANK), (0, 0)))
    a2p = jnp.pad(a2, ((0, LANES - RWKV_A_RANK), (0, 0)))
    return c0, w_lora, mu_main, mu_lora, w2p, a2p, pool_w


def kernel(x, meta_tokens, mix_norm_gain, ffn_norm_gain, final_norm_gain, w_in_even, pool_w, pool_scale, shift_mu, rwkv_w0, rwkv_w2, rwkv_a0, rwkv_a2, rwkv_g2, rwkv_k_k, rwkv_k_a, rwkv_r_k, rwkv_gn_gain, rwkv_gn_bias, w_out_even, w_in_odd, ssm_lam_re, ssm_lam_im, ssm_log_dt, ssm_b_re, ssm_b_im, ssm_c_re, ssm_c_im, ssm_d, w_glu, peer_w_q, peer_sub_keys, peer_u, peer_v):
    bsz, seq, d = x.shape
    depth = mix_norm_gain.shape[0]
    real = N_META + seq
    length = -(-real // SEQ_ALIGN) * SEQ_ALIGN
    meta = jnp.broadcast_to(meta_tokens[None].astype(x.dtype), (bsz, N_META, d))
    h = jnp.concatenate([meta, x, jnp.zeros((bsz, length - real, d), x.dtype)], axis=1)
    h = h.reshape(bsz * length, d)
    peer_u_bf = peer_u.astype(BF16)
    peer_v_bf = peer_v.astype(BF16)

    pending = None
    for layer in range(depth):
        if pending is None:
            hn = rms_norm_bf16(h, mix_norm_gain[layer])
        else:
            hn, h = rms_norm_bf16(h, mix_norm_gain[layer], delta=pending)
        if layer % 2 == 0:
            i = layer // 2
            width = rwkv_w0.shape[-1]
            main_cols, w_lora, mu_main, mu_lora, w2p, a2p, pool_cols = _even_layer_weights(
                w_in_even[i], shift_mu[i], rwkv_w2[i], rwkv_a2[i], width)
            z3 = matmul(hn, w_in_even, i, n=main_cols).reshape(bsz, length, -1)
            zl3 = matmul(hn, w_lora, 0).reshape(bsz, length, -1)
            y_pool = pool_mixer(z3, pool_w[i], pool_scale[i])
            ob = pool_cols // 512
            wb = width // 512
            r, k, v, kap, b, lw, g = rwkv_prep(
                z3, zl3, (ob, ob + wb, ob + 2 * wb), mu_main, mu_lora, rwkv_w0[i], w2p,
                rwkv_a0[i], a2p, rwkv_g2[i], rwkv_k_k[i], rwkv_k_a[i])
            y_rwkv = rwkv_scan(r, k, v, kap, b, lw, g, rwkv_r_k[i], rwkv_gn_gain[i],
                               rwkv_gn_bias[i])
            h = proj2_residual(y_pool.reshape(bsz * length, -1), y_rwkv.reshape(bsz * length, -1),
                               w_out_even, i, h)
        else:
            j = layer // 2
            u = matmul(hn, w_in_odd, j, out_dtype=BF16)
            y = s5_mixer(u.reshape(bsz, length, -1), ssm_lam_re[j], ssm_lam_im[j], ssm_log_dt[j],
                         ssm_b_re[j], ssm_b_im[j], ssm_c_re[j], ssm_c_im[j], ssm_d[j])
            h = glu_residual(y, w_glu, j, h)
        if layer + 1 < depth:
            pending = peer_ffn(h, ffn_norm_gain[layer], peer_w_q, peer_sub_keys[layer], peer_u_bf,
                               peer_v_bf, layer)
    h3 = h.reshape(bsz, length, d)
    last = depth - 1
    pending = peer_ffn(h3, ffn_norm_gain[last], peer_w_q, peer_sub_keys[last], peer_u_bf, peer_v_bf,
                       last, window=(N_META, seq))
    return final_norm(h3, pending.reshape(bsz, seq, d), final_norm_gain, N_META)
```
